```python
import jax, jax.numpy as jnp
from jax import lax
import numpy as np

D_MODEL = 1024
BATCH = 8
SEQ = 8192
DEPTH = 1
DEC_BATCH = 32
DEC_SEQ = 32
PAST_LEN = 4096

CHUNK = 64
N_META = 16
CONV_CH = D_MODEL
CONV_WIDTH = 31
N_HEADS = 16
N_KV_HEADS = 4
HEAD_DIM = 64
ROT_DIM = HEAD_DIM // 4
ROPE_THETA = 500000.0
IDX_HEADS = 8
IDX_DIM = 64
TOPK_MAX = 256
Q_BLOCK = 128
N_EXPERTS = 32
TOP_K_EXPERTS = 4
D_FF = D_MODEL
SWIGLU_LIMIT = 7.0
SWIGLU_ALPHA = 1.702
EPS = 1e-5
SPLITS = (2 * CONV_CH, N_HEADS * HEAD_DIM, N_KV_HEADS * HEAD_DIM, N_KV_HEADS * HEAD_DIM,
          IDX_HEADS * IDX_DIM, IDX_DIM, IDX_HEADS, 2 * D_MODEL)
D_IN = sum(SPLITS)

kernel_name = 'hybrid_conformer_dsa_moe_stream_step'


def rmsnorm(x, g):
    xf = x.astype(jnp.float32)
    y = xf * lax.rsqrt(jnp.mean(xf * xf, axis=-1, keepdims=True) + EPS)
    return (y * g.astype(jnp.float32)).astype(x.dtype)


def layernorm(x, g, b):
    xf = x.astype(jnp.float32)
    mu = jnp.mean(xf, axis=-1, keepdims=True)
    xc = xf - mu
    y = xc * lax.rsqrt(jnp.mean(xc * xc, axis=-1, keepdims=True) + EPS)
    return (y * g.astype(jnp.float32) + b.astype(jnp.float32)).astype(x.dtype)


def rope_partial(x, pos):
    half = ROT_DIM // 2
    inv = ROPE_THETA ** (-jnp.arange(half, dtype=jnp.float32) / half)
    ang = pos.astype(jnp.float32)[:, None] * inv[None, :]
    cos = jnp.cos(ang)[:, None, :]
    sin = jnp.sin(ang)[:, None, :]
    xr = x[..., :ROT_DIM].astype(jnp.float32)
    x1, x2 = xr[..., :half], xr[..., half:]
    rot = jnp.concatenate([x1 * cos - x2 * sin, x2 * cos + x1 * sin], axis=-1).astype(x.dtype)
    return jnp.concatenate([rot, x[..., ROT_DIM:]], axis=-1)


def project(x_seq, pos, g_norm_mix, w_in):
    B, S = x_seq.shape[0], x_seq.shape[1]
    h = rmsnorm(x_seq, g_norm_mix)
    p = h @ w_in
    points = [int(c) for c in np.cumsum(SPLITS)[:-1]]
    glu_in, q, k, v, qi, ki, wi, gl = jnp.split(p, points, axis=-1)
    q = rope_partial(q.reshape(B, S, N_HEADS, HEAD_DIM), pos)
    k = rope_partial(k.reshape(B, S, N_KV_HEADS, HEAD_DIM), pos)
    v = v.reshape(B, S, N_KV_HEADS, HEAD_DIM)
    qi = rope_partial(qi.reshape(B, S, IDX_HEADS, IDX_DIM), pos)
    ki = rope_partial(ki[:, :, None, :], pos)[:, :, 0, :]
    a = glu_in[..., :CONV_CH] * jax.nn.sigmoid(glu_in[..., CONV_CH:])
    return a, q, k, v, qi, ki, wi, gl


def conv_branch(a_padded, conv_w, conv_b, conv_ln_g, conv_ln_b, w_conv_out):
    c = lax.conv_general_dilated(a_padded, conv_w[:, None, :], window_strides=(1,), padding='VALID',
                                 dimension_numbers=('NWC', 'WIO', 'NWC'),
                                 feature_group_count=CONV_CH) + conv_b
    c = layernorm(c, conv_ln_g, conv_ln_b)
    return jax.nn.silu(c) @ w_conv_out


def dsa_attend(q, qi, wi, q_cid, k_all, v_all, ki_all, k_cid, k_sel):
    f32 = jnp.float32
    rel = jax.nn.relu(jnp.einsum('bthd,bsd->bths', qi.astype(f32), ki_all.astype(f32)) * IDX_DIM ** -0.5)
    score = jnp.einsum('bth,bths->bts', wi.astype(f32) * IDX_HEADS ** -0.5, rel)
    adm = k_cid[None, :] <= q_cid[:, None]
    score = jnp.where(adm[None], score, -jnp.inf)
    vals, idx = lax.top_k(score, k_sel)
    valid = vals > -jnp.inf
    gather = jax.vmap(lambda t, i: t[i])
    k_rows = gather(k_all, idx)
    v_rows = gather(v_all, idx)
    B, Tq = q.shape[0], q.shape[1]
    qg = q.reshape(B, Tq, N_KV_HEADS, N_HEADS // N_KV_HEADS, HEAD_DIM)
    s = jnp.einsum('btgrd,btkgd->btgrk', qg.astype(f32), k_rows.astype(f32)) * HEAD_DIM ** -0.5
    s = jnp.where(valid[:, :, None, None, :], s, -jnp.inf)
    p = jax.nn.softmax(s, axis=-1).astype(v_all.dtype)
    o = jnp.einsum('btgrk,btkgd->btgrd', p, v_rows)
    return o.reshape(B, Tq, N_HEADS * HEAD_DIM)


def merge(conv_out, attn_out, gate_logits, w_attn_out, w_out):
    g = jax.nn.sigmoid(gate_logits)
    return (g[..., :D_MODEL] * conv_out + g[..., D_MODEL:] * (attn_out @ w_attn_out)) @ w_out


def moe(h, w_router, b_router, w1, b1, w3, b3, w2, b2):
    logits = (h @ w_router + b_router).astype(jnp.float32)
    vals, idx = lax.top_k(logits, TOP_K_EXPERTS)
    gates = jax.nn.softmax(vals, axis=-1)
    combine = jnp.einsum('nk,nke->ne', gates,
                         jax.nn.one_hot(idx, N_EXPERTS, dtype=jnp.float32)).astype(h.dtype)
    y = jnp.zeros_like(h)
    for e in range(N_EXPERTS):
        g = jnp.minimum(h @ w1[e] + b1[e], SWIGLU_LIMIT)
        u = jnp.clip(h @ w3[e] + b3[e], -SWIGLU_LIMIT, SWIGLU_LIMIT)
        act = g * jax.nn.sigmoid(SWIGLU_ALPHA * g) * (u + 1)
        y = y + combine[:, e:e + 1] * (act @ w2[e] + b2[e])
    return y


def setup_inputs(seed: int = 0) -> dict:
    key = jax.random.key(seed)
    ks = jax.random.split(key, 32)
    nrm = lambda k, shape, scale: jax.random.normal(k, shape, jnp.float32) * scale
    return {
        'x_prompt': nrm(ks[0], (BATCH, SEQ, D_MODEL), 1.0),
        'x_sample': nrm(ks[1], (DEC_BATCH, DEC_SEQ, D_MODEL), 1.0),
        'cache_k': nrm(ks[2], (DEC_BATCH, PAST_LEN, N_KV_HEADS, HEAD_DIM), 1.0),
        'cache_v': nrm(ks[3], (DEC_BATCH, PAST_LEN, N_KV_HEADS, HEAD_DIM), 1.0),
        'cache_idx_k': nrm(ks[4], (DEC_BATCH, PAST_LEN, IDX_DIM), 1.0),
        'state_conv': nrm(ks[5], (DEC_BATCH, CONV_WIDTH - 1, CONV_CH), 0.5),
        'meta_tokens': nrm(ks[6], (N_META, D_MODEL), 1.0),
        'g_norm_mix': 1.0 + nrm(ks[7], (D_MODEL,), 0.01),
        'w_in': nrm(ks[8], (D_MODEL, D_IN), D_MODEL ** -0.5),
        'conv_w': nrm(ks[9], (CONV_WIDTH, CONV_CH), CONV_WIDTH ** -0.5),
        'conv_b': nrm(ks[10], (CONV_CH,), 0.01),
        'conv_ln_g': 1.0 + nrm(ks[11], (CONV_CH,), 0.01),
        'conv_ln_b': nrm(ks[12], (CONV_CH,), 0.01),
        'w_conv_out': nrm(ks[13], (CONV_CH, D_MODEL), CONV_CH ** -0.5),
        'w_attn_out': nrm(ks[14], (N_HEADS * HEAD_DIM, D_MODEL), (N_HEADS * HEAD_DIM) ** -0.5),
        'w_out': nrm(ks[15], (D_MODEL, D_MODEL), D_MODEL ** -0.5),
        'g_norm_ffn': 1.0 + nrm(ks[16], (D_MODEL,), 0.01),
        'w_router': nrm(ks[17], (D_MODEL, N_EXPERTS), D_MODEL ** -0.5),
        'b_router': nrm(ks[18], (N_EXPERTS,), 0.01),
        'w1': nrm(ks[19], (N_EXPERTS, D_MODEL, D_FF), D_MODEL ** -0.5),
        'b1': nrm(ks[20], (N_EXPERTS, D_FF), 0.01),
        'w3': nrm(ks[21], (N_EXPERTS, D_MODEL, D_FF), D_MODEL ** -0.5),
        'b3': nrm(ks[22], (N_EXPERTS, D_FF), 0.01),
        'w2': nrm(ks[23], (N_EXPERTS, D_FF, D_MODEL), D_FF ** -0.5),
        'b2': nrm(ks[24], (N_EXPERTS, D_MODEL), 0.01),
        'g_norm_final': 1.0 + nrm(ks[25], (D_MODEL,), 0.01),
    }


def reference(x_prompt, x_sample, cache_k, cache_v, cache_idx_k, state_conv, meta_tokens,
              g_norm_mix, w_in, conv_w, conv_b, conv_ln_g, conv_ln_b, w_conv_out, w_attn_out, w_out,
              g_norm_ffn, w_router, b_router, w1, b1, w3, b3, w2, b2, g_norm_final):
    B, DB = x_prompt.shape[0], x_sample.shape[0]
    S_P, S_N, P = x_prompt.shape[1], x_sample.shape[1], cache_k.shape[1]
    L = N_META + S_P
    meta = meta_tokens.astype(x_prompt.dtype)

    xs = jnp.concatenate([jnp.broadcast_to(meta[None], (B, N_META, D_MODEL)), x_prompt], axis=1)
    pos_p = jnp.arange(L, dtype=jnp.int32)
    cid_p = (pos_p - N_META) // CHUNK
    for _ in range(DEPTH):
        a, q, k, v, qi, ki, wi, gl = project(xs, pos_p, g_norm_mix, w_in)
        conv_p = conv_branch(jnp.pad(a, ((0, 0), (CONV_WIDTH - 1, 0), (0, 0))),
                             conv_w, conv_b, conv_ln_g, conv_ln_b, w_conv_out)
        conv_state_p = a[:, L - (CONV_WIDTH - 1):]
        k_sel_p = min(TOPK_MAX, S_P // 4)
        attn_meta = dsa_attend(q[:, :N_META], qi[:, :N_META], wi[:, :N_META], cid_p[:N_META],
                               k, v, ki, cid_p, k_sel_p)
        nb = S_P // Q_BLOCK

        def to_blocks(t):
            t = t[:, N_META:]
            return jnp.moveaxis(t.reshape((B, nb, Q_BLOCK) + t.shape[2:]), 1, 0)

        def attend_block(args):
            qb, qib, wib, cb = args
            return dsa_attend(qb, qib, wib, cb, k, v, ki, cid_p, k_sel_p)

        out_blocks = lax.map(attend_block, (to_blocks(q), to_blocks(qi), to_blocks(wi),
                                            cid_p[N_META:].reshape(nb, Q_BLOCK)))
        attn_f = jnp.moveaxis(out_blocks, 0, 1).reshape(B, S_P, N_HEADS * HEAD_DIM)
        attn_p = jnp.concatenate([attn_meta, attn_f], axis=1)
        x1_p = xs + merge(conv_p, attn_p, gl, w_attn_out, w_out)
        k_prompt, v_prompt, idxk_prompt = k, v, ki

        pos_m = jnp.arange(N_META, dtype=jnp.int32)
        _, _, k_m, v_m, _, ki_m, _, _ = project(meta[None], pos_m, g_norm_mix, w_in)
        pos_n = N_META + P + jnp.arange(S_N, dtype=jnp.int32)
        cid_n = (pos_n - N_META) // CHUNK
        a_s, q_s, k_s, v_s, qi_s, ki_s, wi_s, gl_s = project(x_sample, pos_n, g_norm_mix, w_in)
        conv_in = jnp.concatenate([state_conv, a_s], axis=1)
        conv_s = conv_branch(conv_in, conv_w, conv_b, conv_ln_g, conv_ln_b, w_conv_out)
        conv_state_s = conv_in[:, S_N:]
        k_all = jnp.concatenate([jnp.broadcast_to(k_m, (DB,) + k_m.shape[1:]), cache_k, k_s], axis=1)
        v_all = jnp.concatenate([jnp.broadcast_to(v_m, (DB,) + v_m.shape[1:]), cache_v, v_s], axis=1)
        ki_all = jnp.concatenate([jnp.broadcast_to(ki_m, (DB,) + ki_m.shape[1:]), cache_idx_k, ki_s], axis=1)
        cid_all = jnp.concatenate([(pos_m - N_META) // CHUNK,
                                   jnp.arange(P, dtype=jnp.int32) // CHUNK, cid_n])
        k_sel_s = min(TOPK_MAX, (P + S_N) // 4)
        attn_s = dsa_attend(q_s, qi_s, wi_s, cid_n, k_all, v_all, ki_all, cid_all, k_sel_s)
        x1_s = x_sample + merge(conv_s, attn_s, gl_s, w_attn_out, w_out)

        flat = jnp.concatenate([x1_p.reshape(B * L, D_MODEL), x1_s.reshape(DB * S_N, D_MODEL)], axis=0)
        flat = flat + moe(rmsnorm(flat, g_norm_ffn), w_router, b_router, w1, b1, w3, b3, w2, b2)

    out = rmsnorm(flat, g_norm_final)
    y_prompt = out[:B * L].reshape(B, L, D_MODEL)[:, N_META:]
    y_sample = out[B * L:].reshape(DB, S_N, D_MODEL)
    return (y_prompt, y_sample, k_prompt, v_prompt, idxk_prompt, conv_state_p,
            k_s, v_s, ki_s, conv_state_s)
```

```python
import functools

import jax
import jax.numpy as jnp
from jax import lax
from jax.experimental import pallas as pl
from jax.experimental.pallas import tpu as pltpu

F32, BF16, I32 = jnp.float32, jnp.bfloat16, jnp.int32

D_MODEL = 1024
CHUNK = 64
N_META = 16
CONV_CH = D_MODEL
CONV_WIDTH = 31
N_HEADS = 16
N_KV_HEADS = 4
HEAD_DIM = 64
ROT_DIM = HEAD_DIM // 4
ROPE_THETA = 500000.0
IDX_HEADS = 8
IDX_DIM = 64
TOPK_MAX = 256
N_EXPERTS = 32
TOP_K_EXPERTS = 4
D_FF = D_MODEL
SWIGLU_LIMIT = 7.0
SWIGLU_ALPHA = 1.702
EPS = 1e-5

LANES = 128
FRONT = 128
HALO = 32
VMEM_LIMIT = 56 * 1024 * 1024
NEG_INF = float("-inf")
INT_MIN = -2 ** 31
M_INIT = -1e30

ROW_TILE = 512
Q_TILE = 128
K_TILE = 512
MOE_TILE = 512


def _cparams(sem):
    return pltpu.CompilerParams(dimension_semantics=sem, vmem_limit_bytes=VMEM_LIMIT)


def _resident(arr):
    nd = arr.ndim
    return pl.BlockSpec(arr.shape, lambda *_: (0,) * nd, pipeline_mode=pl.Buffered(1))


def _rope_cols(x, c, s1, s2):
    outs = []
    for j in range(x.shape[1] // LANES):
        xj = x[:, j * LANES:(j + 1) * LANES]
        outs.append(xj * c + pltpu.roll(xj, LANES - ROT_DIM // 2, 1) * s1 + pltpu.roll(xj, ROT_DIM // 2, 1) * s2)
    return outs


def _proj_kernel(xp_ref, xs_ref, g_ref, wa_ref, wq_ref, wk_ref, wv_ref, wqi_ref, wkw_ref, wgl_ref,
                 c_ref, s1_ref, s2_ref,
                 a_ref, q_ref, k_ref, v_ref, qi_ref, kw_ref, gate_ref, *, n_first):
    x = jnp.where(pl.program_id(0) < n_first, xp_ref[...], xs_ref[...])
    h = x * lax.rsqrt(jnp.mean(x * x, axis=-1, keepdims=True) + EPS) * g_ref[...]
    hb = h.astype(BF16)
    c, s1, s2 = c_ref[...], s1_ref[...], s2_ref[...]

    def dot(w_ref):
        return jnp.dot(hb, w_ref[...], preferred_element_type=F32)

    glu = dot(wa_ref)
    a_ref[...] = glu[:, :CONV_CH] * jax.nn.sigmoid(glu[:, CONV_CH:])
    for j, col in enumerate(_rope_cols(dot(wq_ref), c, s1, s2)):
        q_ref[:, j * LANES:(j + 1) * LANES] = (col * HEAD_DIM ** -0.5).astype(BF16)
    for j, col in enumerate(_rope_cols(dot(wk_ref), c, s1, s2)):
        k_ref[:, j * LANES:(j + 1) * LANES] = col
    v_ref[...] = dot(wv_ref)
    for j, col in enumerate(_rope_cols(dot(wqi_ref), c, s1, s2)):
        qi_ref[:, j * LANES:(j + 1) * LANES] = (col * IDX_DIM ** -0.5).astype(BF16)
    pkw = dot(wkw_ref)
    lane = lax.broadcasted_iota(I32, pkw.shape, 1)
    kw_ref[...] = jnp.where(lane < IDX_DIM, _rope_cols(pkw, c, s1, s2)[0], pkw * IDX_HEADS ** -0.5)
    gate_ref[...] = jax.nn.sigmoid(dot(wgl_ref)).astype(BF16)


def _rope_tables(pos):
    half = ROT_DIM // 2
    inv = ROPE_THETA ** (-jnp.arange(half, dtype=F32) / half)
    ang = pos.astype(F32)[:, None] * inv[None, :]
    cos, sin = jnp.cos(ang), jnp.sin(ang)
    r = pos.shape[0]
    z = lambda n: jnp.zeros((r, n), F32)
    c = jnp.concatenate([cos, cos, jnp.ones((r, HEAD_DIM - ROT_DIM), F32)], axis=1)
    s1 = jnp.concatenate([-sin, z(HEAD_DIM - half)], axis=1)
    s2 = jnp.concatenate([z(half), sin, z(HEAD_DIM - ROT_DIM)], axis=1)
    two = lambda t: jnp.concatenate([t, t], axis=1)
    return two(c), two(s1), two(s2)


def _project(xp, xs, tabs, tab_map, g, ws, tr):
    n_first, n_second = xp.shape[0] // tr, xs.shape[0] // tr
    rows = xp.shape[0] + xs.shape[0]
    row = lambda w: pl.BlockSpec((tr, w), lambda i: (i, 0))
    tab = pl.BlockSpec((tr, LANES), lambda i: (tab_map(i), 0))
    in_specs = [pl.BlockSpec((tr, D_MODEL), lambda i: (jnp.minimum(i, n_first - 1), 0)),
                pl.BlockSpec((tr, D_MODEL), lambda i: (jnp.maximum(i - n_first, 0), 0)),
                _resident(g)] + [_resident(w) for w in ws] + [tab, tab, tab]
    widths = [(CONV_CH, F32), (N_HEADS * HEAD_DIM, BF16), (N_KV_HEADS * HEAD_DIM, F32), (N_KV_HEADS * HEAD_DIM, F32),
              (IDX_HEADS * IDX_DIM, BF16), (LANES, F32), (2 * D_MODEL, BF16)]
    return pl.pallas_call(
        functools.partial(_proj_kernel, n_first=n_first),
        out_shape=[jax.ShapeDtypeStruct((rows, w), dt) for w, dt in widths],
        grid=(n_first + n_second,),
        in_specs=in_specs,
        out_specs=[row(w) for w, _ in widths],
        compiler_params=_cparams(("parallel",)),
        name="proj",
    )(xp, xs, g, *ws, *tabs)


def _conv_kernel(cur_ref, halo_ref, lc_ref, cw_ref, cb_ref, lg_ref, lb_ref, wo_ref, out_ref, e_ref, c_ref, *, tr, rc):
    halo = jnp.where(pl.program_id(1) == 0, lc_ref[0], halo_ref[...])
    e_ref[0, 0:HALO, :] = halo
    e_ref[0, HALO:HALO + tr, :] = cur_ref[...]
    n = tr + HALO - 8
    for b in range(1, 8):
        e_ref[b, 0:n, :] = e_ref[0, b:b + n, :]
    first = HALO - (CONV_WIDTH - 1)

    def chunk(ci, carry):
        r0 = pl.multiple_of(ci * rc, rc)
        acc = jnp.zeros((rc, CONV_CH), F32) + cb_ref[...]
        for w in range(CONV_WIDTH):
            s = first + w
            acc = acc + e_ref[s % 8, pl.ds(r0 + 8 * (s // 8), rc), :] * cw_ref[pl.ds(w, 1), :]
        c_ref[pl.ds(r0, rc), :] = acc
        return carry

    lax.fori_loop(0, tr // rc, chunk, 0)
    c = c_ref[...]
    mu = jnp.mean(c, axis=-1, keepdims=True)
    xc = c - mu
    y = xc * lax.rsqrt(jnp.mean(xc * xc, axis=-1, keepdims=True) + EPS) * lg_ref[...] + lb_ref[...]
    act = (y * jax.nn.sigmoid(y)).astype(BF16)
    out_ref[...] = jnp.dot(act, wo_ref[...], preferred_element_type=F32).astype(BF16)


def _conv_branch(a_flat, row_off, nb, t, left_ctx, cw, cb, lg, lb, wo, tr):
    nt = t // tr
    rc = min(tr, 32)
    per_batch_ctx = left_ctx.shape[0] > 1
    cur = pl.BlockSpec((tr, CONV_CH), lambda b, i: (row_off // tr + b * nt + i, 0))
    halo = pl.BlockSpec((HALO, CONV_CH),
                        lambda b, i: (jnp.maximum((row_off + b * t + i * tr) // HALO - 1, 0), 0))
    lctx = pl.BlockSpec((1, HALO, CONV_CH), lambda b, i: (b if per_batch_ctx else 0, 0, 0))
    return pl.pallas_call(
        functools.partial(_conv_kernel, tr=tr, rc=rc),
        out_shape=jax.ShapeDtypeStruct((nb * t, D_MODEL), BF16),
        grid=(nb, nt),
        in_specs=[cur, halo, lctx, _resident(cw), _resident(cb), _resident(lg), _resident(lb), _resident(wo)],
        out_specs=pl.BlockSpec((tr, D_MODEL), lambda b, i: (b * nt + i, 0)),
        scratch_shapes=[pltpu.VMEM((8, tr + HALO, CONV_CH), F32), pltpu.VMEM((tr, CONV_CH), F32)],
        compiler_params=_cparams(("parallel", "arbitrary")),
        name="conv",
    )(a_flat, a_flat, left_ctx, cw, cb, lg, lb, wo)


def _key_to_float(u):
    key = u ^ INT_MIN
    bits = key ^ ((key >> 31) & 0x7FFFFFFF)
    return lax.bitcast_convert_type(bits, F32)


def _dsa_kernel(q_ref, qi_ref, kw_ref, k_ref, v_ref, ki_ref, o_ref, sc_ref, m_ref, acc_ref,
                *, tq, tk, k_sel, q_off, n_valid_end, idx_bits):
    i = pl.program_id(1)
    qf0 = q_off + i * tq
    klim = jnp.minimum(FRONT + ((qf0 + tq - 1) // CHUNK + 1) * CHUNK, n_valid_end)
    n_tiles = (klim + tk - 1) // tk
    ncol = tk // LANES
    kf = float(k_sel)
    group = N_HEADS // N_KV_HEADS

    kwv = kw_ref[...]
    qrow = lax.broadcasted_iota(I32, (tq, 1), 0)
    lim_row = jnp.minimum(FRONT + ((qf0 + qrow) // CHUNK + 1) * CHUNK, n_valid_end)

    def score_tile(t, carry):
        k0 = pl.multiple_of(t * tk, tk)
        kt = ki_ref[0, pl.ds(k0, tk), :]
        sc = jnp.zeros((tq, tk), F32)
        for h in range(IDX_HEADS):
            rel = lax.dot_general(qi_ref[0, h], kt, (((1,), (1,)), ((), ())), preferred_element_type=F32)
            sc = sc + kwv[:, IDX_DIM + h:IDX_DIM + h + 1] * jnp.maximum(rel, 0.0)
        j = k0 + lax.broadcasted_iota(I32, (tq, tk), 1)
        j = jnp.where(j >= FRONT - N_META, j, n_valid_end)
        sc_ref[:, pl.ds(k0, tk)] = jnp.where(j < lim_row, sc, NEG_INF)
        return carry

    lax.fori_loop(0, n_tiles, score_tile, 0)

    lane = lax.broadcasted_iota(I32, (tq, LANES), 1)

    def count(pred):
        def body(t, acc):
            k0 = pl.multiple_of(t * tk, tk)
            for c in range(ncol):
                blk = sc_ref[:, pl.ds(k0 + c * LANES, LANES)]
                acc = acc + pred(blk, k0 + c * LANES)
            return acc
        acc = lax.fori_loop(0, n_tiles, body, jnp.zeros((tq, LANES), F32))
        return jnp.broadcast_to(jnp.sum(acc, axis=1, keepdims=True), (tq, LANES))

    def bisect(it, u):
        trial = u | jnp.left_shift(jnp.int32(1), 31 - it)
        thr = _key_to_float(trial)
        cnt = count(lambda blk, j0: jnp.where(blk >= thr, 1.0, 0.0))
        return jnp.where(cnt >= kf, trial, u)

    u = lax.fori_loop(0, 32, bisect, jnp.zeros((tq, LANES), I32))
    key = jnp.maximum(u ^ INT_MIN, INT_MIN + 0x00800000)
    thr = _key_to_float(key ^ INT_MIN)
    n_ge = count(lambda blk, j0: jnp.where(blk >= thr, 1.0, 0.0))
    tied = jnp.max(n_ge) > kf

    def write_bias(sel):
        def body(t, carry):
            k0 = pl.multiple_of(t * tk, tk)
            for c in range(ncol):
                blk = sc_ref[:, pl.ds(k0 + c * LANES, LANES)]
                sc_ref[:, pl.ds(k0 + c * LANES, LANES)] = sel(blk, k0 + c * LANES)
            return carry
        lax.fori_loop(0, n_tiles, body, 0)

    @pl.when(jnp.logical_not(tied))
    def _():
        write_bias(lambda blk, j0: jnp.where(blk >= thr, 0.0, NEG_INF))

    @pl.when(tied)
    def _():
        need = kf - count(lambda blk, j0: jnp.where(blk > thr, 1.0, 0.0))

        def jbis(it, ans):
            trial = ans + jnp.left_shift(jnp.int32(1), idx_bits - 1 - it)
            cnt = count(lambda blk, j0: jnp.where(blk == thr, jnp.where(j0 + lane < trial, 1.0, 0.0), 0.0))
            return jnp.where(cnt < need, trial, ans)

        ans = lax.fori_loop(0, idx_bits, jbis, jnp.zeros((tq, LANES), I32))
        write_bias(lambda blk, j0: jnp.where(
            blk > thr, 0.0, jnp.where(blk == thr, jnp.where(j0 + lane <= ans, 0.0, NEG_INF), NEG_INF)))

    m_ref[...] = jnp.full(m_ref.shape, M_INIT, F32)
    acc_ref[...] = jnp.zeros(acc_ref.shape, F32)

    def attend_tile(t, carry):
        k0 = pl.multiple_of(t * tk, tk)
        bias = sc_ref[:, pl.ds(k0, tk)]
        for g in range(N_KV_HEADS):
            qg = q_ref[0, g * group:(g + 1) * group].reshape(group * tq, HEAD_DIM)
            s = lax.dot_general(qg, k_ref[0, g, pl.ds(k0, tk), :], (((1,), (1,)), ((), ())),
                                preferred_element_type=F32)
            s = (s.reshape(group, tq, tk) + bias[None]).reshape(group * tq, tk)
            m_prev = m_ref[g]
            m_new = jnp.maximum(m_prev, jnp.max(s, axis=1, keepdims=True))
            p = jnp.exp(s - m_new[:, :1])
            pv = jnp.dot(p.astype(BF16), v_ref[0, g, pl.ds(k0, tk), :], preferred_element_type=F32)
            acc_ref[g] = jnp.exp(m_prev - m_new) * acc_ref[g] + pv
            m_ref[g] = m_new
        return carry

    lax.fori_loop(0, n_tiles, attend_tile, 0)

    for g in range(N_KV_HEADS):
        acc = acc_ref[g]
        o = acc[:, :HEAD_DIM] / acc[:, HEAD_DIM:HEAD_DIM + 1]
        for r in range(0, group, 2):
            pair = jnp.concatenate([o[r * tq:(r + 1) * tq], o[(r + 1) * tq:(r + 2) * tq]], axis=1)
            h0 = g * group + r
            o_ref[:, h0 * HEAD_DIM:(h0 + 2) * HEAD_DIM] = pair.astype(BF16)


def _dsa(qh, qih, kw_flat, kw_row_off, kh, vh, kih, *, tq, tk, k_sel, q_off, n_valid_end):
    nb, _, s, _ = qh.shape
    lp = kh.shape[2]
    nq = s // tq
    kern = functools.partial(_dsa_kernel, tq=tq, tk=tk, k_sel=k_sel, q_off=q_off, n_valid_end=n_valid_end,
                             idx_bits=max(1, int(lp).bit_length()))
    per_batch = lambda shp: pl.BlockSpec((1,) + shp, lambda b, i: (b,) + (0,) * len(shp), pipeline_mode=pl.Buffered(1))
    return pl.pallas_call(
        kern,
        out_shape=jax.ShapeDtypeStruct((nb * s, N_HEADS * HEAD_DIM), BF16),
        grid=(nb, nq),
        in_specs=[pl.BlockSpec((1, N_HEADS, tq, HEAD_DIM), lambda b, i: (b, 0, i, 0)),
                  pl.BlockSpec((1, IDX_HEADS, tq, IDX_DIM), lambda b, i: (b, 0, i, 0)),
                  pl.BlockSpec((tq, LANES), lambda b, i: (kw_row_off // tq + b * nq + i, 0)),
                  per_batch((N_KV_HEADS, lp, HEAD_DIM)), per_batch((N_KV_HEADS, lp, LANES)), per_batch((lp, IDX_DIM))],
        out_specs=pl.BlockSpec((tq, N_HEADS * HEAD_DIM), lambda b, i: (b * nq + i, 0)),
        scratch_shapes=[pltpu.VMEM((tq, lp), F32),
                        pltpu.VMEM((N_KV_HEADS, (N_HEADS // N_KV_HEADS) * tq, LANES), F32),
                        pltpu.VMEM((N_KV_HEADS, (N_HEADS // N_KV_HEADS) * tq, LANES), F32)],
        compiler_params=_cparams(("parallel", "arbitrary")),
        name="dsa",
    )(qh, qih, kw_flat, kh, vh, kih)


def _pad_keys(meta_rows, mid_rows, lp):
    nb = mid_rows[0].shape[0]
    tail = mid_rows[0].shape[2:]
    meta_b = jnp.broadcast_to(meta_rows[None], (nb,) + meta_rows.shape)
    used = FRONT + sum(m.shape[1] for m in mid_rows)
    parts = [jnp.zeros((nb, FRONT - N_META) + tail, meta_rows.dtype), meta_b] + list(mid_rows)
    if lp > used:
        parts.append(jnp.zeros((nb, lp - used) + tail, meta_rows.dtype))
    return jnp.concatenate(parts, axis=1)


def _attention_inputs(k_meta, v_meta, ki_meta, k_mid, v_mid, ki_mid, lp):
    nb = k_mid[0].shape[0]
    kp = _pad_keys(k_meta, k_mid, lp).astype(BF16).reshape(nb, lp, N_KV_HEADS, HEAD_DIM).transpose(0, 2, 1, 3)
    vp = _pad_keys(v_meta, v_mid, lp).astype(BF16).reshape(nb, lp, N_KV_HEADS, HEAD_DIM).transpose(0, 2, 1, 3)
    ones = jnp.ones(vp.shape[:3] + (1,), BF16)
    vp = jnp.concatenate([vp, ones, jnp.zeros(vp.shape[:3] + (LANES - HEAD_DIM - 1,), BF16)], axis=3)
    kip = _pad_keys(ki_meta, ki_mid, lp).astype(BF16)
    return kp, vp, kip


def _merge_kernel(xp_ref, xs_ref, cp_ref, cs_ref, ap_ref, as_ref, gate_ref, wao_ref, wout_ref, gffn_ref, wr_ref, br_ref,
                  x1_ref, hn_ref, ids_ref, gts_ref, *, n_first):
    first = pl.program_id(0) < n_first
    x = jnp.where(first, xp_ref[...], xs_ref[...])
    conv = jnp.where(first, cp_ref[...], cs_ref[...]).astype(F32)
    attn = jnp.where(first, ap_ref[...], as_ref[...])
    ao = jnp.dot(attn, wao_ref[...], preferred_element_type=F32)
    gates = gate_ref[...].astype(F32)
    mix = gates[:, :D_MODEL] * conv + gates[:, D_MODEL:] * ao
    x1 = x + jnp.dot(mix.astype(BF16), wout_ref[...], preferred_element_type=F32)
    x1_ref[...] = x1
    hn = x1 * lax.rsqrt(jnp.mean(x1 * x1, axis=-1, keepdims=True) + EPS) * gffn_ref[...]
    hn_ref[...] = hn
    logits = jnp.dot(hn, wr_ref[...], preferred_element_type=F32, precision=lax.Precision.HIGHEST) + br_ref[...]
    lane = lax.broadcasted_iota(I32, logits.shape, 1)
    lg = jnp.where(lane < N_EXPERTS, logits, NEG_INF)
    vals, ids = [], []
    for _ in range(TOP_K_EXPERTS):
        m = jnp.max(lg, axis=1, keepdims=True)
        idx = jnp.min(jnp.where(lg == m, lane, LANES), axis=1, keepdims=True)
        vals.append(m)
        ids.append(idx)
        lg = jnp.where(lane == idx, NEG_INF, lg)
    es = [jnp.exp(v - vals[0]) for v in vals]
    den = es[0]
    for e in es[1:]:
        den = den + e
    ids_out = jnp.zeros(logits.shape, I32)
    gts_out = jnp.zeros(logits.shape, F32)
    for j in range(TOP_K_EXPERTS):
        ids_out = jnp.where(lane == j, ids[j], ids_out)
        gts_out = jnp.where(lane == j, es[j] / den, gts_out)
    ids_ref[...] = ids_out
    gts_ref[...] = gts_out


def _merge(xp, xs, conv_p, conv_s, attn_p, attn_s, gates, wao, wout, gffn, wr, br, tr):
    n_first, n_second = xp.shape[0] // tr, xs.shape[0] // tr
    rows = xp.shape[0] + xs.shape[0]
    first = lambda w: pl.BlockSpec((tr, w), lambda i: (jnp.minimum(i, n_first - 1), 0))
    second = lambda w: pl.BlockSpec((tr, w), lambda i: (jnp.maximum(i - n_first, 0), 0))
    row = lambda w: pl.BlockSpec((tr, w), lambda i: (i, 0))
    return pl.pallas_call(
        functools.partial(_merge_kernel, n_first=n_first),
        out_shape=[jax.ShapeDtypeStruct((rows, D_MODEL), F32), jax.ShapeDtypeStruct((rows, D_MODEL), F32),
                   jax.ShapeDtypeStruct((rows, LANES), I32), jax.ShapeDtypeStruct((rows, LANES), F32)],
        grid=(n_first + n_second,),
        in_specs=[first(D_MODEL), second(D_MODEL), first(D_MODEL), second(D_MODEL), first(D_MODEL), second(D_MODEL),
                  row(2 * D_MODEL), _resident(wao), _resident(wout), _resident(gffn), _resident(wr), _resident(br)],
        out_specs=[row(D_MODEL), row(D_MODEL), row(LANES), row(LANES)],
        compiler_params=_cparams(("parallel",)),
        name="merge",
    )(xp, xs, conv_p, conv_s, attn_p, attn_s, gates, wao, wout, gffn, wr, br)


def _moe_kernel(te_ref, nreal_ref, src_hbm, dst_hbm, h_hbm, w1_ref, b1_ref, w3_ref, b3_ref, w2_ref, b2_ref,
                y_hbm, src_sm, dst_sm, xbuf, ybuf, isem, gsem, ssem, *, tm):
    del te_ref
    i = pl.program_id(0)
    nreal = nreal_ref[0]
    slot = lax.rem(i, 2)

    def fetch_indices(tile, s):
        cs = pltpu.make_async_copy(src_hbm.at[tile], src_sm.at[s], isem.at[0])
        cd = pltpu.make_async_copy(dst_hbm.at[tile], dst_sm.at[s], isem.at[1])
        cs.start()
        cd.start()
        cs.wait()
        cd.wait()

    def gather_row(s, r):
        return pltpu.make_async_copy(h_hbm.at[pl.ds(src_sm[s, r], 1)], xbuf.at[s, pl.ds(r, 1)], gsem.at[s])

    def scatter_row(s, r):
        return pltpu.make_async_copy(ybuf.at[pl.ds(r, 1)], y_hbm.at[pl.ds(dst_sm[s, r], 1)], ssem.at[0])

    def gather_wait(s, r):
        return pltpu.make_async_copy(h_hbm.at[pl.ds(0, 1)], xbuf.at[s, pl.ds(r, 1)], gsem.at[s])

    def scatter_wait(r):
        return pltpu.make_async_copy(ybuf.at[pl.ds(r, 1)], y_hbm.at[pl.ds(0, 1)], ssem.at[0])

    def start_gather(s):
        def body(r, carry):
            gather_row(s, r).start()
            return carry
        lax.fori_loop(0, tm, body, 0, unroll=8)

    def wait_rows(copy_of_row):
        def body(r, carry):
            copy_of_row(r).wait()
            return carry
        lax.fori_loop(0, tm, body, 0, unroll=8)

    @pl.when(jnp.logical_and(i == 0, nreal > 0))
    def _():
        fetch_indices(0, 0)
        start_gather(0)

    @pl.when(i + 1 < nreal)
    def _():
        fetch_indices(i + 1, 1 - slot)
        start_gather(1 - slot)

    @pl.when(i < nreal)
    def _():
        wait_rows(lambda r: gather_wait(slot, r))
        x = xbuf[slot].astype(BF16)
        g = jnp.minimum(jnp.dot(x, w1_ref[0], preferred_element_type=F32) + b1_ref[0], SWIGLU_LIMIT)
        u = jnp.clip(jnp.dot(x, w3_ref[0], preferred_element_type=F32) + b3_ref[0], -SWIGLU_LIMIT, SWIGLU_LIMIT)
        act = g * jax.nn.sigmoid(SWIGLU_ALPHA * g) * (u + 1.0)
        y = jnp.dot(act.astype(BF16), w2_ref[0], preferred_element_type=F32) + b2_ref[0]

        @pl.when(i > 0)
        def _():
            wait_rows(scatter_wait)

        ybuf[...] = y

        def body(r, carry):
            scatter_row(slot, r).start()
            return carry
        lax.fori_loop(0, tm, body, 0, unroll=8)

        @pl.when(i == nreal - 1)
        def _():
            wait_rows(scatter_wait)


def _moe(hn, src, dst, tile_expert, n_real, w1, b1, w3, b3, w2, b2, n_out_rows, tm):
    nt = src.shape[0]
    wspec = pl.BlockSpec((1, D_MODEL, D_FF), lambda i, te, nr: (te[i], 0, 0))
    w2spec = pl.BlockSpec((1, D_FF, D_MODEL), lambda i, te, nr: (te[i], 0, 0))
    bspec = pl.BlockSpec((1, 1, D_FF), lambda i, te, nr: (te[i], 0, 0))
    any_spec = pl.BlockSpec(memory_space=pl.ANY)
    grid_spec = pltpu.PrefetchScalarGridSpec(
        num_scalar_prefetch=2,
        grid=(nt,),
        in_specs=[any_spec, any_spec, any_spec, wspec, bspec, wspec, bspec, w2spec, bspec],
        out_specs=any_spec,
        scratch_shapes=[pltpu.SMEM((2, tm), I32), pltpu.SMEM((2, tm), I32),
                        pltpu.VMEM((2, tm, D_MODEL), F32), pltpu.VMEM((tm, D_MODEL), F32),
                        pltpu.SemaphoreType.DMA((2,)), pltpu.SemaphoreType.DMA((2,)), pltpu.SemaphoreType.DMA((1,))],
    )
    return pl.pallas_call(
        functools.partial(_moe_kernel, tm=tm),
        out_shape=jax.ShapeDtypeStruct((n_out_rows, D_MODEL), F32),
        grid_spec=grid_spec,
        compiler_params=_cparams(("arbitrary",)),
        name="moe",
    )(tile_expert, n_real, src, dst, hn, w1, b1, w3, b3, w2, b2)


def _route(ids, n, tm):
    e = ids.reshape(-1)
    onehot = (e[:, None] == jnp.arange(N_EXPERTS, dtype=I32)[None, :]).astype(I32)
    csum = jnp.cumsum(onehot, axis=0)
    rank = jnp.sum((csum - onehot) * onehot, axis=1)
    counts = csum[-1]
    padded = (counts + tm - 1) // tm * tm
    ends = jnp.cumsum(padded)
    pos = (ends - padded)[e] + rank
    n_pairs = n * TOP_K_EXPERTS
    nt = -(-(n_pairs + N_EXPERTS * (tm - 1)) // tm)
    n_rows = nt * tm
    pair = jnp.arange(n_pairs, dtype=I32)
    src = jnp.zeros((n_rows,), I32).at[pos].set(pair // TOP_K_EXPERTS)
    real = jnp.zeros((n_rows,), I32).at[pos].set(1)
    dump = n_pairs + jnp.cumsum(1 - real) - 1
    dst = jnp.where(real == 1, jnp.zeros((n_rows,), I32).at[pos].set((pair % TOP_K_EXPERTS) * n + pair // TOP_K_EXPERTS),
                    dump)
    tile_expert = jnp.minimum(jnp.searchsorted(ends // tm, jnp.arange(nt, dtype=I32), side="right"),
                              N_EXPERTS - 1).astype(I32)
    n_real = (ends[-1] // tm).astype(I32).reshape(1)
    return src.reshape(nt, tm), dst.reshape(nt, tm), tile_expert, n_real, n_rows


def _final_kernel(x1_ref, y0_ref, y1_ref, y2_ref, y3_ref, gts_ref, g_ref, out_ref):
    gts = gts_ref[...]
    acc = x1_ref[...]
    for j, y_ref in enumerate((y0_ref, y1_ref, y2_ref, y3_ref)):
        acc = acc + gts[:, j:j + 1] * y_ref[...]
    out_ref[...] = acc * lax.rsqrt(jnp.mean(acc * acc, axis=-1, keepdims=True) + EPS) * g_ref[...]


def _final(x1, yslots, gts, g, tr):
    n = x1.shape[0]
    nblk = n // tr
    row = lambda w: pl.BlockSpec((tr, w), lambda i: (i, 0))
    slot = lambda j: pl.BlockSpec((tr, D_MODEL), lambda i: (j * nblk + i, 0))
    return pl.pallas_call(
        _final_kernel,
        out_shape=jax.ShapeDtypeStruct((n, D_MODEL), F32),
        grid=(nblk,),
        in_specs=[row(D_MODEL)] + [slot(j) for j in range(TOP_K_EXPERTS)] + [row(LANES), _resident(g)],
        out_specs=row(D_MODEL),
        compiler_params=_cparams(("parallel",)),
        name="final",
    )(x1, yslots, yslots, yslots, yslots, gts, g)


def _tile(n, pref):
    t = min(n, pref)
    while n % t:
        t //= 2
    return t


def kernel(x_prompt, x_sample, cache_k, cache_v, cache_idx_k, state_conv, meta_tokens, g_norm_mix, w_in, conv_w,
           conv_b, conv_ln_g, conv_ln_b, w_conv_out, w_attn_out, w_out, g_norm_ffn, w_router, b_router,
           w1, b1, w3, b3, w2, b2, g_norm_final):
    nb, s_p, _ = x_prompt.shape
    db, s_n, _ = x_sample.shape
    past = cache_k.shape[1]
    rows_p, rows_s = nb * s_p, db * s_n
    n = rows_p + rows_s
    tr = _tile(rows_s, ROW_TILE)
    assert rows_p % tr == 0 and s_p % tr == 0 and tr % s_n == 0 and s_p % Q_TILE == 0

    cuts = [0, 2 * CONV_CH, N_HEADS * HEAD_DIM, N_KV_HEADS * HEAD_DIM, N_KV_HEADS * HEAD_DIM, IDX_HEADS * IDX_DIM,
            IDX_DIM + IDX_HEADS, 2 * D_MODEL]
    offs = [sum(cuts[:i + 1]) for i in range(len(cuts))]
    seg = [w_in[:, offs[i]:offs[i + 1]].astype(BF16) for i in range(len(cuts) - 1)]
    seg[5] = jnp.pad(seg[5], ((0, 0), (0, LANES - seg[5].shape[1])))
    g_mix = g_norm_mix.reshape(1, D_MODEL)

    pos_p = N_META + jnp.arange(s_p, dtype=I32)
    pos_s = jnp.tile(N_META + past + jnp.arange(s_n, dtype=I32), tr // s_n)
    tabs = _rope_tables(jnp.concatenate([pos_p, pos_s]))
    blocks_per_seq = s_p // tr
    n_first = rows_p // tr
    tab_map = lambda i: jnp.where(i < n_first, lax.rem(i, blocks_per_seq), blocks_per_seq)

    xp, xs = x_prompt.reshape(rows_p, D_MODEL), x_sample.reshape(rows_s, D_MODEL)
    a, q, k, v, qi, kw, gates = _project(xp, xs, tabs, tab_map, g_mix, seg, tr)

    meta = meta_tokens.astype(F32)
    tabs_m = _rope_tables(jnp.arange(N_META, dtype=I32))
    a_m, _, k_m, v_m, _, kw_m, _ = _project(meta, meta, tabs_m, lambda i: 0, g_mix, seg, N_META)
    a_m, k_m, v_m, ki_m = a_m[:N_META], k_m[:N_META], v_m[:N_META], kw_m[:N_META, :IDX_DIM]

    cw = jnp.pad(conv_w, ((0, HALO - CONV_WIDTH), (0, 0)))
    cb, lg, lb = conv_b.reshape(1, -1), conv_ln_g.reshape(1, -1), conv_ln_b.reshape(1, -1)
    wco = w_conv_out.astype(BF16)
    ctx_p = jnp.concatenate([jnp.zeros((HALO - N_META, CONV_CH), F32), a_m], axis=0)[None]
    ctx_s = jnp.pad(state_conv, ((0, 0), (HALO - (CONV_WIDTH - 1), 0), (0, 0)))
    conv_p = _conv_branch(a, 0, nb, s_p, ctx_p, cw, cb, lg, lb, wco, _tile(s_p, 256))
    conv_s = _conv_branch(a, rows_p, db, s_n, ctx_s, cw, cb, lg, lb, wco, s_n)

    k_p3, v_p3 = k[:rows_p].reshape(nb, s_p, -1), v[:rows_p].reshape(nb, s_p, -1)
    ki_p3 = kw[:rows_p, :IDX_DIM].reshape(nb, s_p, IDX_DIM)
    k_s3, v_s3 = k[rows_p:].reshape(db, s_n, -1), v[rows_p:].reshape(db, s_n, -1)
    ki_s3 = kw[rows_p:, :IDX_DIM].reshape(db, s_n, IDX_DIM)

    def heads(t, rows0, b_, s_, nh):
        return t[rows0:rows0 + b_ * s_].reshape(b_, s_, nh, -1).transpose(0, 2, 1, 3)

    tk_p = _tile(s_p, K_TILE)
    lp_p = -(-(FRONT + s_p) // tk_p) * tk_p
    kp, vp, kip = _attention_inputs(k_m, v_m, ki_m, [k_p3], [v_p3], [ki_p3], lp_p)
    attn_p = _dsa(heads(q, 0, nb, s_p, N_HEADS), heads(qi, 0, nb, s_p, IDX_HEADS), kw, 0, kp, vp, kip,
                  tq=Q_TILE, tk=tk_p, k_sel=min(TOPK_MAX, s_p // 4), q_off=0, n_valid_end=FRONT + s_p)

    tk_s = K_TILE
    lp_s = -(-(FRONT + past + s_n) // tk_s) * tk_s
    ck = cache_k.reshape(db, past, -1)
    cv = cache_v.reshape(db, past, -1)
    ks, vs, kis = _attention_inputs(k_m, v_m, ki_m, [ck, k_s3], [cv, v_s3], [cache_idx_k, ki_s3], lp_s)
    attn_s = _dsa(heads(q, rows_p, db, s_n, N_HEADS), heads(qi, rows_p, db, s_n, IDX_HEADS), kw, rows_p, ks, vs, kis,
                  tq=s_n, tk=tk_s, k_sel=min(TOPK_MAX, (past + s_n) // 4), q_off=past,
                  n_valid_end=FRONT + past + s_n)

    wr = jnp.pad(w_router, ((0, 0), (0, LANES - N_EXPERTS)))
    br = jnp.pad(b_router, (0, LANES - N_EXPERTS)).reshape(1, LANES)
    x1, hn, ids, gts = _merge(xp, xs, conv_p, conv_s, attn_p, attn_s, gates, w_attn_out.astype(BF16),
                              w_out.astype(BF16), g_norm_ffn.reshape(1, -1), wr, br, tr)

    tm = MOE_TILE
    src, dst, tile_expert, n_real, n_out_rows = _route(ids[:, :TOP_K_EXPERTS], n, tm)
    yslots = _moe(hn, src, dst, tile_expert, n_real, w1.astype(BF16), b1.reshape(N_EXPERTS, 1, D_FF),
                  w3.astype(BF16), b3.reshape(N_EXPERTS, 1, D_FF), w2.astype(BF16),
                  b2.reshape(N_EXPERTS, 1, D_MODEL), n_out_rows, tm)
    out = _final(x1, yslots, gts, g_norm_final.reshape(1, -1), tr)

    y_prompt = out[:rows_p].reshape(nb, s_p, D_MODEL)
    y_sample = out[rows_p:].reshape(db, s_n, D_MODEL)
    lead = lambda m: jnp.broadcast_to(m[None], (nb,) + m.shape)
    k_prompt = jnp.concatenate([lead(k_m), k_p3], axis=1).reshape(nb, N_META + s_p, N_KV_HEADS, HEAD_DIM)
    v_prompt = jnp.concatenate([lead(v_m), v_p3], axis=1).reshape(nb, N_META + s_p, N_KV_HEADS, HEAD_DIM)
    idxk_prompt = jnp.concatenate([lead(ki_m), ki_p3], axis=1)
    a_p3 = a[:rows_p].reshape(nb, s_p, CONV_CH)
    a_s3 = a[rows_p:].reshape(db, s_n, CONV_CH)
    keep = CONV_WIDTH - 1
    conv_state_p = jnp.concatenate([lead(a_m), a_p3], axis=1)[:, N_META + s_p - keep:]
    conv_state_s = jnp.concatenate([state_conv, a_s3], axis=1)[:, s_n:]
    return (y_prompt, y_sample, k_prompt, v_prompt, idxk_prompt, conv_state_p,
            k_s3.reshape(db, s_n, N_KV_HEADS, HEAD_DIM), v_s3.reshape(db, s_n, N_KV_HEADS, HEAD_DIM), ki_s3,
            conv_state_s)
```

```python
import functools

import jax
import jax.numpy as jnp
from jax import lax
from jax.experimental import pallas as pl
from jax.experimental.pallas import tpu as pltpu

F32, BF16, I32 = jnp.float32, jnp.bfloat16, jnp.int32

D_MODEL = 1024
CHUNK = 64
N_META = 16
CONV_CH = D_MODEL
CONV_WIDTH = 31
N_HEADS = 16
N_KV_HEADS = 4
HEAD_DIM = 64
ROT_DIM = HEAD_DIM // 4
ROPE_THETA = 500000.0
IDX_HEADS = 8
IDX_DIM = 64
TOPK_MAX = 256
N_EXPERTS = 32
TOP_K_EXPERTS = 4
D_FF = D_MODEL
SWIGLU_LIMIT = 7.0
SWIGLU_ALPHA = 1.702
EPS = 1e-5

LANES = 128
SUB = 8
FRONT = 128
HALO = 32
VMEM_LIMIT = 56 * 1024 * 1024
NEG_INF = float("-inf")
INT_MIN = -2 ** 31
M_INIT = -1e30

ROW_TILE = 512
Q_TILE = 128
K_TILE = 512
MOE_TILE = 512


def _cparams(sem):
    return pltpu.CompilerParams(dimension_semantics=sem, vmem_limit_bytes=VMEM_LIMIT)


def _resident(arr):
    nd = arr.ndim
    return pl.BlockSpec(arr.shape, lambda *_: (0,) * nd, pipeline_mode=pl.Buffered(1))


def _rope_cols(x, c, s1, s2):
    outs = []
    for j in range(x.shape[1] // LANES):
        xj = x[:, j * LANES:(j + 1) * LANES]
        outs.append(xj * c + pltpu.roll(xj, LANES - ROT_DIM // 2, 1) * s1 + pltpu.roll(xj, ROT_DIM // 2, 1) * s2)
    return outs


def _proj_kernel(xp_ref, xs_ref, g_ref, wa_ref, wq_ref, wk_ref, wv_ref, wqi_ref, wkw_ref, wgl_ref,
                 c_ref, s1_ref, s2_ref,
                 a_ref, q_ref, k_ref, v_ref, qi_ref, kw_ref, gate_ref, *, n_first):
    x = jnp.where(pl.program_id(0) < n_first, xp_ref[...], xs_ref[...])
    h = x * lax.rsqrt(jnp.mean(x * x, axis=-1, keepdims=True) + EPS) * g_ref[...]
    hb = h.astype(BF16)
    c, s1, s2 = c_ref[...], s1_ref[...], s2_ref[...]

    def dot(w_ref):
        return jnp.dot(hb, w_ref[...], preferred_element_type=F32)

    glu = dot(wa_ref)
    a_ref[...] = glu[:, :CONV_CH] * jax.nn.sigmoid(glu[:, CONV_CH:])
    for j, col in enumerate(_rope_cols(dot(wq_ref), c, s1, s2)):
        q_ref[:, j * LANES:(j + 1) * LANES] = (col * HEAD_DIM ** -0.5).astype(BF16)
    for j, col in enumerate(_rope_cols(dot(wk_ref), c, s1, s2)):
        k_ref[:, j * LANES:(j + 1) * LANES] = col
    v_ref[...] = dot(wv_ref)
    for j, col in enumerate(_rope_cols(dot(wqi_ref), c, s1, s2)):
        qi_ref[:, j * LANES:(j + 1) * LANES] = (col * IDX_DIM ** -0.5).astype(BF16)
    pkw = dot(wkw_ref)
    lane = lax.broadcasted_iota(I32, pkw.shape, 1)
    kw_ref[...] = jnp.where(lane < IDX_DIM, _rope_cols(pkw, c, s1, s2)[0], pkw * IDX_HEADS ** -0.5)
    gate_ref[...] = jax.nn.sigmoid(dot(wgl_ref)).astype(BF16)


def _rope_tables(pos):
    half = ROT_DIM // 2
    inv = ROPE_THETA ** (-jnp.arange(half, dtype=F32) / half)
    ang = pos.astype(F32)[:, None] * inv[None, :]
    cos, sin = jnp.cos(ang), jnp.sin(ang)
    r = pos.shape[0]
    z = lambda n: jnp.zeros((r, n), F32)
    c = jnp.concatenate([cos, cos, jnp.ones((r, HEAD_DIM - ROT_DIM), F32)], axis=1)
    s1 = jnp.concatenate([-sin, z(HEAD_DIM - half)], axis=1)
    s2 = jnp.concatenate([z(half), sin, z(HEAD_DIM - ROT_DIM)], axis=1)
    two = lambda t: jnp.concatenate([t, t], axis=1)
    return two(c), two(s1), two(s2)


def _project(xp, xs, tabs, tab_map, g, ws, tr):
    n_first, n_second = xp.shape[0] // tr, xs.shape[0] // tr
    rows = xp.shape[0] + xs.shape[0]
    row = lambda w: pl.BlockSpec((tr, w), lambda i: (i, 0))
    tab = pl.BlockSpec((tr, LANES), lambda i: (tab_map(i), 0))
    in_specs = [pl.BlockSpec((tr, D_MODEL), lambda i: (jnp.minimum(i, n_first - 1), 0)),
                pl.BlockSpec((tr, D_MODEL), lambda i: (jnp.maximum(i - n_first, 0), 0)),
                _resident(g)] + [_resident(w) for w in ws] + [tab, tab, tab]
    widths = [(CONV_CH, F32), (N_HEADS * HEAD_DIM, BF16), (N_KV_HEADS * HEAD_DIM, F32), (N_KV_HEADS * HEAD_DIM, F32),
              (IDX_HEADS * IDX_DIM, BF16), (LANES, F32), (2 * D_MODEL, BF16)]
    return pl.pallas_call(
        functools.partial(_proj_kernel, n_first=n_first),
        out_shape=[jax.ShapeDtypeStruct((rows, w), dt) for w, dt in widths],
        grid=(n_first + n_second,),
        in_specs=in_specs,
        out_specs=[row(w) for w, _ in widths],
        compiler_params=_cparams(("arbitrary",)),
        name="proj",
    )(xp, xs, g, *ws, *tabs)


def _conv_kernel(cur_ref, halo_ref, lc_ref, cw_ref, cb_ref, lg_ref, lb_ref, wo_ref, out_ref, e_ref, c_ref, *, tr, rc):
    halo = jnp.where(pl.program_id(1) == 0, lc_ref[0], halo_ref[...])
    e_ref[0, 0:HALO, :] = halo
    e_ref[0, HALO:HALO + tr, :] = cur_ref[...]
    n = tr + HALO - 8
    for b in range(1, 8):
        e_ref[b, 0:n, :] = e_ref[0, b:b + n, :]
    first = HALO - (CONV_WIDTH - 1)

    def chunk(ci, carry):
        r0 = pl.multiple_of(ci * rc, rc)
        acc = jnp.zeros((rc, CONV_CH), F32) + cb_ref[...]
        for w in range(CONV_WIDTH):
            s = first + w
            acc = acc + e_ref[s % 8, pl.ds(r0 + 8 * (s // 8), rc), :] * cw_ref[pl.ds(w, 1), :]
        c_ref[pl.ds(r0, rc), :] = acc
        return carry

    lax.fori_loop(0, tr // rc, chunk, 0)
    c = c_ref[...]
    mu = jnp.mean(c, axis=-1, keepdims=True)
    xc = c - mu
    y = xc * lax.rsqrt(jnp.mean(xc * xc, axis=-1, keepdims=True) + EPS) * lg_ref[...] + lb_ref[...]
    act = (y * jax.nn.sigmoid(y)).astype(BF16)
    out_ref[...] = jnp.dot(act, wo_ref[...], preferred_element_type=F32).astype(BF16)


def _conv_branch(a_flat, row_off, nb, t, left_ctx, cw, cb, lg, lb, wo, tr):
    nt = t // tr
    rc = min(tr, 32)
    per_batch_ctx = left_ctx.shape[0] > 1
    cur = pl.BlockSpec((tr, CONV_CH), lambda b, i: (row_off // tr + b * nt + i, 0))
    halo = pl.BlockSpec((HALO, CONV_CH),
                        lambda b, i: (jnp.maximum((row_off + b * t + i * tr) // HALO - 1, 0), 0))
    lctx = pl.BlockSpec((1, HALO, CONV_CH), lambda b, i: (b if per_batch_ctx else 0, 0, 0))
    return pl.pallas_call(
        functools.partial(_conv_kernel, tr=tr, rc=rc),
        out_shape=jax.ShapeDtypeStruct((nb * t, D_MODEL), BF16),
        grid=(nb, nt),
        in_specs=[cur, halo, lctx, _resident(cw), _resident(cb), _resident(lg), _resident(lb), _resident(wo)],
        out_specs=pl.BlockSpec((tr, D_MODEL), lambda b, i: (b * nt + i, 0)),
        scratch_shapes=[pltpu.VMEM((8, tr + HALO, CONV_CH), F32), pltpu.VMEM((tr, CONV_CH), F32)],
        compiler_params=_cparams(("arbitrary", "arbitrary")),
        name="conv",
    )(a_flat, a_flat, left_ctx, cw, cb, lg, lb, wo)


def _key_to_float(u):
    key = u ^ INT_MIN
    bits = key ^ ((key >> 31) & 0x7FFFFFFF)
    return lax.bitcast_convert_type(bits, F32)


def _dsa_kernel(q_ref, qi_ref, kw_ref, k_ref, v_ref, ki_ref, o_ref, sc_ref, m_ref, acc_ref,
                *, tq, tk, k_sel, q_off, n_valid_end, idx_bits):
    i = pl.program_id(1)
    qf0 = q_off + i * tq
    klim = jnp.minimum(FRONT + ((qf0 + tq - 1) // CHUNK + 1) * CHUNK, n_valid_end)
    n_tiles = (klim + tk - 1) // tk
    ncol = tk // LANES
    kf = float(k_sel)
    group = N_HEADS // N_KV_HEADS

    kwv = kw_ref[...]
    qrow = lax.broadcasted_iota(I32, (tq, 1), 0)
    lim_row = jnp.minimum(FRONT + ((qf0 + qrow) // CHUNK + 1) * CHUNK, n_valid_end)

    def score_tile(t, carry):
        k0 = pl.multiple_of(t * tk, tk)
        kt = ki_ref[0, pl.ds(k0, tk), :]
        sc = jnp.zeros((tq, tk), F32)
        for h in range(IDX_HEADS):
            rel = lax.dot_general(qi_ref[0, h], kt, (((1,), (1,)), ((), ())), preferred_element_type=F32)
            sc = sc + kwv[:, IDX_DIM + h:IDX_DIM + h + 1] * jnp.maximum(rel, 0.0)
        j = k0 + lax.broadcasted_iota(I32, (tq, tk), 1)
        j = jnp.where(j >= FRONT - N_META, j, n_valid_end)
        sc_ref[:, pl.ds(k0, tk)] = jnp.where(j < lim_row, sc, NEG_INF)
        return carry

    lax.fori_loop(0, n_tiles, score_tile, 0)

    lane = lax.broadcasted_iota(I32, (tq, LANES), 1)

    def count(pred):
        def body(t, acc):
            k0 = pl.multiple_of(t * tk, tk)
            for c in range(ncol):
                blk = sc_ref[:, pl.ds(k0 + c * LANES, LANES)]
                acc = acc + pred(blk, k0 + c * LANES)
            return acc
        acc = lax.fori_loop(0, n_tiles, body, jnp.zeros((tq, LANES), F32))
        return jnp.broadcast_to(jnp.sum(acc, axis=1, keepdims=True), (tq, LANES))

    def bisect(it, u):
        trial = u | jnp.left_shift(jnp.int32(1), 31 - it)
        thr = _key_to_float(trial)
        cnt = count(lambda blk, j0: jnp.where(blk >= thr, 1.0, 0.0))
        return jnp.where(cnt >= kf, trial, u)

    u = lax.fori_loop(0, 32, bisect, jnp.zeros((tq, LANES), I32))
    key = jnp.maximum(u ^ INT_MIN, INT_MIN + 0x00800000)
    thr = _key_to_float(key ^ INT_MIN)
    n_ge = count(lambda blk, j0: jnp.where(blk >= thr, 1.0, 0.0))
    tied = jnp.max(n_ge) > kf

    def write_bias(sel):
        def body(t, carry):
            k0 = pl.multiple_of(t * tk, tk)
            for c in range(ncol):
                blk = sc_ref[:, pl.ds(k0 + c * LANES, LANES)]
                sc_ref[:, pl.ds(k0 + c * LANES, LANES)] = sel(blk, k0 + c * LANES)
            return carry
        lax.fori_loop(0, n_tiles, body, 0)

    @pl.when(jnp.logical_not(tied))
    def _():
        write_bias(lambda blk, j0: jnp.where(blk >= thr, 0.0, NEG_INF))

    @pl.when(tied)
    def _():
        need = kf - count(lambda blk, j0: jnp.where(blk > thr, 1.0, 0.0))

        def jbis(it, ans):
            trial = ans + jnp.left_shift(jnp.int32(1), idx_bits - 1 - it)
            cnt = count(lambda blk, j0: jnp.where(blk == thr, jnp.where(j0 + lane < trial, 1.0, 0.0), 0.0))
            return jnp.where(cnt < need, trial, ans)

        ans = lax.fori_loop(0, idx_bits, jbis, jnp.zeros((tq, LANES), I32))
        write_bias(lambda blk, j0: jnp.where(
            blk > thr, 0.0, jnp.where(blk == thr, jnp.where(j0 + lane <= ans, 0.0, NEG_INF), NEG_INF)))

    m_ref[...] = jnp.full(m_ref.shape, M_INIT, F32)
    acc_ref[...] = jnp.zeros(acc_ref.shape, F32)

    def attend_tile(t, carry):
        k0 = pl.multiple_of(t * tk, tk)
        bias = sc_ref[:, pl.ds(k0, tk)]
        for g in range(N_KV_HEADS):
            qg = q_ref[0, g * group:(g + 1) * group].reshape(group * tq, HEAD_DIM)
            s = lax.dot_general(qg, k_ref[0, g, pl.ds(k0, tk), :], (((1,), (1,)), ((), ())),
                                preferred_element_type=F32)
            s = (s.reshape(group, tq, tk) + bias[None]).reshape(group * tq, tk)
            m_prev = m_ref[g]
            m_new = jnp.maximum(m_prev, jnp.max(s, axis=1, keepdims=True))
            p = jnp.exp(s - jnp.concatenate([m_new] * ncol, axis=1))
            pv = jnp.dot(p.astype(BF16), v_ref[0, g, pl.ds(k0, tk), :], preferred_element_type=F32)
            acc_ref[g] = jnp.exp(m_prev - m_new) * acc_ref[g] + pv
            m_ref[g] = m_new
        return carry

    lax.fori_loop(0, n_tiles, attend_tile, 0)

    for g in range(N_KV_HEADS):
        acc = acc_ref[g]
        o = acc[:, :HEAD_DIM] / acc[:, HEAD_DIM:HEAD_DIM + 1]
        for r in range(0, group, 2):
            pair = jnp.concatenate([o[r * tq:(r + 1) * tq], o[(r + 1) * tq:(r + 2) * tq]], axis=1)
            h0 = g * group + r
            o_ref[:, h0 * HEAD_DIM:(h0 + 2) * HEAD_DIM] = pair.astype(BF16)


def _dsa(qh, qih, kw_flat, kw_row_off, kh, vh, kih, *, tq, tk, k_sel, q_off, n_valid_end):
    nb, _, s, _ = qh.shape
    lp = kh.shape[2]
    nq = s // tq
    kern = functools.partial(_dsa_kernel, tq=tq, tk=tk, k_sel=k_sel, q_off=q_off, n_valid_end=n_valid_end,
                             idx_bits=max(1, int(lp).bit_length()))
    per_batch = lambda shp: pl.BlockSpec((1,) + shp, lambda b, i: (b,) + (0,) * len(shp), pipeline_mode=pl.Buffered(1))
    return pl.pallas_call(
        kern,
        out_shape=jax.ShapeDtypeStruct((nb * s, N_HEADS * HEAD_DIM), BF16),
        grid=(nb, nq),
        in_specs=[pl.BlockSpec((1, N_HEADS, tq, HEAD_DIM), lambda b, i: (b, 0, i, 0)),
                  pl.BlockSpec((1, IDX_HEADS, tq, IDX_DIM), lambda b, i: (b, 0, i, 0)),
                  pl.BlockSpec((tq, LANES), lambda b, i: (kw_row_off // tq + b * nq + i, 0)),
                  per_batch((N_KV_HEADS, lp, HEAD_DIM)), per_batch((N_KV_HEADS, lp, LANES)), per_batch((lp, IDX_DIM))],
        out_specs=pl.BlockSpec((tq, N_HEADS * HEAD_DIM), lambda b, i: (b * nq + i, 0)),
        scratch_shapes=[pltpu.VMEM((tq, (lp // LANES | 1) * LANES), F32),
                        pltpu.VMEM((N_KV_HEADS, (N_HEADS // N_KV_HEADS) * tq, LANES), F32),
                        pltpu.VMEM((N_KV_HEADS, (N_HEADS // N_KV_HEADS) * tq, LANES), F32)],
        compiler_params=_cparams(("arbitrary", "arbitrary")),
        name="dsa",
    )(qh, qih, kw_flat, kh, vh, kih)


def _dsa_t_kernel(q_ref, qi_ref, wt_ref, k_ref, vt_ref, ki_ref, o_ref, sc_ref, m_ref, acc_ref, s_ref,
                  *, tk, k_sel, n_valid_end, idx_bits):
    tq = LANES
    i = pl.program_id(1)
    qf0 = i * tq
    klim = jnp.minimum(FRONT + qf0 + tq, n_valid_end)
    n_tiles = (klim + tk - 1) // tk
    kf = float(k_sel)
    group = N_HEADS // N_KV_HEADS
    fold = 32

    wt = wt_ref[0]
    qlane = lax.broadcasted_iota(I32, (1, tq), 1)
    lim = jnp.minimum(FRONT + ((qf0 + qlane) // CHUNK + 1) * CHUNK, n_valid_end)
    qi_all = qi_ref[0].reshape(IDX_HEADS * tq, IDX_DIM)

    def score_tile(t, carry):
        th = tk // 2
        for half in range(2):
            k0 = pl.multiple_of(t * tk, tk) + half * th
            rel = lax.dot_general(ki_ref[0, pl.ds(k0, th), :], qi_all, (((1,), (1,)), ((), ())),
                                  preferred_element_type=F32)
            sc = jnp.zeros((th, tq), F32)
            for h in range(IDX_HEADS):
                sc = sc + wt[h:h + 1, :] * jnp.maximum(rel[:, h * tq:(h + 1) * tq], 0.0)
            j = k0 + lax.broadcasted_iota(I32, (th, tq), 0)
            j = jnp.where(j >= FRONT - N_META, j, n_valid_end)
            sc_ref[pl.ds(k0, th), :] = jnp.where(j < lim, sc, NEG_INF)
        return carry

    lax.fori_loop(0, n_tiles, score_tile, 0)

    def count(pred):
        def body(t, acc):
            k0 = pl.multiple_of(t * tk, tk)
            ones = pred(sc_ref[pl.ds(k0, tk), :], k0)
            return acc + jnp.sum(ones.reshape(tk // fold, fold, tq), axis=0)
        acc = lax.fori_loop(0, n_tiles, body, jnp.zeros((fold, tq), F32))
        return jnp.sum(acc, axis=0, keepdims=True)

    def bisect(it, u):
        trial = u | jnp.left_shift(jnp.int32(1), 31 - it)
        thr = _key_to_float(trial)
        cnt = count(lambda blk, k0: jnp.where(blk >= thr, 1.0, 0.0))
        return jnp.where(cnt >= kf, trial, u)

    u = lax.fori_loop(0, 32, bisect, jnp.zeros((1, tq), I32))
    key = jnp.maximum(u ^ INT_MIN, INT_MIN + 0x00800000)
    thr = _key_to_float(key ^ INT_MIN)
    n_ge = count(lambda blk, k0: jnp.where(blk >= thr, 1.0, 0.0))
    tied = jnp.max(n_ge) > kf

    def write_bias(sel):
        def body(t, carry):
            k0 = pl.multiple_of(t * tk, tk)
            sc_ref[pl.ds(k0, tk), :] = sel(sc_ref[pl.ds(k0, tk), :], k0)
            return carry
        lax.fori_loop(0, n_tiles, body, 0)

    @pl.when(jnp.logical_not(tied))
    def _():
        write_bias(lambda blk, k0: jnp.where(blk >= thr, 0.0, NEG_INF))

    @pl.when(tied)
    def _():
        need = kf - count(lambda blk, k0: jnp.where(blk > thr, 1.0, 0.0))
        krow = lax.broadcasted_iota(I32, (tk, tq), 0)

        def jbis(it, ans):
            trial = ans + jnp.left_shift(jnp.int32(1), idx_bits - 1 - it)
            cnt = count(lambda blk, k0: jnp.where(blk == thr, jnp.where(k0 + krow < trial, 1.0, 0.0), 0.0))
            return jnp.where(cnt < need, trial, ans)

        ans = lax.fori_loop(0, idx_bits, jbis, jnp.zeros((1, tq), I32))
        write_bias(lambda blk, k0: jnp.where(
            blk > thr, 0.0, jnp.where(blk == thr, jnp.where(k0 + krow <= ans, 0.0, NEG_INF), NEG_INF)))

    m_ref[...] = jnp.full(m_ref.shape, M_INIT, F32)
    acc_ref[...] = jnp.zeros(acc_ref.shape, F32)

    def attend_tile(t, carry):
        k0 = pl.multiple_of(t * tk, tk)
        bias = sc_ref[pl.ds(k0, tk), :]
        alphas = []
        for g in range(N_KV_HEADS):
            qg = q_ref[0, g * group:(g + 1) * group].reshape(group * tq, HEAD_DIM)
            s = lax.dot_general(k_ref[0, g, pl.ds(k0, tk), :], qg, (((1,), (1,)), ((), ())),
                                preferred_element_type=F32)
            s = jnp.concatenate([s[:, r * tq:(r + 1) * tq] + bias for r in range(group)], axis=1)
            s_ref[g] = s
            m_prev = m_ref[g]
            m_new = jnp.maximum(m_prev, jnp.max(s, axis=0, keepdims=True))
            m_ref[g] = m_new
            alphas.append(jnp.exp(m_prev - m_new))
        for g in range(N_KV_HEADS):
            p = jnp.exp(s_ref[g] - m_ref[g])
            pv = jnp.dot(vt_ref[0, g, :, pl.ds(k0, tk)], p.astype(BF16), preferred_element_type=F32)
            acc_ref[g] = alphas[g] * acc_ref[g] + pv
        return carry

    lax.fori_loop(0, n_tiles, attend_tile, 0)

    for g in range(N_KV_HEADS):
        acc = acc_ref[g]
        o = acc[:HEAD_DIM, :] / acc[HEAD_DIM:HEAD_DIM + 1, :]
        for r in range(0, group, 2):
            pair = jnp.concatenate([o[:, r * tq:(r + 1) * tq], o[:, (r + 1) * tq:(r + 2) * tq]], axis=0)
            h0 = g * group + r
            o_ref[:, h0 * HEAD_DIM:(h0 + 2) * HEAD_DIM] = pair.T.astype(BF16)


def _dsa_t(qh, qih, wt, kh, vt, kih, *, tk, k_sel, n_valid_end):
    nb, _, s, _ = qh.shape
    lp = kh.shape[2]
    tq = LANES
    nq = s // tq
    group = N_HEADS // N_KV_HEADS
    kern = functools.partial(_dsa_t_kernel, tk=tk, k_sel=k_sel, n_valid_end=n_valid_end,
                             idx_bits=max(1, int(lp).bit_length()))
    per_batch = lambda shp: pl.BlockSpec((1,) + shp, lambda b, i: (b,) + (0,) * len(shp), pipeline_mode=pl.Buffered(1))
    return pl.pallas_call(
        kern,
        out_shape=jax.ShapeDtypeStruct((nb * s, N_HEADS * HEAD_DIM), BF16),
        grid=(nb, nq),
        in_specs=[pl.BlockSpec((1, N_HEADS, tq, HEAD_DIM), lambda b, i: (b, 0, i, 0)),
                  pl.BlockSpec((1, IDX_HEADS, tq, IDX_DIM), lambda b, i: (b, 0, i, 0)),
                  pl.BlockSpec((1, IDX_HEADS, tq), lambda b, i: (b, 0, i)),
                  per_batch((N_KV_HEADS, lp, HEAD_DIM)), per_batch((N_KV_HEADS, LANES, lp)), per_batch((lp, IDX_DIM))],
        out_specs=pl.BlockSpec((tq, N_HEADS * HEAD_DIM), lambda b, i: (b * nq + i, 0)),
        scratch_shapes=[pltpu.VMEM((lp, tq), F32),
                        pltpu.VMEM((N_KV_HEADS, 1, group * tq), F32),
                        pltpu.VMEM((N_KV_HEADS, LANES, group * tq), F32),
                        pltpu.VMEM((N_KV_HEADS, tk, group * tq), F32)],
        compiler_params=_cparams(("arbitrary", "arbitrary")),
        name="dsa_t",
    )(qh, qih, wt, kh, vt, kih)


def _pad_keys(meta_rows, mid_rows, lp):
    nb = mid_rows[0].shape[0]
    tail = mid_rows[0].shape[2:]
    meta_b = jnp.broadcast_to(meta_rows[None], (nb,) + meta_rows.shape)
    used = FRONT + sum(m.shape[1] for m in mid_rows)
    parts = [jnp.zeros((nb, FRONT - N_META) + tail, meta_rows.dtype), meta_b] + list(mid_rows)
    if lp > used:
        parts.append(jnp.zeros((nb, lp - used) + tail, meta_rows.dtype))
    return jnp.concatenate(parts, axis=1)


def _attention_inputs(k_meta, v_meta, ki_meta, k_mid, v_mid, ki_mid, lp, keys_on_lanes):
    nb = k_mid[0].shape[0]
    kp = _pad_keys(k_meta, k_mid, lp).astype(BF16).reshape(nb, lp, N_KV_HEADS, HEAD_DIM).transpose(0, 2, 1, 3)
    vp = _pad_keys(v_meta, v_mid, lp).astype(BF16).reshape(nb, lp, N_KV_HEADS, HEAD_DIM)
    if keys_on_lanes:
        vp = vp.transpose(0, 2, 3, 1)
        ones = jnp.ones((nb, N_KV_HEADS, 1, lp), BF16)
        vp = jnp.concatenate([vp, ones, jnp.zeros((nb, N_KV_HEADS, LANES - HEAD_DIM - 1, lp), BF16)], axis=2)
    else:
        vp = vp.transpose(0, 2, 1, 3)
        ones = jnp.ones(vp.shape[:3] + (1,), BF16)
        vp = jnp.concatenate([vp, ones, jnp.zeros(vp.shape[:3] + (LANES - HEAD_DIM - 1,), BF16)], axis=3)
    kip = _pad_keys(ki_meta, ki_mid, lp).astype(BF16)
    return kp, vp, kip


def _merge_kernel(xp_ref, xs_ref, cp_ref, cs_ref, ap_ref, as_ref, gate_ref, wao_ref, wout_ref, gffn_ref, wr_ref, br_ref,
                  x1_ref, hn_ref, ids_ref, gts_ref, *, n_first):
    first = pl.program_id(0) < n_first
    x = jnp.where(first, xp_ref[...], xs_ref[...])
    conv = jnp.where(first, cp_ref[...], cs_ref[...]).astype(F32)
    attn = jnp.where(first, ap_ref[...], as_ref[...])
    ao = jnp.dot(attn, wao_ref[...], preferred_element_type=F32)
    gates = gate_ref[...].astype(F32)
    mix = gates[:, :D_MODEL] * conv + gates[:, D_MODEL:] * ao
    x1 = x + jnp.dot(mix.astype(BF16), wout_ref[...], preferred_element_type=F32)
    x1_ref[...] = x1
    hn = x1 * lax.rsqrt(jnp.mean(x1 * x1, axis=-1, keepdims=True) + EPS) * gffn_ref[...]
    tr = hn.shape[0]
    for c in range(SUB):
        hn_ref[pl.ds(c, tr, stride=SUB), :] = hn[:, c * LANES:(c + 1) * LANES]
    logits = jnp.dot(hn, wr_ref[...], preferred_element_type=F32, precision=lax.Precision.HIGHEST) + br_ref[...]
    lane = lax.broadcasted_iota(I32, logits.shape, 1)
    lg = jnp.where(lane < N_EXPERTS, logits, NEG_INF)
    vals, ids = [], []
    for _ in range(TOP_K_EXPERTS):
        m = jnp.max(lg, axis=1, keepdims=True)
        idx = jnp.min(jnp.where(lg == m, lane, LANES), axis=1, keepdims=True)
        vals.append(m)
        ids.append(idx)
        lg = jnp.where(lane == idx, NEG_INF, lg)
    es = [jnp.exp(v - vals[0]) for v in vals]
    den = es[0]
    for e in es[1:]:
        den = den + e
    ids_out = jnp.zeros(logits.shape, I32)
    gts_out = jnp.zeros(logits.shape, F32)
    for j in range(TOP_K_EXPERTS):
        ids_out = jnp.where(lane == j, ids[j], ids_out)
        gts_out = jnp.where(lane == j, es[j] / den, gts_out)
    ids_ref[...] = ids_out
    gts_ref[...] = gts_out


def _merge(xp, xs, conv_p, conv_s, attn_p, attn_s, gates, wao, wout, gffn, wr, br, tr):
    n_first, n_second = xp.shape[0] // tr, xs.shape[0] // tr
    rows = xp.shape[0] + xs.shape[0]
    first = lambda w: pl.BlockSpec((tr, w), lambda i: (jnp.minimum(i, n_first - 1), 0))
    second = lambda w: pl.BlockSpec((tr, w), lambda i: (jnp.maximum(i - n_first, 0), 0))
    row = lambda w: pl.BlockSpec((tr, w), lambda i: (i, 0))
    return pl.pallas_call(
        functools.partial(_merge_kernel, n_first=n_first),
        out_shape=[jax.ShapeDtypeStruct((rows, D_MODEL), F32), jax.ShapeDtypeStruct((rows * SUB, LANES), F32),
                   jax.ShapeDtypeStruct((rows, LANES), I32), jax.ShapeDtypeStruct((rows, LANES), F32)],
        grid=(n_first + n_second,),
        in_specs=[first(D_MODEL), second(D_MODEL), first(D_MODEL), second(D_MODEL), first(D_MODEL), second(D_MODEL),
                  row(2 * D_MODEL), _resident(wao), _resident(wout), _resident(gffn), _resident(wr), _resident(br)],
        out_specs=[row(D_MODEL), pl.BlockSpec((tr * SUB, LANES), lambda i: (i, 0)), row(LANES), row(LANES)],
        compiler_params=_cparams(("arbitrary",)),
        name="merge",
    )(xp, xs, conv_p, conv_s, attn_p, attn_s, gates, wao, wout, gffn, wr, br)


KEY_ALIGN = 1024


def _key_window(tm):
    return -(-(tm + KEY_ALIGN - 1) // KEY_ALIGN) * KEY_ALIGN


def _for_rows(n, fn):
    def group(gi, carry):
        for u in range(SUB):
            fn(gi * SUB + u)
        return carry

    def single(r, carry):
        fn(r)
        return carry

    shift = SUB.bit_length() - 1
    lax.fori_loop(0, n >> shift, group, 0)
    lax.fori_loop((n >> shift) << shift, n, single, 0)


def _moe_kernel(te_ref, base_ref, nval_ref, nreal_ref, keys_hbm, h_hbm, w1_ref, b1_ref, w3_ref, b3_ref, w2_ref, b2_ref,
                y_hbm, key_sm, xbuf, xb, ybuf, ksem, gsem, ssem, *, tm, n_tok, pair_bits):
    del te_ref
    i = pl.program_id(0)
    nreal = nreal_ref[0]
    window = key_sm.shape[0] // 3
    slot_shift = TOP_K_EXPERTS.bit_length() - 1

    def keys_copy(tile):
        s = lax.rem(tile, 3)
        start = pl.multiple_of(base_ref[tile] & -KEY_ALIGN, KEY_ALIGN)
        return pltpu.make_async_copy(keys_hbm.at[pl.ds(start, window)],
                                     key_sm.at[pl.ds(pl.multiple_of(s * window, KEY_ALIGN), window)], ksem.at[s])

    def pair_reader(tile):
        first = lax.rem(tile, 3) * window + (base_ref[tile] & (KEY_ALIGN - 1))
        return lambda r: key_sm[first + r] & ((1 << pair_bits) - 1)

    def gather_row(s, r, token):
        return pltpu.make_async_copy(h_hbm.at[pl.ds(token * SUB, SUB)], xbuf.at[s, pl.ds(r * SUB, SUB)], gsem.at[s])

    def scatter_row(r, out_token):
        return pltpu.make_async_copy(ybuf.at[pl.ds(r * SUB, SUB)], y_hbm.at[pl.ds(out_token * SUB, SUB)], ssem.at[0])

    def start_gather(tile):
        pair_of, s = pair_reader(tile), tile & 1
        _for_rows(tm, lambda r: gather_row(s, r, pair_of(r) >> slot_shift).start())

    @pl.when(jnp.logical_and(i == 0, nreal > 0))
    def _():
        keys_copy(0).start()

        @pl.when(nreal > 1)
        def _():
            keys_copy(1).start()

        keys_copy(0).wait()
        start_gather(0)

    @pl.when(i + 1 < nreal)
    def _():
        keys_copy(i + 1).wait()
        start_gather(i + 1)

    @pl.when(i + 2 < nreal)
    def _():
        keys_copy(i + 2).start()

    @pl.when(i < nreal)
    def _():
        s = i & 1
        _for_rows(tm, lambda r: gather_row(s, r, 0).wait())
        for c in range(SUB):
            xb[:, c * LANES:(c + 1) * LANES] = xbuf[s, pl.ds(c, tm, stride=SUB), :].astype(BF16)
        x = xb[...]
        g = jnp.minimum(jnp.dot(x, w1_ref[0], preferred_element_type=F32) + b1_ref[0], SWIGLU_LIMIT)
        u = jnp.clip(jnp.dot(x, w3_ref[0], preferred_element_type=F32) + b3_ref[0], -SWIGLU_LIMIT, SWIGLU_LIMIT)
        act = g * jax.nn.sigmoid(SWIGLU_ALPHA * g) * (u + 1.0)
        y = jnp.dot(act.astype(BF16), w2_ref[0], preferred_element_type=F32) + b2_ref[0]

        @pl.when(i > 0)
        def _():
            _for_rows(nval_ref[i - 1], lambda r: scatter_row(r, 0).wait())

        for c in range(SUB):
            ybuf[pl.ds(c, tm, stride=SUB), :] = y[:, c * LANES:(c + 1) * LANES]

        pair_of = pair_reader(i)

        def put(r):
            pair = pair_of(r)
            scatter_row(r, (pair & (TOP_K_EXPERTS - 1)) * n_tok + (pair >> slot_shift)).start()

        _for_rows(nval_ref[i], put)

        @pl.when(i == nreal - 1)
        def _():
            _for_rows(nval_ref[i], lambda r: scatter_row(r, 0).wait())


def _moe(hn, keys, tile_expert, base, nval, n_real, w1, b1, w3, b3, w2, b2, n_tok, pair_bits, tm):
    nt = tile_expert.shape[0]
    by_expert = lambda i, te, *_: (te[i], 0, 0)
    wspec = pl.BlockSpec((1, D_MODEL, D_FF), by_expert)
    w2spec = pl.BlockSpec((1, D_FF, D_MODEL), by_expert)
    bspec = pl.BlockSpec((1, 1, D_FF), by_expert)
    any_spec = pl.BlockSpec(memory_space=pl.ANY)
    grid_spec = pltpu.PrefetchScalarGridSpec(
        num_scalar_prefetch=4,
        grid=(nt,),
        in_specs=[any_spec, any_spec, wspec, bspec, wspec, bspec, w2spec, bspec],
        out_specs=any_spec,
        scratch_shapes=[pltpu.SMEM((3 * _key_window(tm),), I32),
                        pltpu.VMEM((2, tm * SUB, LANES), F32), pltpu.VMEM((tm, D_MODEL), BF16),
                        pltpu.VMEM((tm * SUB, LANES), F32),
                        pltpu.SemaphoreType.DMA((3,)), pltpu.SemaphoreType.DMA((2,)), pltpu.SemaphoreType.DMA((1,))],
    )
    return pl.pallas_call(
        functools.partial(_moe_kernel, tm=tm, n_tok=n_tok, pair_bits=pair_bits),
        out_shape=jax.ShapeDtypeStruct((TOP_K_EXPERTS * n_tok * SUB, LANES), F32),
        grid_spec=grid_spec,
        compiler_params=_cparams(("arbitrary",)),
        name="moe",
    )(tile_expert, base, nval, n_real, keys, hn, w1, b1, w3, b3, w2, b2)


def _route(ids, n, tm):
    n_pairs = n * TOP_K_EXPERTS
    pair_bits = max(1, (n_pairs - 1).bit_length())
    assert pair_bits + (N_EXPERTS - 1).bit_length() <= 31 and tm % LANES == 0
    assert TOP_K_EXPERTS & (TOP_K_EXPERTS - 1) == 0
    pair = jnp.arange(n_pairs, dtype=I32)
    keys = jnp.sort(jnp.left_shift(ids.reshape(-1), pair_bits) | pair)
    starts = jnp.searchsorted(keys, jnp.left_shift(jnp.arange(N_EXPERTS + 1, dtype=I32), pair_bits)).astype(I32)
    counts = starts[1:] - starts[:-1]
    tiles = (counts + tm - 1) // tm
    tile_end = jnp.cumsum(tiles)
    nt = -(-(n_pairs + N_EXPERTS * (tm - 1)) // tm)
    ti = jnp.arange(nt, dtype=I32)
    te = jnp.minimum(jnp.searchsorted(tile_end, ti, side="right"), N_EXPERTS - 1).astype(I32)
    within = ti - (tile_end - tiles)[te]
    base = jnp.clip(starts[:-1][te] + within * tm, 0, n_pairs - 1).astype(I32)
    nval = jnp.clip(counts[te] - within * tm, 0, tm).astype(I32)
    padded = -(-n_pairs // KEY_ALIGN) * KEY_ALIGN + _key_window(tm)
    keys = jnp.pad(keys, (0, padded - n_pairs))
    return keys, te, base, nval, tile_end[-1:].astype(I32), pair_bits


def _final_kernel(x1_ref, y0_ref, y1_ref, y2_ref, y3_ref, gts_ref, g_ref, out_ref):
    gts = gts_ref[...]
    tr = gts.shape[0]
    cols = []
    for c in range(SUB):
        acc = x1_ref[:, c * LANES:(c + 1) * LANES]
        for j, y_ref in enumerate((y0_ref, y1_ref, y2_ref, y3_ref)):
            acc = acc + gts[:, j:j + 1] * y_ref[pl.ds(c, tr, stride=SUB), :]
        cols.append(acc)
    ss = cols[0] * cols[0]
    for acc in cols[1:]:
        ss = ss + acc * acc
    scale = lax.rsqrt(jnp.sum(ss, axis=-1, keepdims=True) / D_MODEL + EPS)
    for c, acc in enumerate(cols):
        out_ref[:, c * LANES:(c + 1) * LANES] = acc * scale * g_ref[:, c * LANES:(c + 1) * LANES]


def _final(x1, yslots, gts, g, tr):
    n = x1.shape[0]
    nblk = n // tr
    row = lambda w: pl.BlockSpec((tr, w), lambda i: (i, 0))
    slot = lambda j: pl.BlockSpec((tr * SUB, LANES), lambda i: (j * nblk + i, 0))
    return pl.pallas_call(
        _final_kernel,
        out_shape=jax.ShapeDtypeStruct((n, D_MODEL), F32),
        grid=(nblk,),
        in_specs=[row(D_MODEL)] + [slot(j) for j in range(TOP_K_EXPERTS)] + [row(LANES), _resident(g)],
        out_specs=row(D_MODEL),
        compiler_params=_cparams(("arbitrary",)),
        name="final",
    )(x1, yslots, yslots, yslots, yslots, gts, g)


def _tile(n, pref):
    t = min(n, pref)
    while n % t:
        t //= 2
    return t


def kernel(x_prompt, x_sample, cache_k, cache_v, cache_idx_k, state_conv, meta_tokens, g_norm_mix, w_in, conv_w,
           conv_b, conv_ln_g, conv_ln_b, w_conv_out, w_attn_out, w_out, g_norm_ffn, w_router, b_router,
           w1, b1, w3, b3, w2, b2, g_norm_final):
    nb, s_p, _ = x_prompt.shape
    db, s_n, _ = x_sample.shape
    past = cache_k.shape[1]
    rows_p, rows_s = nb * s_p, db * s_n
    n = rows_p + rows_s
    tr = _tile(rows_s, ROW_TILE)
    assert rows_p % tr == 0 and s_p % tr == 0 and tr % s_n == 0 and s_p % Q_TILE == 0

    cuts = [0, 2 * CONV_CH, N_HEADS * HEAD_DIM, N_KV_HEADS * HEAD_DIM, N_KV_HEADS * HEAD_DIM, IDX_HEADS * IDX_DIM,
            IDX_DIM + IDX_HEADS, 2 * D_MODEL]
    offs = [sum(cuts[:i + 1]) for i in range(len(cuts))]
    seg = [w_in[:, offs[i]:offs[i + 1]].astype(BF16) for i in range(len(cuts) - 1)]
    seg[5] = jnp.pad(seg[5], ((0, 0), (0, LANES - seg[5].shape[1])))
    g_mix = g_norm_mix.reshape(1, D_MODEL)

    pos_p = N_META + jnp.arange(s_p, dtype=I32)
    pos_s = jnp.tile(N_META + past + jnp.arange(s_n, dtype=I32), tr // s_n)
    tabs = _rope_tables(jnp.concatenate([pos_p, pos_s]))
    blocks_per_seq = s_p // tr
    n_first = rows_p // tr
    tab_map = lambda i: jnp.where(i < n_first, lax.rem(i, blocks_per_seq), blocks_per_seq)

    xp, xs = x_prompt.reshape(rows_p, D_MODEL), x_sample.reshape(rows_s, D_MODEL)
    a, q, k, v, qi, kw, gates = _project(xp, xs, tabs, tab_map, g_mix, seg, tr)

    meta = meta_tokens.astype(F32)
    tabs_m = _rope_tables(jnp.arange(N_META, dtype=I32))
    a_m, _, k_m, v_m, _, kw_m, _ = _project(meta, meta, tabs_m, lambda i: 0, g_mix, seg, N_META)
    a_m, k_m, v_m, ki_m = a_m[:N_META], k_m[:N_META], v_m[:N_META], kw_m[:N_META, :IDX_DIM]

    cw = jnp.pad(conv_w, ((0, HALO - CONV_WIDTH), (0, 0)))
    cb, lg, lb = conv_b.reshape(1, -1), conv_ln_g.reshape(1, -1), conv_ln_b.reshape(1, -1)
    wco = w_conv_out.astype(BF16)
    ctx_p = jnp.concatenate([jnp.zeros((HALO - N_META, CONV_CH), F32), a_m], axis=0)[None]
    ctx_s = jnp.pad(state_conv, ((0, 0), (HALO - (CONV_WIDTH - 1), 0), (0, 0)))
    conv_p = _conv_branch(a, 0, nb, s_p, ctx_p, cw, cb, lg, lb, wco, _tile(s_p, 256))
    conv_s = _conv_branch(a, rows_p, db, s_n, ctx_s, cw, cb, lg, lb, wco, s_n)

    k_p3, v_p3 = k[:rows_p].reshape(nb, s_p, -1), v[:rows_p].reshape(nb, s_p, -1)
    ki_p3 = kw[:rows_p, :IDX_DIM].reshape(nb, s_p, IDX_DIM)
    k_s3, v_s3 = k[rows_p:].reshape(db, s_n, -1), v[rows_p:].reshape(db, s_n, -1)
    ki_s3 = kw[rows_p:, :IDX_DIM].reshape(db, s_n, IDX_DIM)

    def heads(t, rows0, b_, s_, nh):
        return t[rows0:rows0 + b_ * s_].reshape(b_, s_, nh, -1).transpose(0, 2, 1, 3)

    tk_p = _tile(s_p, K_TILE)
    lp_p = -(-(FRONT + s_p) // tk_p) * tk_p
    kp, vtp, kip = _attention_inputs(k_m, v_m, ki_m, [k_p3], [v_p3], [ki_p3], lp_p, True)
    wt_p = kw[:rows_p, IDX_DIM:IDX_DIM + IDX_HEADS].reshape(nb, s_p, IDX_HEADS).transpose(0, 2, 1)
    attn_p = _dsa_t(heads(q, 0, nb, s_p, N_HEADS), heads(qi, 0, nb, s_p, IDX_HEADS), wt_p, kp, vtp, kip,
                    tk=tk_p, k_sel=min(TOPK_MAX, s_p // 4), n_valid_end=FRONT + s_p)

    tk_s = K_TILE
    lp_s = -(-(FRONT + past + s_n) // tk_s) * tk_s
    ck = cache_k.reshape(db, past, -1)
    cv = cache_v.reshape(db, past, -1)
    ks, vs, kis = _attention_inputs(k_m, v_m, ki_m, [ck, k_s3], [cv, v_s3], [cache_idx_k, ki_s3], lp_s, False)
    attn_s = _dsa(heads(q, rows_p, db, s_n, N_HEADS), heads(qi, rows_p, db, s_n, IDX_HEADS), kw, rows_p, ks, vs, kis,
                  tq=s_n, tk=tk_s, k_sel=min(TOPK_MAX, (past + s_n) // 4), q_off=past,
                  n_valid_end=FRONT + past + s_n)

    wr = jnp.pad(w_router, ((0, 0), (0, LANES - N_EXPERTS)))
    br = jnp.pad(b_router, (0, LANES - N_EXPERTS)).reshape(1, LANES)
    x1, hn, ids, gts = _merge(xp, xs, conv_p, conv_s, attn_p, attn_s, gates, w_attn_out.astype(BF16),
                              w_out.astype(BF16), g_norm_ffn.reshape(1, -1), wr, br, tr)

    tm = MOE_TILE
    keys, tile_expert, base, nval, n_real, pair_bits = _route(ids[:, :TOP_K_EXPERTS], n, tm)
    yslots = _moe(hn, keys, tile_expert, base, nval, n_real, w1.astype(BF16), b1.reshape(N_EXPERTS, 1, D_FF),
                  w3.astype(BF16), b3.reshape(N_EXPERTS, 1, D_FF), w2.astype(BF16),
                  b2.reshape(N_EXPERTS, 1, D_MODEL), n, pair_bits, tm)
    out = _final(x1, yslots, gts, g_norm_final.reshape(1, -1), tr)

    y_prompt = out[:rows_p].reshape(nb, s_p, D_MODEL)
    y_sample = out[rows_p:].reshape(db, s_n, D_MODEL)
    lead = lambda m: jnp.broadcast_to(m[None], (nb,) + m.shape)
    k_prompt = jnp.concatenate([lead(k_m), k_p3], axis=1).reshape(nb, N_META + s_p, N_KV_HEADS, HEAD_DIM)
    v_prompt = jnp.concatenate([lead(v_m), v_p3], axis=1).reshape(nb, N_META + s_p, N_KV_HEADS, HEAD_DIM)
    idxk_prompt = jnp.concatenate([lead(ki_m), ki_p3], axis=1)
    a_p3 = a[:rows_p].reshape(nb, s_p, CONV_CH)
    a_s3 = a[rows_p:].reshape(db, s_n, CONV_CH)
    keep = CONV_WIDTH - 1
    conv_state_p = jnp.concatenate([lead(a_m), a_p3], axis=1)[:, N_META + s_p - keep:]
    conv_state_s = jnp.concatenate([state_conv, a_s3], axis=1)[:, s_n:]
    return (y_prompt, y_sample, k_prompt, v_prompt, idxk_prompt, conv_state_p,
            k_s3.reshape(db, s_n, N_KV_HEADS, HEAD_DIM), v_s3.reshape(db, s_n, N_KV_HEADS, HEAD_DIM), ki_s3,
            conv_state_s)
```

```python
import functools

import jax
import jax.numpy as jnp
from jax import lax
from jax.experimental import pallas as pl
from jax.experimental.pallas import tpu as pltpu

F32, BF16, I32 = jnp.float32, jnp.bfloat16, jnp.int32
HALF = jnp.bfloat16

D_MODEL = 1024
CHUNK = 64
N_META = 16
CONV_CH = D_MODEL
CONV_WIDTH = 31
N_HEADS = 16
N_KV_HEADS = 4
HEAD_DIM = 64
ROT_DIM = HEAD_DIM // 4
ROPE_THETA = 500000.0
IDX_HEADS = 8
IDX_DIM = 64
TOPK_MAX = 256
N_EXPERTS = 32
TOP_K_EXPERTS = 4
D_FF = D_MODEL
SWIGLU_LIMIT = 7.0
SWIGLU_ALPHA = 1.702
EPS = 1e-5
Q_SCALE = HEAD_DIM ** -0.5 * 1.4426950408889634

LANES = 128
SUB = 8
VT_ROWS = HEAD_DIM + 2 * SUB
FRONT = 128
HALO = 32
VMEM_LIMIT = 56 * 1024 * 1024
NEG_INF = float("-inf")
INT_MIN = -2 ** 31
M_INIT = -1e30

ROW_TILE = 512
Q_TILE = 128
K_TILE = 512
MOE_TILE = 512


def _cparams(sem):
    return pltpu.CompilerParams(dimension_semantics=sem, vmem_limit_bytes=VMEM_LIMIT)


def _resident(arr):
    nd = arr.ndim
    return pl.BlockSpec(arr.shape, lambda *_: (0,) * nd, pipeline_mode=pl.Buffered(1))


def _rope_cols(x, c, s1, s2):
    outs = []
    for j in range(x.shape[1] // LANES):
        xj = x[:, j * LANES:(j + 1) * LANES]
        outs.append(xj * c + pltpu.roll(xj, LANES - ROT_DIM // 2, 1) * s1 + pltpu.roll(xj, ROT_DIM // 2, 1) * s2)
    return outs


def _proj_kernel(xp_ref, xs_ref, g_ref, wa_ref, wq_ref, wk_ref, wv_ref, wqi_ref, wkw_ref, wgl_ref,
                 c_ref, s1_ref, s2_ref,
                 a_ref, q_ref, k_ref, v_ref, qi_ref, kw_ref, gate_ref, *, n_first):
    x = jnp.where(pl.program_id(0) < n_first, xp_ref[...], xs_ref[...])
    h = x * lax.rsqrt(jnp.mean(x * x, axis=-1, keepdims=True) + EPS) * g_ref[...]
    hb = h.astype(BF16)
    c, s1, s2 = c_ref[...], s1_ref[...], s2_ref[...]

    def dot(w_ref):
        return jnp.dot(hb, w_ref[...], preferred_element_type=F32)

    glu = dot(wa_ref)
    a_ref[...] = glu[:, :CONV_CH] * jax.nn.sigmoid(glu[:, CONV_CH:])
    for j, col in enumerate(_rope_cols(dot(wq_ref), c, s1, s2)):
        q_ref[:, j * LANES:(j + 1) * LANES] = (col * Q_SCALE).astype(BF16)
    for j, col in enumerate(_rope_cols(dot(wk_ref), c, s1, s2)):
        k_ref[:, j * LANES:(j + 1) * LANES] = col
    v_ref[...] = dot(wv_ref)
    for j, col in enumerate(_rope_cols(dot(wqi_ref), c, s1, s2)):
        qi_ref[:, j * LANES:(j + 1) * LANES] = (col * IDX_DIM ** -0.5).astype(BF16)
    pkw = dot(wkw_ref)
    lane = lax.broadcasted_iota(I32, pkw.shape, 1)
    kw_ref[...] = jnp.where(lane < IDX_DIM, _rope_cols(pkw, c, s1, s2)[0], pkw * IDX_HEADS ** -0.5)
    gate_ref[...] = jax.nn.sigmoid(dot(wgl_ref)).astype(BF16)


def _rope_tables(pos):
    half = ROT_DIM // 2
    inv = ROPE_THETA ** (-jnp.arange(half, dtype=F32) / half)
    ang = pos.astype(F32)[:, None] * inv[None, :]
    cos, sin = jnp.cos(ang), jnp.sin(ang)
    r = pos.shape[0]
    z = lambda n: jnp.zeros((r, n), F32)
    c = jnp.concatenate([cos, cos, jnp.ones((r, HEAD_DIM - ROT_DIM), F32)], axis=1)
    s1 = jnp.concatenate([-sin, z(HEAD_DIM - half)], axis=1)
    s2 = jnp.concatenate([z(half), sin, z(HEAD_DIM - ROT_DIM)], axis=1)
    two = lambda t: jnp.concatenate([t, t], axis=1)
    return two(c), two(s1), two(s2)


def _project(xp, xs, tabs, tab_map, g, ws, tr):
    n_first, n_second = xp.shape[0] // tr, xs.shape[0] // tr
    rows = xp.shape[0] + xs.shape[0]
    row = lambda w: pl.BlockSpec((tr, w), lambda i: (i, 0))
    tab = pl.BlockSpec((tr, LANES), lambda i: (tab_map(i), 0))
    in_specs = [pl.BlockSpec((tr, D_MODEL), lambda i: (jnp.minimum(i, n_first - 1), 0)),
                pl.BlockSpec((tr, D_MODEL), lambda i: (jnp.maximum(i - n_first, 0), 0)),
                _resident(g)] + [_resident(w) for w in ws] + [tab, tab, tab]
    widths = [(CONV_CH, F32), (N_HEADS * HEAD_DIM, BF16), (N_KV_HEADS * HEAD_DIM, F32), (N_KV_HEADS * HEAD_DIM, F32),
              (IDX_HEADS * IDX_DIM, BF16), (LANES, F32), (2 * D_MODEL, BF16)]
    return pl.pallas_call(
        functools.partial(_proj_kernel, n_first=n_first),
        out_shape=[jax.ShapeDtypeStruct((rows, w), dt) for w, dt in widths],
        grid=(n_first + n_second,),
        in_specs=in_specs,
        out_specs=[row(w) for w, _ in widths],
        compiler_params=_cparams(("arbitrary",)),
        name="proj",
    )(xp, xs, g, *ws, *tabs)


def _conv_kernel(cur_ref, halo_ref, lc_ref, cw_ref, cb_ref, lg_ref, lb_ref, wo_ref, out_ref, e_ref, c_ref, *, tr, rc):
    halo = jnp.where(pl.program_id(1) == 0, lc_ref[0], halo_ref[...])
    e_ref[0, 0:HALO, :] = halo
    e_ref[0, HALO:HALO + tr, :] = cur_ref[...]
    n = tr + HALO - 8
    for b in range(1, 8):
        e_ref[b, 0:n, :] = e_ref[0, b:b + n, :]
    first = HALO - (CONV_WIDTH - 1)

    def chunk(ci, carry):
        r0 = pl.multiple_of(ci * rc, rc)
        acc = jnp.zeros((rc, CONV_CH), F32) + cb_ref[...]
        for w in range(CONV_WIDTH):
            s = first + w
            acc = acc + e_ref[s % 8, pl.ds(r0 + 8 * (s // 8), rc), :] * cw_ref[pl.ds(w, 1), :]
        c_ref[pl.ds(r0, rc), :] = acc
        return carry

    lax.fori_loop(0, tr // rc, chunk, 0)
    c = c_ref[...]
    mu = jnp.mean(c, axis=-1, keepdims=True)
    xc = c - mu
    y = xc * lax.rsqrt(jnp.mean(xc * xc, axis=-1, keepdims=True) + EPS) * lg_ref[...] + lb_ref[...]
    act = (y * jax.nn.sigmoid(y)).astype(BF16)
    out_ref[...] = jnp.dot(act, wo_ref[...], preferred_element_type=F32).astype(BF16)


def _conv_branch(a_flat, row_off, nb, t, left_ctx, cw, cb, lg, lb, wo, tr):
    nt = t // tr
    rc = min(tr, 32)
    per_batch_ctx = left_ctx.shape[0] > 1
    cur = pl.BlockSpec((tr, CONV_CH), lambda b, i: (row_off // tr + b * nt + i, 0))
    halo = pl.BlockSpec((HALO, CONV_CH),
                        lambda b, i: (jnp.maximum((row_off + b * t + i * tr) // HALO - 1, 0), 0))
    lctx = pl.BlockSpec((1, HALO, CONV_CH), lambda b, i: (b if per_batch_ctx else 0, 0, 0))
    return pl.pallas_call(
        functools.partial(_conv_kernel, tr=tr, rc=rc),
        out_shape=jax.ShapeDtypeStruct((nb * t, D_MODEL), BF16),
        grid=(nb, nt),
        in_specs=[cur, halo, lctx, _resident(cw), _resident(cb), _resident(lg), _resident(lb), _resident(wo)],
        out_specs=pl.BlockSpec((tr, D_MODEL), lambda b, i: (b * nt + i, 0)),
        scratch_shapes=[pltpu.VMEM((8, tr + HALO, CONV_CH), F32), pltpu.VMEM((tr, CONV_CH), F32)],
        compiler_params=_cparams(("arbitrary", "arbitrary")),
        name="conv",
    )(a_flat, a_flat, left_ctx, cw, cb, lg, lb, wo)


def _float_to_skey(f):
    bits = lax.bitcast_convert_type(f, I32)
    return bits ^ ((bits >> 31) & 0x7FFFFFFF)


def _skey_to_float(key):
    return lax.bitcast_convert_type(key ^ ((key >> 31) & 0x7FFFFFFF), F32)


SKEY_LOWEST_FINITE = INT_MIN + 0x00800000


def _high_half(f):
    return lax.bitcast_convert_type(lax.bitcast_convert_type(f, I32) & -65536, F32)


def _key_to_float(u):
    key = u ^ INT_MIN
    bits = key ^ ((key >> 31) & 0x7FFFFFFF)
    return lax.bitcast_convert_type(bits, F32)


def _dsa_kernel(q_ref, qi_ref, kw_ref, k_ref, v_ref, ki_ref, o_ref, sc_ref, m_ref, acc_ref,
                *, tq, tk, k_sel, q_off, n_valid_end, idx_bits):
    i = pl.program_id(1)
    qf0 = q_off + i * tq
    klim = jnp.minimum(FRONT + ((qf0 + tq - 1) // CHUNK + 1) * CHUNK, n_valid_end)
    n_tiles = (klim + tk - 1) // tk
    ncol = tk // LANES
    kf = float(k_sel)
    group = N_HEADS // N_KV_HEADS

    kwv = kw_ref[...]
    qrow = lax.broadcasted_iota(I32, (tq, 1), 0)
    lim_row = jnp.minimum(FRONT + ((qf0 + qrow) // CHUNK + 1) * CHUNK, n_valid_end)

    def score_tile(t, carry):
        k0 = pl.multiple_of(t * tk, tk)
        kt = ki_ref[0, pl.ds(k0, tk), :]
        sc = jnp.zeros((tq, tk), F32)
        for h in range(IDX_HEADS):
            rel = lax.dot_general(qi_ref[0, h], kt, (((1,), (1,)), ((), ())), preferred_element_type=F32)
            sc = sc + kwv[:, IDX_DIM + h:IDX_DIM + h + 1] * jnp.maximum(rel, 0.0)
        j = k0 + lax.broadcasted_iota(I32, (tq, tk), 1)
        j = jnp.where(j >= FRONT - N_META, j, n_valid_end)
        sc_ref[:, pl.ds(k0, tk)] = jnp.where(j < lim_row, sc, NEG_INF)
        return carry

    lax.fori_loop(0, n_tiles, score_tile, 0)

    lane = lax.broadcasted_iota(I32, (tq, LANES), 1)

    def count(pred):
        def body(t, acc):
            k0 = pl.multiple_of(t * tk, tk)
            for c in range(ncol):
                blk = sc_ref[:, pl.ds(k0 + c * LANES, LANES)]
                acc = acc + pred(blk, k0 + c * LANES)
            return acc
        acc = lax.fori_loop(0, n_tiles, body, jnp.zeros((tq, LANES), F32))
        return jnp.broadcast_to(jnp.sum(acc, axis=1, keepdims=True), (tq, LANES))

    def bisect(it, u):
        trial = u | jnp.left_shift(jnp.int32(1), 31 - it)
        thr = _key_to_float(trial)
        cnt = count(lambda blk, j0: jnp.where(blk >= thr, 1.0, 0.0))
        return jnp.where(cnt >= kf, trial, u)

    u = lax.fori_loop(0, 32, bisect, jnp.zeros((tq, LANES), I32))
    key = jnp.maximum(u ^ INT_MIN, INT_MIN + 0x00800000)
    thr = _key_to_float(key ^ INT_MIN)
    n_ge = count(lambda blk, j0: jnp.where(blk >= thr, 1.0, 0.0))
    tied = jnp.max(n_ge) > kf

    def write_bias(sel):
        def body(t, carry):
            k0 = pl.multiple_of(t * tk, tk)
            for c in range(ncol):
                blk = sc_ref[:, pl.ds(k0 + c * LANES, LANES)]
                sc_ref[:, pl.ds(k0 + c * LANES, LANES)] = sel(blk, k0 + c * LANES)
            return carry
        lax.fori_loop(0, n_tiles, body, 0)

    @pl.when(jnp.logical_not(tied))
    def _():
        write_bias(lambda blk, j0: jnp.where(blk >= thr, 0.0, NEG_INF))

    @pl.when(tied)
    def _():
        need = kf - count(lambda blk, j0: jnp.where(blk > thr, 1.0, 0.0))

        def jbis(it, ans):
            trial = ans + jnp.left_shift(jnp.int32(1), idx_bits - 1 - it)
            cnt = count(lambda blk, j0: jnp.where(blk == thr, jnp.where(j0 + lane < trial, 1.0, 0.0), 0.0))
            return jnp.where(cnt < need, trial, ans)

        ans = lax.fori_loop(0, idx_bits, jbis, jnp.zeros((tq, LANES), I32))
        write_bias(lambda blk, j0: jnp.where(
            blk > thr, 0.0, jnp.where(blk == thr, jnp.where(j0 + lane <= ans, 0.0, NEG_INF), NEG_INF)))

    m_ref[...] = jnp.full(m_ref.shape, M_INIT, F32)
    acc_ref[...] = jnp.zeros(acc_ref.shape, F32)

    def attend_tile(t, carry):
        k0 = pl.multiple_of(t * tk, tk)
        bias = sc_ref[:, pl.ds(k0, tk)]
        for g in range(N_KV_HEADS):
            qg = q_ref[0, g * group:(g + 1) * group].reshape(group * tq, HEAD_DIM)
            s = lax.dot_general(qg, k_ref[0, g, pl.ds(k0, tk), :], (((1,), (1,)), ((), ())),
                                preferred_element_type=F32)
            s = (s.reshape(group, tq, tk) + bias[None]).reshape(group * tq, tk)
            m_prev = m_ref[g]
            m_new = jnp.maximum(m_prev, jnp.max(s, axis=1, keepdims=True))
            p = jnp.exp2(s - jnp.concatenate([m_new] * ncol, axis=1))
            pv = jnp.dot(p.astype(BF16), v_ref[0, g, pl.ds(k0, tk), :], preferred_element_type=F32)
            acc_ref[g] = jnp.exp2(m_prev - m_new) * acc_ref[g] + pv
            m_ref[g] = m_new
        return carry

    lax.fori_loop(0, n_tiles, attend_tile, 0)

    for g in range(N_KV_HEADS):
        acc = acc_ref[g]
        o = acc[:, :HEAD_DIM] / acc[:, HEAD_DIM:HEAD_DIM + 1]
        for r in range(0, group, 2):
            pair = jnp.concatenate([o[r * tq:(r + 1) * tq], o[(r + 1) * tq:(r + 2) * tq]], axis=1)
            h0 = g * group + r
            o_ref[:, h0 * HEAD_DIM:(h0 + 2) * HEAD_DIM] = pair.astype(BF16)


def _dsa(qh, qih, kw_flat, kw_row_off, kh, vh, kih, *, tq, tk, k_sel, q_off, n_valid_end):
    nb, _, s, _ = qh.shape
    lp = kh.shape[2]
    nq = s // tq
    kern = functools.partial(_dsa_kernel, tq=tq, tk=tk, k_sel=k_sel, q_off=q_off, n_valid_end=n_valid_end,
                             idx_bits=max(1, int(lp).bit_length()))
    per_batch = lambda shp: pl.BlockSpec((1,) + shp, lambda b, i: (b,) + (0,) * len(shp), pipeline_mode=pl.Buffered(1))
    return pl.pallas_call(
        kern,
        out_shape=jax.ShapeDtypeStruct((nb * s, N_HEADS * HEAD_DIM), BF16),
        grid=(nb, nq),
        in_specs=[pl.BlockSpec((1, N_HEADS, tq, HEAD_DIM), lambda b, i: (b, 0, i, 0)),
                  pl.BlockSpec((1, IDX_HEADS, tq, IDX_DIM), lambda b, i: (b, 0, i, 0)),
                  pl.BlockSpec((tq, LANES), lambda b, i: (kw_row_off // tq + b * nq + i, 0)),
                  per_batch((N_KV_HEADS, lp, HEAD_DIM)), per_batch((N_KV_HEADS, lp, LANES)), per_batch((lp, IDX_DIM))],
        out_specs=pl.BlockSpec((tq, N_HEADS * HEAD_DIM), lambda b, i: (b * nq + i, 0)),
        scratch_shapes=[pltpu.VMEM((tq, (lp // LANES | 1) * LANES), F32),
                        pltpu.VMEM((N_KV_HEADS, (N_HEADS // N_KV_HEADS) * tq, LANES), F32),
                        pltpu.VMEM((N_KV_HEADS, (N_HEADS // N_KV_HEADS) * tq, LANES), F32)],
        compiler_params=_cparams(("arbitrary", "arbitrary")),
        name="dsa",
    )(qh, qih, kw_flat, kh, vh, kih)


def _dsa_t_kernel(q_ref, qi_ref, wt_ref, k_ref, vt_ref, ki_ref, o_ref, sc_ref, sch_ref, m_ref, mu_ref, acc_ref,
                  s_ref, s2_ref,
                  *, tk, k_sel, n_valid_end, idx_bits):
    tq = LANES
    i = pl.program_id(1)
    qf0 = i * tq
    klim = jnp.minimum(FRONT + qf0 + tq, n_valid_end)
    n_tiles = (klim + tk - 1) // tk
    kf = float(k_sel)
    group = N_HEADS // N_KV_HEADS
    fold = 32
    fold_h = 64
    assert sch_ref.shape[0] // fold_h <= 256

    wt = wt_ref[0]
    qlane = lax.broadcasted_iota(I32, (1, tq), 1)
    lim = jnp.minimum(FRONT + ((qf0 + qlane) // CHUNK + 1) * CHUNK, n_valid_end)
    qi_all = qi_ref[0].reshape(IDX_HEADS * tq, IDX_DIM)

    def score_tile(t, carry):
        th = tk // 2
        for half in range(2):
            k0 = pl.multiple_of(t * tk, tk) + half * th
            rel = lax.dot_general(ki_ref[0, pl.ds(k0, th), :], qi_all, (((1,), (1,)), ((), ())),
                                  preferred_element_type=F32)
            sc = jnp.zeros((th, tq), F32)
            for h in range(IDX_HEADS):
                sc = sc + wt[h:h + 1, :] * jnp.maximum(rel[:, h * tq:(h + 1) * tq], 0.0)
            j = k0 + lax.broadcasted_iota(I32, (th, tq), 0)
            j = jnp.where(j >= FRONT - N_META, j, n_valid_end)
            sc = jnp.where(j < lim, sc, NEG_INF)
            sc_ref[pl.ds(k0, th), :] = sc
            sch_ref[pl.ds(k0, th), :] = _high_half(sc).astype(HALF)
        return carry

    lax.fori_loop(0, n_tiles, score_tile, 0)

    def count(pred):
        def body(t, acc):
            k0 = pl.multiple_of(t * tk, tk)
            ones = pred(sc_ref[pl.ds(k0, tk), :], k0)
            return acc + jnp.sum(ones.reshape(tk // fold, fold, tq), axis=0)
        acc = lax.fori_loop(0, n_tiles, body, jnp.zeros((fold, tq), F32))
        return jnp.sum(acc, axis=0, keepdims=True)

    def count_high(thr):
        thr_h = _high_half(thr).astype(HALF)
        one, zero = jnp.ones((), HALF), jnp.zeros((), HALF)

        def body(t, acc):
            ones = jnp.where(sch_ref[pl.ds(pl.multiple_of(t * tk, tk), tk), :] >= thr_h, one, zero)
            for c in range(tk // fold_h):
                acc = acc + ones[c * fold_h:(c + 1) * fold_h]
            return acc
        acc = lax.fori_loop(0, n_tiles, body, jnp.zeros((fold_h, tq), HALF))
        return jnp.sum(acc.astype(F32), axis=0, keepdims=True)

    def bisect(counter):
        def step(it, u):
            trial = u | jnp.left_shift(jnp.int32(1), 31 - it)
            return jnp.where(counter(_key_to_float(trial)) >= kf, trial, u)
        return step

    u = lax.fori_loop(0, 16, bisect(count_high), jnp.zeros((1, tq), I32))
    u = lax.fori_loop(16, 32, bisect(lambda thr: count(lambda blk, k0: jnp.where(blk >= thr, 1.0, 0.0))), u)
    key = jnp.maximum(u ^ INT_MIN, SKEY_LOWEST_FINITE)
    thr = _skey_to_float(key)
    n_ge = count(lambda blk, k0: jnp.where(blk >= thr, 1.0, 0.0))
    tied = jnp.max(n_ge) > kf

    def write_bias(sel):
        def body(t, carry):
            k0 = pl.multiple_of(t * tk, tk)
            sc_ref[pl.ds(k0, tk), :] = sel(sc_ref[pl.ds(k0, tk), :], k0)
            return carry
        lax.fori_loop(0, n_tiles, body, 0)

    @pl.when(jnp.logical_not(tied))
    def _():
        write_bias(lambda blk, k0: jnp.where(blk >= thr, 0.0, NEG_INF))

    @pl.when(tied)
    def _():
        need = kf - count(lambda blk, k0: jnp.where(blk > thr, 1.0, 0.0))
        krow = lax.broadcasted_iota(I32, (tk, tq), 0)

        def jbis(it, ans):
            trial = ans + jnp.left_shift(jnp.int32(1), idx_bits - 1 - it)
            cnt = count(lambda blk, k0: jnp.where(blk == thr, jnp.where(k0 + krow < trial, 1.0, 0.0), 0.0))
            return jnp.where(cnt < need, trial, ans)

        ans = lax.fori_loop(0, idx_bits, jbis, jnp.zeros((1, tq), I32))
        write_bias(lambda blk, k0: jnp.where(
            blk > thr, 0.0, jnp.where(blk == thr, jnp.where(k0 + krow <= ans, 0.0, NEG_INF), NEG_INF)))

    m_ref[...] = jnp.full(m_ref.shape, M_INIT, F32)
    acc_ref[...] = jnp.zeros(acc_ref.shape, F32)

    mu_ref[...] = jnp.full(mu_ref.shape, M_INIT, F32)

    def logits(t, buf, g, m_before):
        k0 = pl.multiple_of(t * tk, tk)
        bias = sc_ref[pl.ds(k0, tk), :]
        qg = q_ref[0, g * group:(g + 1) * group].reshape(group * tq, HEAD_DIM)
        s = lax.dot_general(k_ref[0, g, pl.ds(k0, tk), :], qg, (((1,), (1,)), ((), ())),
                            preferred_element_type=F32)
        s = jnp.concatenate([s[:, r * tq:(r + 1) * tq] + bias for r in range(group)], axis=1)
        buf[g] = s
        return jnp.maximum(m_before, jnp.max(s, axis=0, keepdims=True))

    def values(t, buf, g, m_t):
        p = jnp.exp2(buf[g] - m_t)
        pv = jnp.dot(vt_ref[0, g, :, pl.ds(pl.multiple_of(t * tk, tk), tk)], p.astype(BF16),
                     preferred_element_type=F32)
        acc_ref[g] = jnp.exp2(mu_ref[g] - m_t) * acc_ref[g] + pv
        mu_ref[g] = m_t

    def step(t_next, buf_next, t, buf):
        for g in range(N_KV_HEADS):
            m_t = m_ref[g]
            m_ref[g] = logits(t_next, buf_next, g, m_t)
            values(t, buf, g, m_t)

    for g in range(N_KV_HEADS):
        m_ref[g] = logits(0, s_ref, g, m_ref[g])

    def tile_pair(j, carry):
        step(2 * j + 1, s2_ref, 2 * j, s_ref)
        step(jnp.minimum(2 * j + 2, n_tiles - 1), s_ref, 2 * j + 1, s2_ref)
        return carry

    lax.fori_loop(0, n_tiles >> 1, tile_pair, 0)

    @pl.when((n_tiles & 1) == 1)
    def _():
        for g in range(N_KV_HEADS):
            values(n_tiles - 1, s_ref, g, m_ref[g])

    for g in range(N_KV_HEADS):
        acc = acc_ref[g]
        o = acc[:HEAD_DIM, :] / acc[HEAD_DIM:HEAD_DIM + 1, :]
        for r in range(0, group, 2):
            pair = jnp.concatenate([o[:, r * tq:(r + 1) * tq], o[:, (r + 1) * tq:(r + 2) * tq]], axis=0)
            h0 = g * group + r
            o_ref[:, h0 * HEAD_DIM:(h0 + 2) * HEAD_DIM] = pair.T.astype(BF16)


def _dsa_t(qh, qih, wt, kh, vt, kih, *, tk, k_sel, n_valid_end):
    nb, _, s, _ = qh.shape
    lp = kh.shape[2]
    tq = LANES
    nq = s // tq
    group = N_HEADS // N_KV_HEADS
    kern = functools.partial(_dsa_t_kernel, tk=tk, k_sel=k_sel, n_valid_end=n_valid_end,
                             idx_bits=max(1, int(lp).bit_length()))
    per_batch = lambda shp: pl.BlockSpec((1,) + shp, lambda b, i: (b,) + (0,) * len(shp), pipeline_mode=pl.Buffered(1))
    return pl.pallas_call(
        kern,
        out_shape=jax.ShapeDtypeStruct((nb * s, N_HEADS * HEAD_DIM), BF16),
        grid=(nb, nq),
        in_specs=[pl.BlockSpec((1, N_HEADS, tq, HEAD_DIM), lambda b, i: (b, 0, i, 0)),
                  pl.BlockSpec((1, IDX_HEADS, tq, IDX_DIM), lambda b, i: (b, 0, i, 0)),
                  pl.BlockSpec((1, IDX_HEADS, tq), lambda b, i: (b, 0, i)),
                  per_batch((N_KV_HEADS, lp, HEAD_DIM)), per_batch((N_KV_HEADS, VT_ROWS, lp)), per_batch((lp, IDX_DIM))],
        out_specs=pl.BlockSpec((tq, N_HEADS * HEAD_DIM), lambda b, i: (b * nq + i, 0)),
        scratch_shapes=[pltpu.VMEM((lp, tq), F32), pltpu.VMEM((lp, tq), HALF),
                        pltpu.VMEM((N_KV_HEADS, 1, group * tq), F32),
                        pltpu.VMEM((N_KV_HEADS, 1, group * tq), F32),
                        pltpu.VMEM((N_KV_HEADS, VT_ROWS, group * tq), F32),
                        pltpu.VMEM((N_KV_HEADS, tk, group * tq), F32),
                        pltpu.VMEM((N_KV_HEADS, tk, group * tq), F32)],
        compiler_params=_cparams(("arbitrary", "arbitrary")),
        name="dsa_t",
    )(qh, qih, wt, kh, vt, kih)


def _pad_keys(meta_rows, mid_rows, lp):
    nb = mid_rows[0].shape[0]
    tail = mid_rows[0].shape[2:]
    meta_b = jnp.broadcast_to(meta_rows[None], (nb,) + meta_rows.shape)
    used = FRONT + sum(m.shape[1] for m in mid_rows)
    parts = [jnp.zeros((nb, FRONT - N_META) + tail, meta_rows.dtype), meta_b] + list(mid_rows)
    if lp > used:
        parts.append(jnp.zeros((nb, lp - used) + tail, meta_rows.dtype))
    return jnp.concatenate(parts, axis=1)


def _attention_inputs(k_meta, v_meta, ki_meta, k_mid, v_mid, ki_mid, lp, keys_on_lanes):
    nb = k_mid[0].shape[0]
    kp = _pad_keys(k_meta, k_mid, lp).astype(BF16).reshape(nb, lp, N_KV_HEADS, HEAD_DIM).transpose(0, 2, 1, 3)
    vp = _pad_keys(v_meta, v_mid, lp).astype(BF16).reshape(nb, lp, N_KV_HEADS, HEAD_DIM)
    if keys_on_lanes:
        vp = vp.transpose(0, 2, 3, 1)
        ones = jnp.ones((nb, N_KV_HEADS, 1, lp), BF16)
        vp = jnp.concatenate([vp, ones, jnp.zeros((nb, N_KV_HEADS, VT_ROWS - HEAD_DIM - 1, lp), BF16)], axis=2)
    else:
        vp = vp.transpose(0, 2, 1, 3)
        ones = jnp.ones(vp.shape[:3] + (1,), BF16)
        vp = jnp.concatenate([vp, ones, jnp.zeros(vp.shape[:3] + (LANES - HEAD_DIM - 1,), BF16)], axis=3)
    kip = _pad_keys(ki_meta, ki_mid, lp).astype(BF16)
    return kp, vp, kip


def _merge_kernel(xp_ref, xs_ref, cp_ref, cs_ref, ap_ref, as_ref, gate_ref, wao_ref, wout_ref, gffn_ref, wr_ref, br_ref,
                  x1_ref, hn_ref, ids_ref, gts_ref, *, n_first):
    first = pl.program_id(0) < n_first
    x = jnp.where(first, xp_ref[...], xs_ref[...])
    conv = jnp.where(first, cp_ref[...], cs_ref[...]).astype(F32)
    attn = jnp.where(first, ap_ref[...], as_ref[...])
    ao = jnp.dot(attn, wao_ref[...], preferred_element_type=F32)
    gates = gate_ref[...].astype(F32)
    mix = gates[:, :D_MODEL] * conv + gates[:, D_MODEL:] * ao
    x1 = x + jnp.dot(mix.astype(BF16), wout_ref[...], preferred_element_type=F32)
    x1_ref[...] = x1
    hn = x1 * lax.rsqrt(jnp.mean(x1 * x1, axis=-1, keepdims=True) + EPS) * gffn_ref[...]
    tr = hn.shape[0]
    for c in range(SUB):
        hn_ref[pl.ds(c, tr, stride=SUB), :] = hn[:, c * LANES:(c + 1) * LANES]
    logits = jnp.dot(hn, wr_ref[...], preferred_element_type=F32, precision=lax.Precision.HIGHEST) + br_ref[...]
    lane = lax.broadcasted_iota(I32, logits.shape, 1)
    lg = jnp.where(lane < N_EXPERTS, logits, NEG_INF)
    vals, ids = [], []
    for _ in range(TOP_K_EXPERTS):
        m = jnp.max(lg, axis=1, keepdims=True)
        idx = jnp.min(jnp.where(lg == m, lane, LANES), axis=1, keepdims=True)
        vals.append(m)
        ids.append(idx)
        lg = jnp.where(lane == idx, NEG_INF, lg)
    es = [jnp.exp(v - vals[0]) for v in vals]
    den = es[0]
    for e in es[1:]:
        den = den + e
    ids_out = jnp.zeros(logits.shape, I32)
    gts_out = jnp.zeros(logits.shape, F32)
    for j in range(TOP_K_EXPERTS):
        ids_out = jnp.where(lane == j, ids[j], ids_out)
        gts_out = jnp.where(lane == j, es[j] / den, gts_out)
    ids_ref[...] = ids_out
    gts_ref[...] = gts_out


def _merge(xp, xs, conv_p, conv_s, attn_p, attn_s, gates, wao, wout, gffn, wr, br, tr):
    n_first, n_second = xp.shape[0] // tr, xs.shape[0] // tr
    rows = xp.shape[0] + xs.shape[0]
    first = lambda w: pl.BlockSpec((tr, w), lambda i: (jnp.minimum(i, n_first - 1), 0))
    second = lambda w: pl.BlockSpec((tr, w), lambda i: (jnp.maximum(i - n_first, 0), 0))
    row = lambda w: pl.BlockSpec((tr, w), lambda i: (i, 0))
    return pl.pallas_call(
        functools.partial(_merge_kernel, n_first=n_first),
        out_shape=[jax.ShapeDtypeStruct((rows, D_MODEL), F32), jax.ShapeDtypeStruct((rows * SUB, LANES), F32),
                   jax.ShapeDtypeStruct((rows, LANES), I32), jax.ShapeDtypeStruct((rows, LANES), F32)],
        grid=(n_first + n_second,),
        in_specs=[first(D_MODEL), second(D_MODEL), first(D_MODEL), second(D_MODEL), first(D_MODEL), second(D_MODEL),
                  row(2 * D_MODEL), _resident(wao), _resident(wout), _resident(gffn), _resident(wr), _resident(br)],
        out_specs=[row(D_MODEL), pl.BlockSpec((tr * SUB, LANES), lambda i: (i, 0)), row(LANES), row(LANES)],
        compiler_params=_cparams(("arbitrary",)),
        name="merge",
    )(xp, xs, conv_p, conv_s, attn_p, attn_s, gates, wao, wout, gffn, wr, br)


KEY_ALIGN = 1024


def _key_window(tm):
    return -(-(tm + KEY_ALIGN - 1) // KEY_ALIGN) * KEY_ALIGN


def _for_rows(n, fn):
    def group(gi, carry):
        for u in range(SUB):
            fn(gi * SUB + u)
        return carry

    def single(r, carry):
        fn(r)
        return carry

    shift = SUB.bit_length() - 1
    lax.fori_loop(0, n >> shift, group, 0)
    lax.fori_loop((n >> shift) << shift, n, single, 0)


def _moe_kernel(te_ref, base_ref, nval_ref, nreal_ref, keys_hbm, h_hbm, w1_ref, b1_ref, w3_ref, b3_ref, w2_ref, b2_ref,
                y_hbm, key_sm, xbuf, xb, ybuf, ksem, gsem, ssem, *, tm, n_tok, pair_bits):
    del te_ref
    i = pl.program_id(0)
    nreal = nreal_ref[0]
    window = key_sm.shape[0] // 3
    slot_shift = TOP_K_EXPERTS.bit_length() - 1

    def keys_copy(tile):
        s = lax.rem(tile, 3)
        start = pl.multiple_of(base_ref[tile] & -KEY_ALIGN, KEY_ALIGN)
        return pltpu.make_async_copy(keys_hbm.at[pl.ds(start, window)],
                                     key_sm.at[pl.ds(pl.multiple_of(s * window, KEY_ALIGN), window)], ksem.at[s])

    def pair_reader(tile):
        first = lax.rem(tile, 3) * window + (base_ref[tile] & (KEY_ALIGN - 1))
        return lambda r: key_sm[first + r] & ((1 << pair_bits) - 1)

    def gather_row(s, r, token):
        return pltpu.make_async_copy(h_hbm.at[pl.ds(token * SUB, SUB)], xbuf.at[s, pl.ds(r * SUB, SUB)], gsem.at[s])

    def scatter_row(r, out_token):
        return pltpu.make_async_copy(ybuf.at[pl.ds(r * SUB, SUB)], y_hbm.at[pl.ds(out_token * SUB, SUB)], ssem.at[0])

    def start_gather(tile):
        pair_of, s = pair_reader(tile), tile & 1
        _for_rows(tm, lambda r: gather_row(s, r, pair_of(r) >> slot_shift).start())

    @pl.when(jnp.logical_and(i == 0, nreal > 0))
    def _():
        keys_copy(0).start()

        @pl.when(nreal > 1)
        def _():
            keys_copy(1).start()

        keys_copy(0).wait()
        start_gather(0)

    @pl.when(i + 1 < nreal)
    def _():
        keys_copy(i + 1).wait()
        start_gather(i + 1)

    @pl.when(i + 2 < nreal)
    def _():
        keys_copy(i + 2).start()

    @pl.when(i < nreal)
    def _():
        s = i & 1
        _for_rows(tm, lambda r: gather_row(s, r, 0).wait())
        for c in range(SUB):
            xb[:, c * LANES:(c + 1) * LANES] = xbuf[s, pl.ds(c, tm, stride=SUB), :].astype(BF16)
        x = xb[...]
        g = jnp.minimum(jnp.dot(x, w1_ref[0], preferred_element_type=F32) + b1_ref[0], SWIGLU_LIMIT)
        u = jnp.clip(jnp.dot(x, w3_ref[0], preferred_element_type=F32) + b3_ref[0], -SWIGLU_LIMIT, SWIGLU_LIMIT)
        act = g * jax.nn.sigmoid(SWIGLU_ALPHA * g) * (u + 1.0)
        y = jnp.dot(act.astype(BF16), w2_ref[0], preferred_element_type=F32) + b2_ref[0]

        @pl.when(i > 0)
        def _():
            _for_rows(nval_ref[i - 1], lambda r: scatter_row(r, 0).wait())

        for c in range(SUB):
            ybuf[pl.ds(c, tm, stride=SUB), :] = y[:, c * LANES:(c + 1) * LANES]

        pair_of = pair_reader(i)

        def put(r):
            pair = pair_of(r)
            scatter_row(r, (pair & (TOP_K_EXPERTS - 1)) * n_tok + (pair >> slot_shift)).start()

        _for_rows(nval_ref[i], put)

        @pl.when(i == nreal - 1)
        def _():
            _for_rows(nval_ref[i], lambda r: scatter_row(r, 0).wait())


def _moe(hn, keys, tile_expert, base, nval, n_real, w1, b1, w3, b3, w2, b2, n_tok, pair_bits, tm):
    nt = tile_expert.shape[0]
    by_expert = lambda i, te, *_: (te[i], 0, 0)
    wspec = pl.BlockSpec((1, D_MODEL, D_FF), by_expert)
    w2spec = pl.BlockSpec((1, D_FF, D_MODEL), by_expert)
    bspec = pl.BlockSpec((1, 1, D_FF), by_expert)
    any_spec = pl.BlockSpec(memory_space=pl.ANY)
    grid_spec = pltpu.PrefetchScalarGridSpec(
        num_scalar_prefetch=4,
        grid=(nt,),
        in_specs=[any_spec, any_spec, wspec, bspec, wspec, bspec, w2spec, bspec],
        out_specs=any_spec,
        scratch_shapes=[pltpu.SMEM((3 * _key_window(tm),), I32),
                        pltpu.VMEM((2, tm * SUB, LANES), F32), pltpu.VMEM((tm, D_MODEL), BF16),
                        pltpu.VMEM((tm * SUB, LANES), F32),
                        pltpu.SemaphoreType.DMA((3,)), pltpu.SemaphoreType.DMA((2,)), pltpu.SemaphoreType.DMA((1,))],
    )
    return pl.pallas_call(
        functools.partial(_moe_kernel, tm=tm, n_tok=n_tok, pair_bits=pair_bits),
        out_shape=jax.ShapeDtypeStruct((TOP_K_EXPERTS * n_tok * SUB, LANES), F32),
        grid_spec=grid_spec,
        compiler_params=_cparams(("arbitrary",)),
        name="moe",
    )(tile_expert, base, nval, n_real, keys, hn, w1, b1, w3, b3, w2, b2)


def _route(ids, n, tm):
    n_pairs = n * TOP_K_EXPERTS
    pair_bits = max(1, (n_pairs - 1).bit_length())
    assert pair_bits + (N_EXPERTS - 1).bit_length() <= 31 and tm % LANES == 0
    assert TOP_K_EXPERTS & (TOP_K_EXPERTS - 1) == 0
    pair = jnp.arange(n_pairs, dtype=I32)
    keys = jnp.sort(jnp.left_shift(ids.reshape(-1), pair_bits) | pair)
    starts = jnp.searchsorted(keys, jnp.left_shift(jnp.arange(N_EXPERTS + 1, dtype=I32), pair_bits)).astype(I32)
    counts = starts[1:] - starts[:-1]
    tiles = (counts + tm - 1) // tm
    tile_end = jnp.cumsum(tiles)
    nt = -(-(n_pairs + N_EXPERTS * (tm - 1)) // tm)
    ti = jnp.arange(nt, dtype=I32)
    te = jnp.minimum(jnp.searchsorted(tile_end, ti, side="right"), N_EXPERTS - 1).astype(I32)
    within = ti - (tile_end - tiles)[te]
    base = jnp.clip(starts[:-1][te] + within * tm, 0, n_pairs - 1).astype(I32)
    nval = jnp.clip(counts[te] - within * tm, 0, tm).astype(I32)
    padded = -(-n_pairs // KEY_ALIGN) * KEY_ALIGN + _key_window(tm)
    keys = jnp.pad(keys, (0, padded - n_pairs))
    return keys, te, base, nval, tile_end[-1:].astype(I32), pair_bits


def _final_kernel(x1_ref, y0_ref, y1_ref, y2_ref, y3_ref, gts_ref, g_ref, out_ref):
    gts = gts_ref[...]
    tr = gts.shape[0]
    cols = []
    for c in range(SUB):
        acc = x1_ref[:, c * LANES:(c + 1) * LANES]
        for j, y_ref in enumerate((y0_ref, y1_ref, y2_ref, y3_ref)):
            acc = acc + gts[:, j:j + 1] * y_ref[pl.ds(c, tr, stride=SUB), :]
        cols.append(acc)
    ss = cols[0] * cols[0]
    for acc in cols[1:]:
        ss = ss + acc * acc
    scale = lax.rsqrt(jnp.sum(ss, axis=-1, keepdims=True) / D_MODEL + EPS)
    for c, acc in enumerate(cols):
        out_ref[:, c * LANES:(c + 1) * LANES] = acc * scale * g_ref[:, c * LANES:(c + 1) * LANES]


def _final(x1, yslots, gts, g, tr):
    n = x1.shape[0]
    nblk = n // tr
    row = lambda w: pl.BlockSpec((tr, w), lambda i: (i, 0))
    slot = lambda j: pl.BlockSpec((tr * SUB, LANES), lambda i: (j * nblk + i, 0))
    return pl.pallas_call(
        _final_kernel,
        out_shape=jax.ShapeDtypeStruct((n, D_MODEL), F32),
        grid=(nblk,),
        in_specs=[row(D_MODEL)] + [slot(j) for j in range(TOP_K_EXPERTS)] + [row(LANES), _resident(g)],
        out_specs=row(D_MODEL),
        compiler_params=_cparams(("arbitrary",)),
        name="final",
    )(x1, yslots, yslots, yslots, yslots, gts, g)


def _tile(n, pref):
    t = min(n, pref)
    while n % t:
        t //= 2
    return t


def kernel(x_prompt, x_sample, cache_k, cache_v, cache_idx_k, state_conv, meta_tokens, g_norm_mix, w_in, conv_w,
           conv_b, conv_ln_g, conv_ln_b, w_conv_out, w_attn_out, w_out, g_norm_ffn, w_router, b_router,
           w1, b1, w3, b3, w2, b2, g_norm_final):
    nb, s_p, _ = x_prompt.shape
    db, s_n, _ = x_sample.shape
    past = cache_k.shape[1]
    rows_p, rows_s = nb * s_p, db * s_n
    n = rows_p + rows_s
    tr = _tile(rows_s, ROW_TILE)
    assert rows_p % tr == 0 and s_p % tr == 0 and tr % s_n == 0 and s_p % Q_TILE == 0

    cuts = [0, 2 * CONV_CH, N_HEADS * HEAD_DIM, N_KV_HEADS * HEAD_DIM, N_KV_HEADS * HEAD_DIM, IDX_HEADS * IDX_DIM,
            IDX_DIM + IDX_HEADS, 2 * D_MODEL]
    offs = [sum(cuts[:i + 1]) for i in range(len(cuts))]
    seg = [w_in[:, offs[i]:offs[i + 1]].astype(BF16) for i in range(len(cuts) - 1)]
    seg[5] = jnp.pad(seg[5], ((0, 0), (0, LANES - seg[5].shape[1])))
    g_mix = g_norm_mix.reshape(1, D_MODEL)

    pos_p = N_META + jnp.arange(s_p, dtype=I32)
    pos_s = jnp.tile(N_META + past + jnp.arange(s_n, dtype=I32), tr // s_n)
    tabs = _rope_tables(jnp.concatenate([pos_p, pos_s]))
    blocks_per_seq = s_p // tr
    n_first = rows_p // tr
    tab_map = lambda i: jnp.where(i < n_first, lax.rem(i, blocks_per_seq), blocks_per_seq)

    xp, xs = x_prompt.reshape(rows_p, D_MODEL), x_sample.reshape(rows_s, D_MODEL)
    a, q, k, v, qi, kw, gates = _project(xp, xs, tabs, tab_map, g_mix, seg, tr)

    meta = meta_tokens.astype(F32)
    tabs_m = _rope_tables(jnp.arange(N_META, dtype=I32))
    a_m, _, k_m, v_m, _, kw_m, _ = _project(meta, meta, tabs_m, lambda i: 0, g_mix, seg, N_META)
    a_m, k_m, v_m, ki_m = a_m[:N_META], k_m[:N_META], v_m[:N_META], kw_m[:N_META, :IDX_DIM]

    cw = jnp.pad(conv_w, ((0, HALO - CONV_WIDTH), (0, 0)))
    cb, lg, lb = conv_b.reshape(1, -1), conv_ln_g.reshape(1, -1), conv_ln_b.reshape(1, -1)
    wco = w_conv_out.astype(BF16)
    ctx_p = jnp.concatenate([jnp.zeros((HALO - N_META, CONV_CH), F32), a_m], axis=0)[None]
    ctx_s = jnp.pad(state_conv, ((0, 0), (HALO - (CONV_WIDTH - 1), 0), (0, 0)))
    conv_p = _conv_branch(a, 0, nb, s_p, ctx_p, cw, cb, lg, lb, wco, _tile(s_p, 256))
    conv_s = _conv_branch(a, rows_p, db, s_n, ctx_s, cw, cb, lg, lb, wco, s_n)

    k_p3, v_p3 = k[:rows_p].reshape(nb, s_p, -1), v[:rows_p].reshape(nb, s_p, -1)
    ki_p3 = kw[:rows_p, :IDX_DIM].reshape(nb, s_p, IDX_DIM)
    k_s3, v_s3 = k[rows_p:].reshape(db, s_n, -1), v[rows_p:].reshape(db, s_n, -1)
    ki_s3 = kw[rows_p:, :IDX_DIM].reshape(db, s_n, IDX_DIM)

    def heads(t, rows0, b_, s_, nh):
        return t[rows0:rows0 + b_ * s_].reshape(b_, s_, nh, -1).transpose(0, 2, 1, 3)

    tk_p = _tile(s_p, K_TILE)
    lp_p = -(-(FRONT + s_p) // tk_p) * tk_p
    kp, vtp, kip = _attention_inputs(k_m, v_m, ki_m, [k_p3], [v_p3], [ki_p3], lp_p, True)
    wt_p = kw[:rows_p, IDX_DIM:IDX_DIM + IDX_HEADS].reshape(nb, s_p, IDX_HEADS).transpose(0, 2, 1)
    attn_p = _dsa_t(heads(q, 0, nb, s_p, N_HEADS), heads(qi, 0, nb, s_p, IDX_HEADS), wt_p, kp, vtp, kip,
                    tk=tk_p, k_sel=min(TOPK_MAX, s_p // 4), n_valid_end=FRONT + s_p)

    tk_s = K_TILE
    lp_s = -(-(FRONT + past + s_n) // tk_s) * tk_s
    ck = cache_k.reshape(db, past, -1)
    cv = cache_v.reshape(db, past, -1)
    ks, vs, kis = _attention_inputs(k_m, v_m, ki_m, [ck, k_s3], [cv, v_s3], [cache_idx_k, ki_s3], lp_s, False)
    attn_s = _dsa(heads(q, rows_p, db, s_n, N_HEADS), heads(qi, rows_p, db, s_n, IDX_HEADS), kw, rows_p, ks, vs, kis,
                  tq=s_n, tk=tk_s, k_sel=min(TOPK_MAX, (past + s_n) // 4), q_off=past,
                  n_valid_end=FRONT + past + s_n)

    wr = jnp.pad(w_router, ((0, 0), (0, LANES - N_EXPERTS)))
    br = jnp.pad(b_router, (0, LANES - N_EXPERTS)).reshape(1, LANES)
    x1, hn, ids, gts = _merge(xp, xs, conv_p, conv_s, attn_p, attn_s, gates, w_attn_out.astype(BF16),
                              w_out.astype(BF16), g_norm_ffn.reshape(1, -1), wr, br, tr)

    tm = MOE_TILE
    keys, tile_expert, base, nval, n_real, pair_bits = _route(ids[:, :TOP_K_EXPERTS], n, tm)
    yslots = _moe(hn, keys, tile_expert, base, nval, n_real, w1.astype(BF16), b1.reshape(N_EXPERTS, 1, D_FF),
                  w3.astype(BF16), b3.reshape(N_EXPERTS, 1, D_FF), w2.astype(BF16),
                  b2.reshape(N_EXPERTS, 1, D_MODEL), n, pair_bits, tm)
    out = _final(x1, yslots, gts, g_norm_final.reshape(1, -1), tr)

    y_prompt = out[:rows_p].reshape(nb, s_p, D_MODEL)
    y_sample = out[rows_p:].reshape(db, s_n, D_MODEL)
    lead = lambda m: jnp.broadcast_to(m[None], (nb,) + m.shape)
    k_prompt = jnp.concatenate([lead(k_m), k_p3], axis=1).reshape(nb, N_META + s_p, N_KV_HEADS, HEAD_DIM)
    v_prompt = jnp.concatenate([lead(v_m), v_p3], axis=1).reshape(nb, N_META + s_p, N_KV_HEADS, HEAD_DIM)
    idxk_prompt = jnp.concatenate([lead(ki_m), ki_p3], axis=1)
    a_p3 = a[:rows_p].reshape(nb, s_p, CONV_CH)
    a_s3 = a[rows_p:].reshape(db, s_n, CONV_CH)
    keep = CONV_WIDTH - 1
    conv_state_p = jnp.concatenate([lead(a_m), a_p3], axis=1)[:, N_META + s_p - keep:]
    conv_state_s = jnp.concatenate([state_conv, a_s3], axis=1)[:, s_n:]
    return (y_prompt, y_sample, k_prompt, v_prompt, idxk_prompt, conv_state_p,
            k_s3.reshape(db, s_n, N_KV_HEADS, HEAD_DIM), v_s3.reshape(db, s_n, N_KV_HEADS, HEAD_DIM), ki_s3,
            conv_state_s)
```

```python
import functools

import jax
import jax.numpy as jnp
from jax import lax
from jax.experimental import pallas as pl
from jax.experimental.pallas import tpu as pltpu

F32, BF16, I32 = jnp.float32, jnp.bfloat16, jnp.int32
HALF = jnp.bfloat16

D_MODEL = 1024
CHUNK = 64
N_META = 16
CONV_CH = D_MODEL
CONV_WIDTH = 31
N_HEADS = 16
N_KV_HEADS = 4
HEAD_DIM = 64
ROT_DIM = HEAD_DIM // 4
ROPE_THETA = 500000.0
IDX_HEADS = 8
IDX_DIM = 64
TOPK_MAX = 256
N_EXPERTS = 32
TOP_K_EXPERTS = 4
D_FF = D_MODEL
SWIGLU_LIMIT = 7.0
SWIGLU_ALPHA = 1.702
EPS = 1e-5
Q_SCALE = HEAD_DIM ** -0.5 * 1.4426950408889634

LANES = 128
SUB = 8
VT_ROWS = HEAD_DIM + 2 * SUB
FRONT = 128
HALO = 32
VMEM_LIMIT = 56 * 1024 * 1024
NEG_INF = float("-inf")
INT_MIN = -2 ** 31
M_INIT = -1e30

ROW_TILE = 512
Q_TILE = 128
K_TILE = 512
K_TILE_T = 1024
MOE_TILE = 512


def _cparams(sem):
    return pltpu.CompilerParams(dimension_semantics=sem, vmem_limit_bytes=VMEM_LIMIT)


def _resident(arr):
    nd = arr.ndim
    return pl.BlockSpec(arr.shape, lambda *_: (0,) * nd, pipeline_mode=pl.Buffered(1))


def _rope_cols(x, c, s1, s2):
    outs = []
    for j in range(x.shape[1] // LANES):
        xj = x[:, j * LANES:(j + 1) * LANES]
        outs.append(xj * c + pltpu.roll(xj, LANES - ROT_DIM // 2, 1) * s1 + pltpu.roll(xj, ROT_DIM // 2, 1) * s2)
    return outs


def _proj_kernel(xp_ref, xs_ref, g_ref, wa_ref, wq_ref, wk_ref, wv_ref, wqi_ref, wkw_ref, wgl_ref,
                 c_ref, s1_ref, s2_ref,
                 a_ref, q_ref, k_ref, v_ref, qi_ref, kw_ref, gate_ref, *, n_first):
    x = jnp.where(pl.program_id(0) < n_first, xp_ref[...], xs_ref[...])
    h = x * lax.rsqrt(jnp.mean(x * x, axis=-1, keepdims=True) + EPS) * g_ref[...]
    hb = h.astype(BF16)
    c, s1, s2 = c_ref[...], s1_ref[...], s2_ref[...]

    def dot(w_ref):
        return jnp.dot(hb, w_ref[...], preferred_element_type=F32)

    glu = dot(wa_ref)
    a_ref[...] = glu[:, :CONV_CH] * jax.nn.sigmoid(glu[:, CONV_CH:])
    for j, col in enumerate(_rope_cols(dot(wq_ref), c, s1, s2)):
        q_ref[:, j * LANES:(j + 1) * LANES] = (col * Q_SCALE).astype(BF16)
    for j, col in enumerate(_rope_cols(dot(wk_ref), c, s1, s2)):
        k_ref[:, j * LANES:(j + 1) * LANES] = col
    v_ref[...] = dot(wv_ref)
    for j, col in enumerate(_rope_cols(dot(wqi_ref), c, s1, s2)):
        qi_ref[:, j * LANES:(j + 1) * LANES] = (col * IDX_DIM ** -0.5).astype(BF16)
    pkw = dot(wkw_ref)
    lane = lax.broadcasted_iota(I32, pkw.shape, 1)
    kw_ref[...] = jnp.where(lane < IDX_DIM, _rope_cols(pkw, c, s1, s2)[0], pkw * IDX_HEADS ** -0.5)
    gate_ref[...] = jax.nn.sigmoid(dot(wgl_ref)).astype(BF16)


def _rope_tables(pos):
    half = ROT_DIM // 2
    inv = ROPE_THETA ** (-jnp.arange(half, dtype=F32) / half)
    ang = pos.astype(F32)[:, None] * inv[None, :]
    cos, sin = jnp.cos(ang), jnp.sin(ang)
    r = pos.shape[0]
    z = lambda n: jnp.zeros((r, n), F32)
    c = jnp.concatenate([cos, cos, jnp.ones((r, HEAD_DIM - ROT_DIM), F32)], axis=1)
    s1 = jnp.concatenate([-sin, z(HEAD_DIM - half)], axis=1)
    s2 = jnp.concatenate([z(half), sin, z(HEAD_DIM - ROT_DIM)], axis=1)
    two = lambda t: jnp.concatenate([t, t], axis=1)
    return two(c), two(s1), two(s2)


def _project(xp, xs, tabs, tab_map, g, ws, tr):
    n_first, n_second = xp.shape[0] // tr, xs.shape[0] // tr
    rows = xp.shape[0] + xs.shape[0]
    row = lambda w: pl.BlockSpec((tr, w), lambda i: (i, 0))
    tab = pl.BlockSpec((tr, LANES), lambda i: (tab_map(i), 0))
    in_specs = [pl.BlockSpec((tr, D_MODEL), lambda i: (jnp.minimum(i, n_first - 1), 0)),
                pl.BlockSpec((tr, D_MODEL), lambda i: (jnp.maximum(i - n_first, 0), 0)),
                _resident(g)] + [_resident(w) for w in ws] + [tab, tab, tab]
    widths = [(CONV_CH, F32), (N_HEADS * HEAD_DIM, BF16), (N_KV_HEADS * HEAD_DIM, F32), (N_KV_HEADS * HEAD_DIM, F32),
              (IDX_HEADS * IDX_DIM, BF16), (LANES, F32), (2 * D_MODEL, BF16)]
    return pl.pallas_call(
        functools.partial(_proj_kernel, n_first=n_first),
        out_shape=[jax.ShapeDtypeStruct((rows, w), dt) for w, dt in widths],
        grid=(n_first + n_second,),
        in_specs=in_specs,
        out_specs=[row(w) for w, _ in widths],
        compiler_params=_cparams(("arbitrary",)),
        name="proj",
    )(xp, xs, g, *ws, *tabs)


def _conv_kernel(cur_ref, halo_ref, lc_ref, cw_ref, cb_ref, lg_ref, lb_ref, wo_ref, out_ref, e_ref, c_ref, *, tr, rc):
    halo = jnp.where(pl.program_id(1) == 0, lc_ref[0], halo_ref[...])
    e_ref[0, 0:HALO, :] = halo
    e_ref[0, HALO:HALO + tr, :] = cur_ref[...]
    n = tr + HALO - 8
    for b in range(1, 8):
        e_ref[b, 0:n, :] = e_ref[0, b:b + n, :]
    first = HALO - (CONV_WIDTH - 1)

    def chunk(ci, carry):
        r0 = pl.multiple_of(ci * rc, rc)
        acc = jnp.zeros((rc, CONV_CH), F32) + cb_ref[...]
        for w in range(CONV_WIDTH):
            s = first + w
            acc = acc + e_ref[s % 8, pl.ds(r0 + 8 * (s // 8), rc), :] * cw_ref[pl.ds(w, 1), :]
        c_ref[pl.ds(r0, rc), :] = acc
        return carry

    lax.fori_loop(0, tr // rc, chunk, 0)
    c = c_ref[...]
    mu = jnp.mean(c, axis=-1, keepdims=True)
    xc = c - mu
    y = xc * lax.rsqrt(jnp.mean(xc * xc, axis=-1, keepdims=True) + EPS) * lg_ref[...] + lb_ref[...]
    act = (y * jax.nn.sigmoid(y)).astype(BF16)
    out_ref[...] = jnp.dot(act, wo_ref[...], preferred_element_type=F32).astype(BF16)


def _conv_branch(a_flat, row_off, nb, t, left_ctx, cw, cb, lg, lb, wo, tr):
    nt = t // tr
    rc = min(tr, 32)
    per_batch_ctx = left_ctx.shape[0] > 1
    cur = pl.BlockSpec((tr, CONV_CH), lambda b, i: (row_off // tr + b * nt + i, 0))
    halo = pl.BlockSpec((HALO, CONV_CH),
                        lambda b, i: (jnp.maximum((row_off + b * t + i * tr) // HALO - 1, 0), 0))
    lctx = pl.BlockSpec((1, HALO, CONV_CH), lambda b, i: (b if per_batch_ctx else 0, 0, 0))
    return pl.pallas_call(
        functools.partial(_conv_kernel, tr=tr, rc=rc),
        out_shape=jax.ShapeDtypeStruct((nb * t, D_MODEL), BF16),
        grid=(nb, nt),
        in_specs=[cur, halo, lctx, _resident(cw), _resident(cb), _resident(lg), _resident(lb), _resident(wo)],
        out_specs=pl.BlockSpec((tr, D_MODEL), lambda b, i: (b * nt + i, 0)),
        scratch_shapes=[pltpu.VMEM((8, tr + HALO, CONV_CH), F32), pltpu.VMEM((tr, CONV_CH), F32)],
        compiler_params=_cparams(("arbitrary", "arbitrary")),
        name="conv",
    )(a_flat, a_flat, left_ctx, cw, cb, lg, lb, wo)


def _float_to_skey(f):
    bits = lax.bitcast_convert_type(f, I32)
    return bits ^ ((bits >> 31) & 0x7FFFFFFF)


def _skey_to_float(key):
    return lax.bitcast_convert_type(key ^ ((key >> 31) & 0x7FFFFFFF), F32)


SKEY_LOWEST_FINITE = INT_MIN + 0x00800000


def _high_half(f):
    return lax.bitcast_convert_type(lax.bitcast_convert_type(f, I32) & -65536, F32)


def _key_to_float(u):
    key = u ^ INT_MIN
    bits = key ^ ((key >> 31) & 0x7FFFFFFF)
    return lax.bitcast_convert_type(bits, F32)


def _dsa_kernel(q_ref, qi_ref, kw_ref, k_ref, v_ref, ki_ref, o_ref, sc_ref, m_ref, acc_ref,
                *, tq, tk, k_sel, q_off, n_valid_end, idx_bits):
    i = pl.program_id(1)
    qf0 = q_off + i * tq
    klim = jnp.minimum(FRONT + ((qf0 + tq - 1) // CHUNK + 1) * CHUNK, n_valid_end)
    n_tiles = (klim + tk - 1) // tk
    ncol = tk // LANES
    kf = float(k_sel)
    group = N_HEADS // N_KV_HEADS

    kwv = kw_ref[...]
    qrow = lax.broadcasted_iota(I32, (tq, 1), 0)
    lim_row = jnp.minimum(FRONT + ((qf0 + qrow) // CHUNK + 1) * CHUNK, n_valid_end)

    def score_tile(t, carry):
        k0 = pl.multiple_of(t * tk, tk)
        kt = ki_ref[0, pl.ds(k0, tk), :]
        sc = jnp.zeros((tq, tk), F32)
        for h in range(IDX_HEADS):
            rel = lax.dot_general(qi_ref[0, h], kt, (((1,), (1,)), ((), ())), preferred_element_type=F32)
            sc = sc + kwv[:, IDX_DIM + h:IDX_DIM + h + 1] * jnp.maximum(rel, 0.0)
        j = k0 + lax.broadcasted_iota(I32, (tq, tk), 1)
        j = jnp.where(j >= FRONT - N_META, j, n_valid_end)
        sc_ref[:, pl.ds(k0, tk)] = jnp.where(j < lim_row, sc, NEG_INF)
        return carry

    lax.fori_loop(0, n_tiles, score_tile, 0)

    lane = lax.broadcasted_iota(I32, (tq, LANES), 1)

    def count(pred):
        def body(t, acc):
            k0 = pl.multiple_of(t * tk, tk)
            for c in range(ncol):
                blk = sc_ref[:, pl.ds(k0 + c * LANES, LANES)]
                acc = acc + pred(blk, k0 + c * LANES)
            return acc
        acc = lax.fori_loop(0, n_tiles, body, jnp.zeros((tq, LANES), F32))
        return jnp.broadcast_to(jnp.sum(acc, axis=1, keepdims=True), (tq, LANES))

    def bisect(it, u):
        trial = u | jnp.left_shift(jnp.int32(1), 31 - it)
        thr = _key_to_float(trial)
        cnt = count(lambda blk, j0: jnp.where(blk >= thr, 1.0, 0.0))
        return jnp.where(cnt >= kf, trial, u)

    u = lax.fori_loop(0, 32, bisect, jnp.zeros((tq, LANES), I32))
    key = jnp.maximum(u ^ INT_MIN, INT_MIN + 0x00800000)
    thr = _key_to_float(key ^ INT_MIN)
    n_ge = count(lambda blk, j0: jnp.where(blk >= thr, 1.0, 0.0))
    tied = jnp.max(n_ge) > kf

    def write_bias(sel):
        def body(t, carry):
            k0 = pl.multiple_of(t * tk, tk)
            for c in range(ncol):
                blk = sc_ref[:, pl.ds(k0 + c * LANES, LANES)]
                sc_ref[:, pl.ds(k0 + c * LANES, LANES)] = sel(blk, k0 + c * LANES)
            return carry
        lax.fori_loop(0, n_tiles, body, 0)

    @pl.when(jnp.logical_not(tied))
    def _():
        write_bias(lambda blk, j0: jnp.where(blk >= thr, 0.0, NEG_INF))

    @pl.when(tied)
    def _():
        need = kf - count(lambda blk, j0: jnp.where(blk > thr, 1.0, 0.0))

        def jbis(it, ans):
            trial = ans + jnp.left_shift(jnp.int32(1), idx_bits - 1 - it)
            cnt = count(lambda blk, j0: jnp.where(blk == thr, jnp.where(j0 + lane < trial, 1.0, 0.0), 0.0))
            return jnp.where(cnt < need, trial, ans)

        ans = lax.fori_loop(0, idx_bits, jbis, jnp.zeros((tq, LANES), I32))
        write_bias(lambda blk, j0: jnp.where(
            blk > thr, 0.0, jnp.where(blk == thr, jnp.where(j0 + lane <= ans, 0.0, NEG_INF), NEG_INF)))

    m_ref[...] = jnp.full(m_ref.shape, M_INIT, F32)
    acc_ref[...] = jnp.zeros(acc_ref.shape, F32)

    def attend_tile(t, carry):
        k0 = pl.multiple_of(t * tk, tk)
        bias = sc_ref[:, pl.ds(k0, tk)]
        for g in range(N_KV_HEADS):
            qg = q_ref[0, g * group:(g + 1) * group].reshape(group * tq, HEAD_DIM)
            s = lax.dot_general(qg, k_ref[0, g, pl.ds(k0, tk), :], (((1,), (1,)), ((), ())),
                                preferred_element_type=F32)
            s = (s.reshape(group, tq, tk) + bias[None]).reshape(group * tq, tk)
            m_prev = m_ref[g]
            m_new = jnp.maximum(m_prev, jnp.max(s, axis=1, keepdims=True))
            p = jnp.exp2(s - jnp.concatenate([m_new] * ncol, axis=1))
            pv = jnp.dot(p.astype(BF16), v_ref[0, g, pl.ds(k0, tk), :], preferred_element_type=F32)
            acc_ref[g] = jnp.exp2(m_prev - m_new) * acc_ref[g] + pv
            m_ref[g] = m_new
        return carry

    lax.fori_loop(0, n_tiles, attend_tile, 0)

    for g in range(N_KV_HEADS):
        acc = acc_ref[g]
        o = acc[:, :HEAD_DIM] / acc[:, HEAD_DIM:HEAD_DIM + 1]
        for r in range(0, group, 2):
            pair = jnp.concatenate([o[r * tq:(r + 1) * tq], o[(r + 1) * tq:(r + 2) * tq]], axis=1)
            h0 = g * group + r
            o_ref[:, h0 * HEAD_DIM:(h0 + 2) * HEAD_DIM] = pair.astype(BF16)


def _dsa(qh, qih, kw_flat, kw_row_off, kh, vh, kih, *, tq, tk, k_sel, q_off, n_valid_end):
    nb, _, s, _ = qh.shape
    lp = kh.shape[2]
    nq = s // tq
    kern = functools.partial(_dsa_kernel, tq=tq, tk=tk, k_sel=k_sel, q_off=q_off, n_valid_end=n_valid_end,
                             idx_bits=max(1, int(lp).bit_length()))
    per_batch = lambda shp: pl.BlockSpec((1,) + shp, lambda b, i: (b,) + (0,) * len(shp), pipeline_mode=pl.Buffered(1))
    return pl.pallas_call(
        kern,
        out_shape=jax.ShapeDtypeStruct((nb * s, N_HEADS * HEAD_DIM), BF16),
        grid=(nb, nq),
        in_specs=[pl.BlockSpec((1, N_HEADS, tq, HEAD_DIM), lambda b, i: (b, 0, i, 0)),
                  pl.BlockSpec((1, IDX_HEADS, tq, IDX_DIM), lambda b, i: (b, 0, i, 0)),
                  pl.BlockSpec((tq, LANES), lambda b, i: (kw_row_off // tq + b * nq + i, 0)),
                  per_batch((N_KV_HEADS, lp, HEAD_DIM)), per_batch((N_KV_HEADS, lp, LANES)), per_batch((lp, IDX_DIM))],
        out_specs=pl.BlockSpec((tq, N_HEADS * HEAD_DIM), lambda b, i: (b * nq + i, 0)),
        scratch_shapes=[pltpu.VMEM((tq, (lp // LANES | 1) * LANES), F32),
                        pltpu.VMEM((N_KV_HEADS, (N_HEADS // N_KV_HEADS) * tq, LANES), F32),
                        pltpu.VMEM((N_KV_HEADS, (N_HEADS // N_KV_HEADS) * tq, LANES), F32)],
        compiler_params=_cparams(("arbitrary", "arbitrary")),
        name="dsa",
    )(qh, qih, kw_flat, kh, vh, kih)


def _dsa_t_kernel(q_ref, qi_ref, wt_ref, k_ref, vt_ref, ki_ref, o_ref, sc_ref, sch_ref, m_ref, mu_ref, acc_ref,
                  s_ref, s2_ref,
                  *, tk, k_sel, n_valid_end, idx_bits):
    tq = LANES
    i = pl.program_id(1)
    qf0 = i * tq
    klim = jnp.minimum(FRONT + qf0 + tq, n_valid_end)
    n_tiles = (klim + tk - 1) // tk
    kf = float(k_sel)
    group = N_HEADS // N_KV_HEADS
    fold = 32
    fold_h = 64
    assert sch_ref.shape[0] // fold_h <= 256

    wt = wt_ref[0]
    qlane = lax.broadcasted_iota(I32, (1, tq), 1)
    lim = jnp.minimum(FRONT + ((qf0 + qlane) // CHUNK + 1) * CHUNK, n_valid_end)
    qi_all = qi_ref[0].reshape(IDX_HEADS * tq, IDX_DIM)

    def score_tile(t, carry):
        th = tk // 2
        for half in range(2):
            k0 = pl.multiple_of(t * tk, tk) + half * th
            rel = lax.dot_general(ki_ref[0, pl.ds(k0, th), :], qi_all, (((1,), (1,)), ((), ())),
                                  preferred_element_type=F32)
            sc = jnp.zeros((th, tq), F32)
            for h in range(IDX_HEADS):
                sc = sc + wt[h:h + 1, :] * jnp.maximum(rel[:, h * tq:(h + 1) * tq], 0.0)
            j = k0 + lax.broadcasted_iota(I32, (th, tq), 0)
            j = jnp.where(j >= FRONT - N_META, j, n_valid_end)
            sc = jnp.where(j < lim, sc, NEG_INF)
            sc_ref[pl.ds(k0, th), :] = sc
            sch_ref[pl.ds(k0, th), :] = _high_half(sc).astype(HALF)
        return carry

    lax.fori_loop(0, n_tiles, score_tile, 0)

    def count(pred):
        def body(t, acc):
            k0 = pl.multiple_of(t * tk, tk)
            ones = pred(sc_ref[pl.ds(k0, tk), :], k0)
            return acc + jnp.sum(ones.reshape(tk // fold, fold, tq), axis=0)
        acc = lax.fori_loop(0, n_tiles, body, jnp.zeros((fold, tq), F32))
        return jnp.sum(acc, axis=0, keepdims=True)

    def count_high(thr):
        thr_h = _high_half(thr).astype(HALF)
        one, zero = jnp.ones((), HALF), jnp.zeros((), HALF)

        def body(t, acc):
            ones = jnp.where(sch_ref[pl.ds(pl.multiple_of(t * tk, tk), tk), :] >= thr_h, one, zero)
            for c in range(tk // fold_h):
                acc = acc + ones[c * fold_h:(c + 1) * fold_h]
            return acc
        acc = lax.fori_loop(0, n_tiles, body, jnp.zeros((fold_h, tq), HALF))
        return jnp.sum(acc.astype(F32), axis=0, keepdims=True)

    def bisect(counter):
        def step(it, u):
            trial = u | jnp.left_shift(jnp.int32(1), 31 - it)
            return jnp.where(counter(_key_to_float(trial)) >= kf, trial, u)
        return step

    u = lax.fori_loop(0, 16, bisect(count_high), jnp.zeros((1, tq), I32))
    u = lax.fori_loop(16, 32, bisect(lambda thr: count(lambda blk, k0: jnp.where(blk >= thr, 1.0, 0.0))), u)
    key = jnp.maximum(u ^ INT_MIN, SKEY_LOWEST_FINITE)
    thr = _skey_to_float(key)
    n_ge = count(lambda blk, k0: jnp.where(blk >= thr, 1.0, 0.0))
    tied = jnp.max(n_ge) > kf

    def write_bias(sel):
        def body(t, carry):
            k0 = pl.multiple_of(t * tk, tk)
            sc_ref[pl.ds(k0, tk), :] = sel(sc_ref[pl.ds(k0, tk), :], k0)
            return carry
        lax.fori_loop(0, n_tiles, body, 0)

    @pl.when(jnp.logical_not(tied))
    def _():
        write_bias(lambda blk, k0: jnp.where(blk >= thr, 0.0, NEG_INF))

    @pl.when(tied)
    def _():
        need = kf - count(lambda blk, k0: jnp.where(blk > thr, 1.0, 0.0))
        krow = lax.broadcasted_iota(I32, (tk, tq), 0)

        def jbis(it, ans):
            trial = ans + jnp.left_shift(jnp.int32(1), idx_bits - 1 - it)
            cnt = count(lambda blk, k0: jnp.where(blk == thr, jnp.where(k0 + krow < trial, 1.0, 0.0), 0.0))
            return jnp.where(cnt < need, trial, ans)

        ans = lax.fori_loop(0, idx_bits, jbis, jnp.zeros((1, tq), I32))
        write_bias(lambda blk, k0: jnp.where(
            blk > thr, 0.0, jnp.where(blk == thr, jnp.where(k0 + krow <= ans, 0.0, NEG_INF), NEG_INF)))

    m_ref[...] = jnp.full(m_ref.shape, M_INIT, F32)
    acc_ref[...] = jnp.zeros(acc_ref.shape, F32)

    mu_ref[...] = jnp.full(mu_ref.shape, M_INIT, F32)

    def logits(t, buf, g, m_before):
        k0 = pl.multiple_of(t * tk, tk)
        bias = sc_ref[pl.ds(k0, tk), :]
        qg = q_ref[0, g * group:(g + 1) * group].reshape(group * tq, HEAD_DIM)
        s = lax.dot_general(k_ref[0, g, pl.ds(k0, tk), :], qg, (((1,), (1,)), ((), ())),
                            preferred_element_type=F32)
        s = jnp.concatenate([s[:, r * tq:(r + 1) * tq] + bias for r in range(group)], axis=1)
        buf[g] = s
        return jnp.maximum(m_before, jnp.max(s, axis=0, keepdims=True))

    def values(t, buf, g, m_t):
        p = jnp.exp2(buf[g] - m_t)
        pv = jnp.dot(vt_ref[0, g, :, pl.ds(pl.multiple_of(t * tk, tk), tk)], p.astype(BF16),
                     preferred_element_type=F32)
        acc_ref[g] = jnp.exp2(mu_ref[g] - m_t) * acc_ref[g] + pv
        mu_ref[g] = m_t

    def step(t_next, buf_next, t, buf):
        for g in range(N_KV_HEADS):
            m_t = m_ref[g]
            m_ref[g] = logits(t_next, buf_next, g, m_t)
            values(t, buf, g, m_t)

    for g in range(N_KV_HEADS):
        m_ref[g] = logits(0, s_ref, g, m_ref[g])

    def tile_pair(j, carry):
        step(2 * j + 1, s2_ref, 2 * j, s_ref)
        step(jnp.minimum(2 * j + 2, n_tiles - 1), s_ref, 2 * j + 1, s2_ref)
        return carry

    lax.fori_loop(0, n_tiles >> 1, tile_pair, 0)

    @pl.when((n_tiles & 1) == 1)
    def _():
        for g in range(N_KV_HEADS):
            values(n_tiles - 1, s_ref, g, m_ref[g])

    for g in range(N_KV_HEADS):
        acc = acc_ref[g]
        o = acc[:HEAD_DIM, :] / acc[HEAD_DIM:HEAD_DIM + 1, :]
        for r in range(0, group, 2):
            pair = jnp.concatenate([o[:, r * tq:(r + 1) * tq], o[:, (r + 1) * tq:(r + 2) * tq]], axis=0)
            h0 = g * group + r
            o_ref[:, h0 * HEAD_DIM:(h0 + 2) * HEAD_DIM] = pair.T.astype(BF16)


def _dsa_t(qh, qih, wt, kh, vt, kih, *, tk, k_sel, n_valid_end):
    nb, _, s, _ = qh.shape
    lp = kh.shape[2]
    tq = LANES
    nq = s // tq
    group = N_HEADS // N_KV_HEADS
    kern = functools.partial(_dsa_t_kernel, tk=tk, k_sel=k_sel, n_valid_end=n_valid_end,
                             idx_bits=max(1, int(lp).bit_length()))
    per_batch = lambda shp: pl.BlockSpec((1,) + shp, lambda b, i: (b,) + (0,) * len(shp), pipeline_mode=pl.Buffered(1))
    return pl.pallas_call(
        kern,
        out_shape=jax.ShapeDtypeStruct((nb * s, N_HEADS * HEAD_DIM), BF16),
        grid=(nb, nq),
        in_specs=[pl.BlockSpec((1, N_HEADS, tq, HEAD_DIM), lambda b, i: (b, 0, i, 0)),
                  pl.BlockSpec((1, IDX_HEADS, tq, IDX_DIM), lambda b, i: (b, 0, i, 0)),
                  pl.BlockSpec((1, IDX_HEADS, tq), lambda b, i: (b, 0, i)),
                  per_batch((N_KV_HEADS, lp, HEAD_DIM)), per_batch((N_KV_HEADS, VT_ROWS, lp)), per_batch((lp, IDX_DIM))],
        out_specs=pl.BlockSpec((tq, N_HEADS * HEAD_DIM), lambda b, i: (b * nq + i, 0)),
        scratch_shapes=[pltpu.VMEM((lp, tq), F32), pltpu.VMEM((lp, tq), HALF),
                        pltpu.VMEM((N_KV_HEADS, 1, group * tq), F32),
                        pltpu.VMEM((N_KV_HEADS, 1, group * tq), F32),
                        pltpu.VMEM((N_KV_HEADS, VT_ROWS, group * tq), F32),
                        pltpu.VMEM((N_KV_HEADS, tk, group * tq), F32),
                        pltpu.VMEM((N_KV_HEADS, tk, group * tq), F32)],
        compiler_params=_cparams(("arbitrary", "arbitrary")),
        name="dsa_t",
    )(qh, qih, wt, kh, vt, kih)


def _pad_keys(meta_rows, mid_rows, lp):
    nb = mid_rows[0].shape[0]
    tail = mid_rows[0].shape[2:]
    meta_b = jnp.broadcast_to(meta_rows[None], (nb,) + meta_rows.shape)
    used = FRONT + sum(m.shape[1] for m in mid_rows)
    parts = [jnp.zeros((nb, FRONT - N_META) + tail, meta_rows.dtype), meta_b] + list(mid_rows)
    if lp > used:
        parts.append(jnp.zeros((nb, lp - used) + tail, meta_rows.dtype))
    return jnp.concatenate(parts, axis=1)


def _attention_inputs(k_meta, v_meta, ki_meta, k_mid, v_mid, ki_mid, lp, keys_on_lanes):
    nb = k_mid[0].shape[0]
    kp = _pad_keys(k_meta, k_mid, lp).astype(BF16).reshape(nb, lp, N_KV_HEADS, HEAD_DIM).transpose(0, 2, 1, 3)
    vp = _pad_keys(v_meta, v_mid, lp).astype(BF16).reshape(nb, lp, N_KV_HEADS, HEAD_DIM)
    if keys_on_lanes:
        vp = vp.transpose(0, 2, 3, 1)
        ones = jnp.ones((nb, N_KV_HEADS, 1, lp), BF16)
        vp = jnp.concatenate([vp, ones, jnp.zeros((nb, N_KV_HEADS, VT_ROWS - HEAD_DIM - 1, lp), BF16)], axis=2)
    else:
        vp = vp.transpose(0, 2, 1, 3)
        ones = jnp.ones(vp.shape[:3] + (1,), BF16)
        vp = jnp.concatenate([vp, ones, jnp.zeros(vp.shape[:3] + (LANES - HEAD_DIM - 1,), BF16)], axis=3)
    kip = _pad_keys(ki_meta, ki_mid, lp).astype(BF16)
    return kp, vp, kip


def _merge_kernel(xp_ref, xs_ref, cp_ref, cs_ref, ap_ref, as_ref, gate_ref, wao_ref, wout_ref, gffn_ref, wr_ref, br_ref,
                  x1_ref, hn_ref, ids_ref, gts_ref, *, n_first):
    first = pl.program_id(0) < n_first
    x = jnp.where(first, xp_ref[...], xs_ref[...])
    conv = jnp.where(first, cp_ref[...], cs_ref[...]).astype(F32)
    attn = jnp.where(first, ap_ref[...], as_ref[...])
    ao = jnp.dot(attn, wao_ref[...], preferred_element_type=F32)
    gates = gate_ref[...].astype(F32)
    mix = gates[:, :D_MODEL] * conv + gates[:, D_MODEL:] * ao
    x1 = x + jnp.dot(mix.astype(BF16), wout_ref[...], preferred_element_type=F32)
    x1_ref[...] = x1
    hn = x1 * lax.rsqrt(jnp.mean(x1 * x1, axis=-1, keepdims=True) + EPS) * gffn_ref[...]
    tr = hn.shape[0]
    for c in range(SUB):
        hn_ref[pl.ds(c, tr, stride=SUB), :] = hn[:, c * LANES:(c + 1) * LANES]
    logits = jnp.dot(hn, wr_ref[...], preferred_element_type=F32, precision=lax.Precision.HIGHEST) + br_ref[...]
    lane = lax.broadcasted_iota(I32, logits.shape, 1)
    lg = jnp.where(lane < N_EXPERTS, logits, NEG_INF)
    vals, ids = [], []
    for _ in range(TOP_K_EXPERTS):
        m = jnp.max(lg, axis=1, keepdims=True)
        idx = jnp.min(jnp.where(lg == m, lane, LANES), axis=1, keepdims=True)
        vals.append(m)
        ids.append(idx)
        lg = jnp.where(lane == idx, NEG_INF, lg)
    es = [jnp.exp(v - vals[0]) for v in vals]
    den = es[0]
    for e in es[1:]:
        den = den + e
    ids_out = jnp.zeros(logits.shape, I32)
    gts_out = jnp.zeros(logits.shape, F32)
    for j in range(TOP_K_EXPERTS):
        ids_out = jnp.where(lane == j, ids[j], ids_out)
        gts_out = jnp.where(lane == j, es[j] / den, gts_out)
    ids_ref[...] = ids_out
    gts_ref[...] = gts_out


def _merge(xp, xs, conv_p, conv_s, attn_p, attn_s, gates, wao, wout, gffn, wr, br, tr):
    n_first, n_second = xp.shape[0] // tr, xs.shape[0] // tr
    rows = xp.shape[0] + xs.shape[0]
    first = lambda w: pl.BlockSpec((tr, w), lambda i: (jnp.minimum(i, n_first - 1), 0))
    second = lambda w: pl.BlockSpec((tr, w), lambda i: (jnp.maximum(i - n_first, 0), 0))
    row = lambda w: pl.BlockSpec((tr, w), lambda i: (i, 0))
    return pl.pallas_call(
        functools.partial(_merge_kernel, n_first=n_first),
        out_shape=[jax.ShapeDtypeStruct((rows, D_MODEL), F32), jax.ShapeDtypeStruct((rows * SUB, LANES), F32),
                   jax.ShapeDtypeStruct((rows, LANES), I32), jax.ShapeDtypeStruct((rows, LANES), F32)],
        grid=(n_first + n_second,),
        in_specs=[first(D_MODEL), second(D_MODEL), first(D_MODEL), second(D_MODEL), first(D_MODEL), second(D_MODEL),
                  row(2 * D_MODEL), _resident(wao), _resident(wout), _resident(gffn), _resident(wr), _resident(br)],
        out_specs=[row(D_MODEL), pl.BlockSpec((tr * SUB, LANES), lambda i: (i, 0)), row(LANES), row(LANES)],
        compiler_params=_cparams(("arbitrary",)),
        name="merge",
    )(xp, xs, conv_p, conv_s, attn_p, attn_s, gates, wao, wout, gffn, wr, br)


KEY_ALIGN = 1024


def _key_window(tm):
    return -(-(tm + KEY_ALIGN - 1) // KEY_ALIGN) * KEY_ALIGN


def _for_rows(n, fn):
    def group(gi, carry):
        for u in range(SUB):
            fn(gi * SUB + u)
        return carry

    def single(r, carry):
        fn(r)
        return carry

    shift = SUB.bit_length() - 1
    lax.fori_loop(0, n >> shift, group, 0)
    lax.fori_loop((n >> shift) << shift, n, single, 0)


def _moe_kernel(te_ref, base_ref, nval_ref, nreal_ref, keys_hbm, h_hbm, w1_ref, b1_ref, w3_ref, b3_ref, w2_ref, b2_ref,
                y_hbm, key_sm, xbuf, xb, ybuf, wb, ksem, gsem, ssem, *, tm, n_tok, pair_bits):
    i = pl.program_id(0)
    nreal = nreal_ref[0]
    window = key_sm.shape[0] // 3
    slot_shift = TOP_K_EXPERTS.bit_length() - 1

    def keys_copy(tile):
        s = lax.rem(tile, 3)
        start = pl.multiple_of(base_ref[tile] & -KEY_ALIGN, KEY_ALIGN)
        return pltpu.make_async_copy(keys_hbm.at[pl.ds(start, window)],
                                     key_sm.at[pl.ds(pl.multiple_of(s * window, KEY_ALIGN), window)], ksem.at[s])

    def pair_reader(tile):
        first = lax.rem(tile, 3) * window + (base_ref[tile] & (KEY_ALIGN - 1))
        return lambda r: key_sm[first + r] & ((1 << pair_bits) - 1)

    def gather_row(s, r, token):
        return pltpu.make_async_copy(h_hbm.at[pl.ds(token * SUB, SUB)], xbuf.at[s, pl.ds(r * SUB, SUB)], gsem.at[s])

    def scatter_row(r, out_token):
        return pltpu.make_async_copy(ybuf.at[pl.ds(r * SUB, SUB)], y_hbm.at[pl.ds(out_token * SUB, SUB)], ssem.at[0])

    def start_gather(tile):
        pair_of, s = pair_reader(tile), tile & 1
        _for_rows(tm, lambda r: gather_row(s, r, pair_of(r) >> slot_shift).start())

    @pl.when(jnp.logical_and(i == 0, nreal > 0))
    def _():
        keys_copy(0).start()

        @pl.when(nreal > 1)
        def _():
            keys_copy(1).start()

        keys_copy(0).wait()
        start_gather(0)

    @pl.when(i + 1 < nreal)
    def _():
        keys_copy(i + 1).wait()
        start_gather(i + 1)

    @pl.when(i + 2 < nreal)
    def _():
        keys_copy(i + 2).start()

    @pl.when(i < nreal)
    def _():
        s = i & 1
        _for_rows(tm, lambda r: gather_row(s, r, 0).wait())
        for c in range(SUB):
            xb[:, c * LANES:(c + 1) * LANES] = xbuf[s, pl.ds(c, tm, stride=SUB), :].astype(BF16)
        @pl.when(jnp.logical_or(i == 0, te_ref[i] != te_ref[jnp.maximum(i - 1, 0)]))
        def _():
            for j, w_ref in enumerate((w1_ref, w3_ref, w2_ref)):
                wb[j] = w_ref[0].astype(BF16)

        x = xb[...]
        g = jnp.minimum(jnp.dot(x, wb[0], preferred_element_type=F32) + b1_ref[0], SWIGLU_LIMIT)
        u = jnp.clip(jnp.dot(x, wb[1], preferred_element_type=F32) + b3_ref[0], -SWIGLU_LIMIT, SWIGLU_LIMIT)
        act = g * jax.nn.sigmoid(SWIGLU_ALPHA * g) * (u + 1.0)
        y = jnp.dot(act.astype(BF16), wb[2], preferred_element_type=F32) + b2_ref[0]

        @pl.when(i > 0)
        def _():
            _for_rows(nval_ref[i - 1], lambda r: scatter_row(r, 0).wait())

        for c in range(SUB):
            ybuf[pl.ds(c, tm, stride=SUB), :] = y[:, c * LANES:(c + 1) * LANES]

        pair_of = pair_reader(i)

        def put(r):
            pair = pair_of(r)
            scatter_row(r, (pair & (TOP_K_EXPERTS - 1)) * n_tok + (pair >> slot_shift)).start()

        _for_rows(nval_ref[i], put)

        @pl.when(i == nreal - 1)
        def _():
            _for_rows(nval_ref[i], lambda r: scatter_row(r, 0).wait())


def _moe(hn, keys, tile_expert, base, nval, n_real, w1, b1, w3, b3, w2, b2, n_tok, pair_bits, tm):
    nt = tile_expert.shape[0]
    by_expert = lambda i, te, *_: (te[i], 0, 0)
    wspec = pl.BlockSpec((1, D_MODEL, D_FF), by_expert)
    w2spec = pl.BlockSpec((1, D_FF, D_MODEL), by_expert)
    bspec = pl.BlockSpec((1, 1, D_FF), by_expert)
    any_spec = pl.BlockSpec(memory_space=pl.ANY)
    grid_spec = pltpu.PrefetchScalarGridSpec(
        num_scalar_prefetch=4,
        grid=(nt,),
        in_specs=[any_spec, any_spec, wspec, bspec, wspec, bspec, w2spec, bspec],
        out_specs=any_spec,
        scratch_shapes=[pltpu.SMEM((3 * _key_window(tm),), I32),
                        pltpu.VMEM((2, tm * SUB, LANES), F32), pltpu.VMEM((tm, D_MODEL), BF16),
                        pltpu.VMEM((tm * SUB, LANES), F32), pltpu.VMEM((3, D_MODEL, D_FF), BF16),
                        pltpu.SemaphoreType.DMA((3,)), pltpu.SemaphoreType.DMA((2,)), pltpu.SemaphoreType.DMA((1,))],
    )
    return pl.pallas_call(
        functools.partial(_moe_kernel, tm=tm, n_tok=n_tok, pair_bits=pair_bits),
        out_shape=jax.ShapeDtypeStruct((TOP_K_EXPERTS * n_tok * SUB, LANES), F32),
        grid_spec=grid_spec,
        compiler_params=_cparams(("arbitrary",)),
        name="moe",
    )(tile_expert, base, nval, n_real, keys, hn, w1, b1, w3, b3, w2, b2)


def _route(ids, n, tm):
    n_pairs = n * TOP_K_EXPERTS
    pair_bits = max(1, (n_pairs - 1).bit_length())
    assert pair_bits + (N_EXPERTS - 1).bit_length() <= 31 and tm % LANES == 0
    assert TOP_K_EXPERTS & (TOP_K_EXPERTS - 1) == 0
    pair = jnp.arange(n_pairs, dtype=I32)
    keys = jnp.sort(jnp.left_shift(ids.reshape(-1), pair_bits) | pair)
    starts = jnp.searchsorted(keys, jnp.left_shift(jnp.arange(N_EXPERTS + 1, dtype=I32), pair_bits),
                              method="compare_all").astype(I32)
    counts = starts[1:] - starts[:-1]
    tiles = (counts + tm - 1) // tm
    tile_end = jnp.cumsum(tiles)
    nt = -(-(n_pairs + N_EXPERTS * (tm - 1)) // tm)
    ti = jnp.arange(nt, dtype=I32)
    te = jnp.minimum(jnp.searchsorted(tile_end, ti, side="right", method="compare_all"), N_EXPERTS - 1).astype(I32)
    within = ti - (tile_end - tiles)[te]
    base = jnp.clip(starts[:-1][te] + within * tm, 0, n_pairs - 1).astype(I32)
    nval = jnp.clip(counts[te] - within * tm, 0, tm).astype(I32)
    padded = -(-n_pairs // KEY_ALIGN) * KEY_ALIGN + _key_window(tm)
    keys = jnp.pad(keys, (0, padded - n_pairs))
    return keys, te, base, nval, tile_end[-1:].astype(I32), pair_bits


def _final_kernel(x1_ref, y0_ref, y1_ref, y2_ref, y3_ref, gts_ref, g_ref, out_ref):
    gts = gts_ref[...]
    tr = gts.shape[0]
    cols = []
    for c in range(SUB):
        acc = x1_ref[:, c * LANES:(c + 1) * LANES]
        for j, y_ref in enumerate((y0_ref, y1_ref, y2_ref, y3_ref)):
            acc = acc + gts[:, j:j + 1] * y_ref[pl.ds(c, tr, stride=SUB), :]
        cols.append(acc)
    ss = cols[0] * cols[0]
    for acc in cols[1:]:
        ss = ss + acc * acc
    scale = lax.rsqrt(jnp.sum(ss, axis=-1, keepdims=True) / D_MODEL + EPS)
    for c, acc in enumerate(cols):
        out_ref[:, c * LANES:(c + 1) * LANES] = acc * scale * g_ref[:, c * LANES:(c + 1) * LANES]


def _final(x1, yslots, gts, g, tr):
    n = x1.shape[0]
    nblk = n // tr
    row = lambda w: pl.BlockSpec((tr, w), lambda i: (i, 0))
    slot = lambda j: pl.BlockSpec((tr * SUB, LANES), lambda i: (j * nblk + i, 0))
    return pl.pallas_call(
        _final_kernel,
        out_shape=jax.ShapeDtypeStruct((n, D_MODEL), F32),
        grid=(nblk,),
        in_specs=[row(D_MODEL)] + [slot(j) for j in range(TOP_K_EXPERTS)] + [row(LANES), _resident(g)],
        out_specs=row(D_MODEL),
        compiler_params=_cparams(("arbitrary",)),
        name="final",
    )(x1, yslots, yslots, yslots, yslots, gts, g)


def _tile(n, pref):
    t = min(n, pref)
    while n % t:
        t //= 2
    return t


def kernel(x_prompt, x_sample, cache_k, cache_v, cache_idx_k, state_conv, meta_tokens, g_norm_mix, w_in, conv_w,
           conv_b, conv_ln_g, conv_ln_b, w_conv_out, w_attn_out, w_out, g_norm_ffn, w_router, b_router,
           w1, b1, w3, b3, w2, b2, g_norm_final):
    nb, s_p, _ = x_prompt.shape
    db, s_n, _ = x_sample.shape
    past = cache_k.shape[1]
    rows_p, rows_s = nb * s_p, db * s_n
    n = rows_p + rows_s
    tr = _tile(rows_s, ROW_TILE)
    assert rows_p % tr == 0 and s_p % tr == 0 and tr % s_n == 0 and s_p % Q_TILE == 0

    cuts = [0, 2 * CONV_CH, N_HEADS * HEAD_DIM, N_KV_HEADS * HEAD_DIM, N_KV_HEADS * HEAD_DIM, IDX_HEADS * IDX_DIM,
            IDX_DIM + IDX_HEADS, 2 * D_MODEL]
    offs = [sum(cuts[:i + 1]) for i in range(len(cuts))]
    seg = [w_in[:, offs[i]:offs[i + 1]].astype(BF16) for i in range(len(cuts) - 1)]
    seg[5] = jnp.pad(seg[5], ((0, 0), (0, LANES - seg[5].shape[1])))
    g_mix = g_norm_mix.reshape(1, D_MODEL)

    pos_p = N_META + jnp.arange(s_p, dtype=I32)
    pos_s = jnp.tile(N_META + past + jnp.arange(s_n, dtype=I32), tr // s_n)
    tabs = _rope_tables(jnp.concatenate([pos_p, pos_s]))
    blocks_per_seq = s_p // tr
    n_first = rows_p // tr
    tab_map = lambda i: jnp.where(i < n_first, lax.rem(i, blocks_per_seq), blocks_per_seq)

    xp, xs = x_prompt.reshape(rows_p, D_MODEL), x_sample.reshape(rows_s, D_MODEL)
    a, q, k, v, qi, kw, gates = _project(xp, xs, tabs, tab_map, g_mix, seg, tr)

    meta = meta_tokens.astype(F32)
    tabs_m = _rope_tables(jnp.arange(N_META, dtype=I32))
    a_m, _, k_m, v_m, _, kw_m, _ = _project(meta, meta, tabs_m, lambda i: 0, g_mix, seg, N_META)
    a_m, k_m, v_m, ki_m = a_m[:N_META], k_m[:N_META], v_m[:N_META], kw_m[:N_META, :IDX_DIM]

    cw = jnp.pad(conv_w, ((0, HALO - CONV_WIDTH), (0, 0)))
    cb, lg, lb = conv_b.reshape(1, -1), conv_ln_g.reshape(1, -1), conv_ln_b.reshape(1, -1)
    wco = w_conv_out.astype(BF16)
    ctx_p = jnp.concatenate([jnp.zeros((HALO - N_META, CONV_CH), F32), a_m], axis=0)[None]
    ctx_s = jnp.pad(state_conv, ((0, 0), (HALO - (CONV_WIDTH - 1), 0), (0, 0)))
    conv_p = _conv_branch(a, 0, nb, s_p, ctx_p, cw, cb, lg, lb, wco, _tile(s_p, 256))
    conv_s = _conv_branch(a, rows_p, db, s_n, ctx_s, cw, cb, lg, lb, wco, s_n)

    k_p3, v_p3 = k[:rows_p].reshape(nb, s_p, -1), v[:rows_p].reshape(nb, s_p, -1)
    ki_p3 = kw[:rows_p, :IDX_DIM].reshape(nb, s_p, IDX_DIM)
    k_s3, v_s3 = k[rows_p:].reshape(db, s_n, -1), v[rows_p:].reshape(db, s_n, -1)
    ki_s3 = kw[rows_p:, :IDX_DIM].reshape(db, s_n, IDX_DIM)

    def heads(t, rows0, b_, s_, nh):
        return t[rows0:rows0 + b_ * s_].reshape(b_, s_, nh, -1).transpose(0, 2, 1, 3)

    tk_p = _tile(s_p, K_TILE_T)
    lp_p = -(-(FRONT + s_p) // tk_p) * tk_p
    kp, vtp, kip = _attention_inputs(k_m, v_m, ki_m, [k_p3], [v_p3], [ki_p3], lp_p, True)
    wt_p = kw[:rows_p, IDX_DIM:IDX_DIM + IDX_HEADS].reshape(nb, s_p, IDX_HEADS).transpose(0, 2, 1)
    attn_p = _dsa_t(heads(q, 0, nb, s_p, N_HEADS), heads(qi, 0, nb, s_p, IDX_HEADS), wt_p, kp, vtp, kip,
                    tk=tk_p, k_sel=min(TOPK_MAX, s_p // 4), n_valid_end=FRONT + s_p)

    tk_s = K_TILE
    lp_s = -(-(FRONT + past + s_n) // tk_s) * tk_s
    ck = cache_k.reshape(db, past, -1)
    cv = cache_v.reshape(db, past, -1)
    ks, vs, kis = _attention_inputs(k_m, v_m, ki_m, [ck, k_s3], [cv, v_s3], [cache_idx_k, ki_s3], lp_s, False)
    attn_s = _dsa(heads(q, rows_p, db, s_n, N_HEADS), heads(qi, rows_p, db, s_n, IDX_HEADS), kw, rows_p, ks, vs, kis,
                  tq=s_n, tk=tk_s, k_sel=min(TOPK_MAX, (past + s_n) // 4), q_off=past,
                  n_valid_end=FRONT + past + s_n)

    wr = jnp.pad(w_router, ((0, 0), (0, LANES - N_EXPERTS)))
    br = jnp.pad(b_router, (0, LANES - N_EXPERTS)).reshape(1, LANES)
    x1, hn, ids, gts = _merge(xp, xs, conv_p, conv_s, attn_p, attn_s, gates, w_attn_out.astype(BF16),
                              w_out.astype(BF16), g_norm_ffn.reshape(1, -1), wr, br, tr)

    tm = MOE_TILE
    keys, tile_expert, base, nval, n_real, pair_bits = _route(ids[:, :TOP_K_EXPERTS], n, tm)
    yslots = _moe(hn, keys, tile_expert, base, nval, n_real, w1, b1.reshape(N_EXPERTS, 1, D_FF),
                  w3, b3.reshape(N_EXPERTS, 1, D_FF), w2, b2.reshape(N_EXPERTS, 1, D_MODEL), n, pair_bits, tm)
    out = _final(x1, yslots, gts, g_norm_final.reshape(1, -1), tr)

    y_prompt = out[:rows_p].reshape(nb, s_p, D_MODEL)
    y_sample = out[rows_p:].reshape(db, s_n, D_MODEL)
    lead = lambda m: jnp.broadcast_to(m[None], (nb,) + m.shape)
    k_prompt = jnp.concatenate([lead(k_m), k_p3], axis=1).reshape(nb, N_META + s_p, N_KV_HEADS, HEAD_DIM)
    v_prompt = jnp.concatenate([lead(v_m), v_p3], axis=1).reshape(nb, N_META + s_p, N_KV_HEADS, HEAD_DIM)
    idxk_prompt = jnp.concatenate([lead(ki_m), ki_p3], axis=1)
    a_p3 = a[:rows_p].reshape(nb, s_p, CONV_CH)
    a_s3 = a[rows_p:].reshape(db, s_n, CONV_CH)
    keep = CONV_WIDTH - 1
    conv_state_p = jnp.concatenate([lead(a_m), a_p3], axis=1)[:, N_META + s_p - keep:]
    conv_state_s = jnp.concatenate([state_conv, a_s3], axis=1)[:, s_n:]
    return (y_prompt, y_sample, k_prompt, v_prompt, idxk_prompt, conv_state_p,
            k_s3.reshape(db, s_n, N_KV_HEADS, HEAD_DIM), v_s3.reshape(db, s_n, N_KV_HEADS, HEAD_DIM), ki_s3,
            conv_state_s)
```

```python
import functools

import jax
import jax.numpy as jnp
from jax import lax
from jax.experimental import pallas as pl
from jax.experimental.pallas import tpu as pltpu

F32, BF16, I32 = jnp.float32, jnp.bfloat16, jnp.int32
HALF = jnp.bfloat16

D_MODEL = 1024
CHUNK = 64
N_META = 16
CONV_CH = D_MODEL
CONV_WIDTH = 31
N_HEADS = 16
N_KV_HEADS = 4
HEAD_DIM = 64
ROT_DIM = HEAD_DIM // 4
ROPE_THETA = 500000.0
IDX_HEADS = 8
IDX_DIM = 64
TOPK_MAX = 256
N_EXPERTS = 32
TOP_K_EXPERTS = 4
D_FF = D_MODEL
SWIGLU_LIMIT = 7.0
SWIGLU_ALPHA = 1.702
EPS = 1e-5
Q_SCALE = HEAD_DIM ** -0.5 * 1.4426950408889634

LANES = 128
SUB = 8
VT_ROWS = HEAD_DIM + 2 * SUB
FRONT = 128
HALO = 32
VMEM_LIMIT = 56 * 1024 * 1024
NEG_INF = float("-inf")
INT_MIN = -2 ** 31
M_INIT = -2.0 ** 100

ROW_TILE = 512
Q_TILE = 128
K_TILE = 512
K_TILE_T = 1024
MOE_TILE = 512


def _cparams(sem):
    return pltpu.CompilerParams(dimension_semantics=sem, vmem_limit_bytes=VMEM_LIMIT)


def _resident(arr):
    nd = arr.ndim
    return pl.BlockSpec(arr.shape, lambda *_: (0,) * nd, pipeline_mode=pl.Buffered(1))


def _rope_cols(x, c, s1, s2):
    outs = []
    for j in range(x.shape[1] // LANES):
        xj = x[:, j * LANES:(j + 1) * LANES]
        outs.append(xj * c + pltpu.roll(xj, LANES - ROT_DIM // 2, 1) * s1 + pltpu.roll(xj, ROT_DIM // 2, 1) * s2)
    return outs


def _proj_kernel(xp_ref, xs_ref, g_ref, wa_ref, wq_ref, wk_ref, wv_ref, wqi_ref, wkw_ref, wgl_ref,
                 c_ref, s1_ref, s2_ref,
                 a_ref, q_ref, k_ref, v_ref, qi_ref, kw_ref, gate_ref, *, n_first):
    x = jnp.where(pl.program_id(0) < n_first, xp_ref[...], xs_ref[...])
    h = x * lax.rsqrt(jnp.mean(x * x, axis=-1, keepdims=True) + EPS) * g_ref[...]
    hb = h.astype(BF16)
    c, s1, s2 = c_ref[...], s1_ref[...], s2_ref[...]

    def dot(w_ref):
        return jnp.dot(hb, w_ref[...], preferred_element_type=F32)

    glu = dot(wa_ref)
    a_ref[...] = glu[:, :CONV_CH] * jax.nn.sigmoid(glu[:, CONV_CH:])
    for j, col in enumerate(_rope_cols(dot(wq_ref), c, s1, s2)):
        q_ref[:, j * LANES:(j + 1) * LANES] = (col * Q_SCALE).astype(BF16)
    for j, col in enumerate(_rope_cols(dot(wk_ref), c, s1, s2)):
        k_ref[:, j * LANES:(j + 1) * LANES] = col
    v_ref[...] = dot(wv_ref)
    for j, col in enumerate(_rope_cols(dot(wqi_ref), c, s1, s2)):
        qi_ref[:, j * LANES:(j + 1) * LANES] = (col * IDX_DIM ** -0.5).astype(BF16)
    pkw = dot(wkw_ref)
    lane = lax.broadcasted_iota(I32, pkw.shape, 1)
    kw_ref[...] = jnp.where(lane < IDX_DIM, _rope_cols(pkw, c, s1, s2)[0], pkw * IDX_HEADS ** -0.5)
    gate_ref[...] = jax.nn.sigmoid(dot(wgl_ref)).astype(BF16)


def _rope_tables(pos):
    half = ROT_DIM // 2
    inv = ROPE_THETA ** (-jnp.arange(half, dtype=F32) / half)
    ang = pos.astype(F32)[:, None] * inv[None, :]
    cos, sin = jnp.cos(ang), jnp.sin(ang)
    r = pos.shape[0]
    z = lambda n: jnp.zeros((r, n), F32)
    c = jnp.concatenate([cos, cos, jnp.ones((r, HEAD_DIM - ROT_DIM), F32)], axis=1)
    s1 = jnp.concatenate([-sin, z(HEAD_DIM - half)], axis=1)
    s2 = jnp.concatenate([z(half), sin, z(HEAD_DIM - ROT_DIM)], axis=1)
    two = lambda t: jnp.concatenate([t, t], axis=1)
    return two(c), two(s1), two(s2)


def _project(xp, xs, tabs, tab_map, g, ws, tr):
    n_first, n_second = xp.shape[0] // tr, xs.shape[0] // tr
    rows = xp.shape[0] + xs.shape[0]
    row = lambda w: pl.BlockSpec((tr, w), lambda i: (i, 0))
    tab = pl.BlockSpec((tr, LANES), lambda i: (tab_map(i), 0))
    in_specs = [pl.BlockSpec((tr, D_MODEL), lambda i: (jnp.minimum(i, n_first - 1), 0)),
                pl.BlockSpec((tr, D_MODEL), lambda i: (jnp.maximum(i - n_first, 0), 0)),
                _resident(g)] + [_resident(w) for w in ws] + [tab, tab, tab]
    widths = [(CONV_CH, F32), (N_HEADS * HEAD_DIM, BF16), (N_KV_HEADS * HEAD_DIM, F32), (N_KV_HEADS * HEAD_DIM, F32),
              (IDX_HEADS * IDX_DIM, BF16), (LANES, F32), (2 * D_MODEL, BF16)]
    return pl.pallas_call(
        functools.partial(_proj_kernel, n_first=n_first),
        out_shape=[jax.ShapeDtypeStruct((rows, w), dt) for w, dt in widths],
        grid=(n_first + n_second,),
        in_specs=in_specs,
        out_specs=[row(w) for w, _ in widths],
        compiler_params=_cparams(("arbitrary",)),
        name="proj",
    )(xp, xs, g, *ws, *tabs)


def _conv_kernel(cur_ref, halo_ref, lc_ref, cw_ref, cb_ref, lg_ref, lb_ref, wo_ref, out_ref, e_ref, c_ref, *, tr, rc):
    halo = jnp.where(pl.program_id(1) == 0, lc_ref[0], halo_ref[...])
    e_ref[0, 0:HALO, :] = halo
    e_ref[0, HALO:HALO + tr, :] = cur_ref[...]
    n = tr + HALO - 8
    for b in range(1, 8):
        e_ref[b, 0:n, :] = e_ref[0, b:b + n, :]
    first = HALO - (CONV_WIDTH - 1)

    def chunk(ci, carry):
        r0 = pl.multiple_of(ci * rc, rc)
        acc = jnp.zeros((rc, CONV_CH), F32) + cb_ref[...]
        for w in range(CONV_WIDTH):
            s = first + w
            acc = acc + e_ref[s % 8, pl.ds(r0 + 8 * (s // 8), rc), :] * cw_ref[pl.ds(w, 1), :]
        c_ref[pl.ds(r0, rc), :] = acc
        return carry

    lax.fori_loop(0, tr // rc, chunk, 0)
    c = c_ref[...]
    mu = jnp.mean(c, axis=-1, keepdims=True)
    xc = c - mu
    y = xc * lax.rsqrt(jnp.mean(xc * xc, axis=-1, keepdims=True) + EPS) * lg_ref[...] + lb_ref[...]
    act = (y * jax.nn.sigmoid(y)).astype(BF16)
    out_ref[...] = jnp.dot(act, wo_ref[...], preferred_element_type=F32).astype(BF16)


def _conv_branch(a_flat, row_off, nb, t, left_ctx, cw, cb, lg, lb, wo, tr):
    nt = t // tr
    rc = min(tr, 32)
    per_batch_ctx = left_ctx.shape[0] > 1
    cur = pl.BlockSpec((tr, CONV_CH), lambda b, i: (row_off // tr + b * nt + i, 0))
    halo = pl.BlockSpec((HALO, CONV_CH),
                        lambda b, i: (jnp.maximum((row_off + b * t + i * tr) // HALO - 1, 0), 0))
    lctx = pl.BlockSpec((1, HALO, CONV_CH), lambda b, i: (b if per_batch_ctx else 0, 0, 0))
    return pl.pallas_call(
        functools.partial(_conv_kernel, tr=tr, rc=rc),
        out_shape=jax.ShapeDtypeStruct((nb * t, D_MODEL), BF16),
        grid=(nb, nt),
        in_specs=[cur, halo, lctx, _resident(cw), _resident(cb), _resident(lg), _resident(lb), _resident(wo)],
        out_specs=pl.BlockSpec((tr, D_MODEL), lambda b, i: (b * nt + i, 0)),
        scratch_shapes=[pltpu.VMEM((8, tr + HALO, CONV_CH), F32), pltpu.VMEM((tr, CONV_CH), F32)],
        compiler_params=_cparams(("arbitrary", "arbitrary")),
        name="conv",
    )(a_flat, a_flat, left_ctx, cw, cb, lg, lb, wo)


def _float_to_skey(f):
    bits = lax.bitcast_convert_type(f, I32)
    return bits ^ ((bits >> 31) & 0x7FFFFFFF)


def _skey_to_float(key):
    return lax.bitcast_convert_type(key ^ ((key >> 31) & 0x7FFFFFFF), F32)


SKEY_LOWEST_FINITE = INT_MIN + 0x00800000


def _high_half(f):
    return lax.bitcast_convert_type(lax.bitcast_convert_type(f, I32) & -65536, F32)


def _key_to_float(u):
    key = u ^ INT_MIN
    bits = key ^ ((key >> 31) & 0x7FFFFFFF)
    return lax.bitcast_convert_type(bits, F32)


def _dsa_kernel(q_ref, qi_ref, kw_ref, k_ref, v_ref, ki_ref, o_ref, sc_ref, m_ref, acc_ref,
                *, tq, tk, k_sel, q_off, n_valid_end, idx_bits):
    i = pl.program_id(1)
    qf0 = q_off + i * tq
    klim = jnp.minimum(FRONT + ((qf0 + tq - 1) // CHUNK + 1) * CHUNK, n_valid_end)
    n_tiles = (klim + tk - 1) // tk
    ncol = tk // LANES
    kf = float(k_sel)
    group = N_HEADS // N_KV_HEADS

    kwv = kw_ref[...]
    qrow = lax.broadcasted_iota(I32, (tq, 1), 0)
    lim_row = jnp.minimum(FRONT + ((qf0 + qrow) // CHUNK + 1) * CHUNK, n_valid_end)

    def score_tile(t, carry):
        k0 = pl.multiple_of(t * tk, tk)
        kt = ki_ref[0, pl.ds(k0, tk), :]
        sc = jnp.zeros((tq, tk), F32)
        for h in range(IDX_HEADS):
            rel = lax.dot_general(qi_ref[0, h], kt, (((1,), (1,)), ((), ())), preferred_element_type=F32)
            sc = sc + kwv[:, IDX_DIM + h:IDX_DIM + h + 1] * jnp.maximum(rel, 0.0)
        j = k0 + lax.broadcasted_iota(I32, (tq, tk), 1)
        j = jnp.where(j >= FRONT - N_META, j, n_valid_end)
        sc_ref[:, pl.ds(k0, tk)] = jnp.where(j < lim_row, sc, NEG_INF)
        return carry

    lax.fori_loop(0, n_tiles, score_tile, 0)

    lane = lax.broadcasted_iota(I32, (tq, LANES), 1)

    def count(pred):
        def body(t, acc):
            k0 = pl.multiple_of(t * tk, tk)
            for c in range(ncol):
                blk = sc_ref[:, pl.ds(k0 + c * LANES, LANES)]
                acc = acc + pred(blk, k0 + c * LANES)
            return acc
        acc = lax.fori_loop(0, n_tiles, body, jnp.zeros((tq, LANES), F32))
        return jnp.broadcast_to(jnp.sum(acc, axis=1, keepdims=True), (tq, LANES))

    def bisect(it, u):
        trial = u | jnp.left_shift(jnp.int32(1), 31 - it)
        thr = _key_to_float(trial)
        cnt = count(lambda blk, j0: jnp.where(blk >= thr, 1.0, 0.0))
        return jnp.where(cnt >= kf, trial, u)

    u = lax.fori_loop(0, 32, bisect, jnp.zeros((tq, LANES), I32))
    key = jnp.maximum(u ^ INT_MIN, INT_MIN + 0x00800000)
    thr = _key_to_float(key ^ INT_MIN)
    n_ge = count(lambda blk, j0: jnp.where(blk >= thr, 1.0, 0.0))
    tied = jnp.max(n_ge) > kf

    def write_bias(sel):
        def body(t, carry):
            k0 = pl.multiple_of(t * tk, tk)
            for c in range(ncol):
                blk = sc_ref[:, pl.ds(k0 + c * LANES, LANES)]
                sc_ref[:, pl.ds(k0 + c * LANES, LANES)] = sel(blk, k0 + c * LANES)
            return carry
        lax.fori_loop(0, n_tiles, body, 0)

    @pl.when(jnp.logical_not(tied))
    def _():
        write_bias(lambda blk, j0: jnp.where(blk >= thr, 0.0, NEG_INF))

    @pl.when(tied)
    def _():
        need = kf - count(lambda blk, j0: jnp.where(blk > thr, 1.0, 0.0))

        def jbis(it, ans):
            trial = ans + jnp.left_shift(jnp.int32(1), idx_bits - 1 - it)
            cnt = count(lambda blk, j0: jnp.where(blk == thr, jnp.where(j0 + lane < trial, 1.0, 0.0), 0.0))
            return jnp.where(cnt < need, trial, ans)

        ans = lax.fori_loop(0, idx_bits, jbis, jnp.zeros((tq, LANES), I32))
        write_bias(lambda blk, j0: jnp.where(
            blk > thr, 0.0, jnp.where(blk == thr, jnp.where(j0 + lane <= ans, 0.0, NEG_INF), NEG_INF)))

    m_ref[...] = jnp.full(m_ref.shape, M_INIT, F32)
    acc_ref[...] = jnp.zeros(acc_ref.shape, F32)

    def attend_tile(t, carry):
        k0 = pl.multiple_of(t * tk, tk)
        bias = sc_ref[:, pl.ds(k0, tk)]
        for g in range(N_KV_HEADS):
            qg = q_ref[0, g * group:(g + 1) * group].reshape(group * tq, HEAD_DIM)
            s = lax.dot_general(qg, k_ref[0, g, pl.ds(k0, tk), :], (((1,), (1,)), ((), ())),
                                preferred_element_type=F32)
            s = (s.reshape(group, tq, tk) + bias[None]).reshape(group * tq, tk)
            m_prev = m_ref[g]
            m_new = jnp.maximum(m_prev, jnp.max(s, axis=1, keepdims=True))
            p = jnp.exp2(s - jnp.concatenate([m_new] * ncol, axis=1))
            pv = jnp.dot(p.astype(BF16), v_ref[0, g, pl.ds(k0, tk), :], preferred_element_type=F32)
            acc_ref[g] = jnp.exp2(m_prev - m_new) * acc_ref[g] + pv
            m_ref[g] = m_new
        return carry

    lax.fori_loop(0, n_tiles, attend_tile, 0)

    for g in range(N_KV_HEADS):
        acc = acc_ref[g]
        o = acc[:, :HEAD_DIM] / acc[:, HEAD_DIM:HEAD_DIM + 1]
        for r in range(0, group, 2):
            pair = jnp.concatenate([o[r * tq:(r + 1) * tq], o[(r + 1) * tq:(r + 2) * tq]], axis=1)
            h0 = g * group + r
            o_ref[:, h0 * HEAD_DIM:(h0 + 2) * HEAD_DIM] = pair.astype(BF16)


def _dsa(qh, qih, kw_flat, kw_row_off, kh, vh, kih, *, tq, tk, k_sel, q_off, n_valid_end):
    nb, _, s, _ = qh.shape
    lp = kh.shape[2]
    nq = s // tq
    kern = functools.partial(_dsa_kernel, tq=tq, tk=tk, k_sel=k_sel, q_off=q_off, n_valid_end=n_valid_end,
                             idx_bits=max(1, int(lp).bit_length()))
    per_batch = lambda shp: pl.BlockSpec((1,) + shp, lambda b, i: (b,) + (0,) * len(shp), pipeline_mode=pl.Buffered(1))
    return pl.pallas_call(
        kern,
        out_shape=jax.ShapeDtypeStruct((nb * s, N_HEADS * HEAD_DIM), BF16),
        grid=(nb, nq),
        in_specs=[pl.BlockSpec((1, N_HEADS, tq, HEAD_DIM), lambda b, i: (b, 0, i, 0)),
                  pl.BlockSpec((1, IDX_HEADS, tq, IDX_DIM), lambda b, i: (b, 0, i, 0)),
                  pl.BlockSpec((tq, LANES), lambda b, i: (kw_row_off // tq + b * nq + i, 0)),
                  per_batch((N_KV_HEADS, lp, HEAD_DIM)), per_batch((N_KV_HEADS, lp, LANES)), per_batch((lp, IDX_DIM))],
        out_specs=pl.BlockSpec((tq, N_HEADS * HEAD_DIM), lambda b, i: (b * nq + i, 0)),
        scratch_shapes=[pltpu.VMEM((tq, (lp // LANES | 1) * LANES), F32),
                        pltpu.VMEM((N_KV_HEADS, (N_HEADS // N_KV_HEADS) * tq, LANES), F32),
                        pltpu.VMEM((N_KV_HEADS, (N_HEADS // N_KV_HEADS) * tq, LANES), F32)],
        compiler_params=_cparams(("arbitrary", "arbitrary")),
        name="dsa",
    )(qh, qih, kw_flat, kh, vh, kih)


def _dsa_t_kernel(q_ref, qi_ref, wt_ref, k_ref, vt_ref, ki_ref, o_ref, sc_ref, sch_ref, m_ref, mu_ref, acc_ref,
                  s_ref, s2_ref,
                  *, tk, k_sel, n_valid_end, idx_bits):
    tq = LANES
    i = pl.program_id(1)
    qf0 = i * tq
    klim = jnp.minimum(FRONT + qf0 + tq, n_valid_end)
    n_tiles = (klim + tk - 1) // tk
    kf = float(k_sel)
    group = N_HEADS // N_KV_HEADS
    fold = 32
    fold_h = 64
    assert sch_ref.shape[0] // fold_h <= 256

    wt = wt_ref[0]
    qlane = lax.broadcasted_iota(I32, (1, tq), 1)
    lim = jnp.minimum(FRONT + ((qf0 + qlane) // CHUNK + 1) * CHUNK, n_valid_end)
    qi_all = qi_ref[0].reshape(IDX_HEADS * tq, IDX_DIM)

    def score_tile(t, carry):
        th = tk // 2
        for half in range(2):
            k0 = pl.multiple_of(t * tk, tk) + half * th
            rel = lax.dot_general(ki_ref[0, pl.ds(k0, th), :], qi_all, (((1,), (1,)), ((), ())),
                                  preferred_element_type=F32)
            sc = jnp.zeros((th, tq), F32)
            for h in range(IDX_HEADS):
                sc = sc + wt[h:h + 1, :] * jnp.maximum(rel[:, h * tq:(h + 1) * tq], 0.0)
            j = k0 + lax.broadcasted_iota(I32, (th, tq), 0)
            j = jnp.where(j >= FRONT - N_META, j, n_valid_end)
            sc = jnp.where(j < lim, sc, NEG_INF)
            sc_ref[pl.ds(k0, th), :] = sc
            sch_ref[pl.ds(k0, th), :] = _high_half(sc).astype(HALF)
        return carry

    lax.fori_loop(0, n_tiles, score_tile, 0)

    def count(pred):
        def body(t, acc):
            k0 = pl.multiple_of(t * tk, tk)
            ones = pred(sc_ref[pl.ds(k0, tk), :], k0)
            return acc + jnp.sum(ones.reshape(tk // fold, fold, tq), axis=0)
        acc = lax.fori_loop(0, n_tiles, body, jnp.zeros((fold, tq), F32))
        return jnp.sum(acc, axis=0, keepdims=True)

    def count_high(thr):
        thr_h = _high_half(thr).astype(HALF)
        one, zero = jnp.ones((), HALF), jnp.zeros((), HALF)

        def body(t, acc):
            ones = jnp.where(sch_ref[pl.ds(pl.multiple_of(t * tk, tk), tk), :] >= thr_h, one, zero)
            for c in range(tk // fold_h):
                acc = acc + ones[c * fold_h:(c + 1) * fold_h]
            return acc
        acc = lax.fori_loop(0, n_tiles, body, jnp.zeros((fold_h, tq), HALF))
        return jnp.sum(acc.astype(F32), axis=0, keepdims=True)

    def bisect(counter):
        def step(it, u):
            trial = u | jnp.left_shift(jnp.int32(1), 31 - it)
            return jnp.where(counter(_key_to_float(trial)) >= kf, trial, u)
        return step

    u = lax.fori_loop(0, 16, bisect(count_high), jnp.zeros((1, tq), I32))
    u = lax.fori_loop(16, 32, bisect(lambda thr: count(lambda blk, k0: jnp.where(blk >= thr, 1.0, 0.0))), u)
    key = jnp.maximum(u ^ INT_MIN, SKEY_LOWEST_FINITE)
    thr = _skey_to_float(key)
    n_ge = count(lambda blk, k0: jnp.where(blk >= thr, 1.0, 0.0))
    tied = jnp.max(n_ge) > kf

    def write_bias(sel):
        def body(t, carry):
            k0 = pl.multiple_of(t * tk, tk)
            sch_ref[pl.ds(k0, tk), :] = sel(sc_ref[pl.ds(k0, tk), :], k0).astype(HALF)
            return carry
        lax.fori_loop(0, n_tiles, body, 0)

    @pl.when(jnp.logical_not(tied))
    def _():
        write_bias(lambda blk, k0: jnp.where(blk >= thr, 0.0, NEG_INF))

    @pl.when(tied)
    def _():
        need = kf - count(lambda blk, k0: jnp.where(blk > thr, 1.0, 0.0))
        krow = lax.broadcasted_iota(I32, (tk, tq), 0)

        def jbis(it, ans):
            trial = ans + jnp.left_shift(jnp.int32(1), idx_bits - 1 - it)
            cnt = count(lambda blk, k0: jnp.where(blk == thr, jnp.where(k0 + krow < trial, 1.0, 0.0), 0.0))
            return jnp.where(cnt < need, trial, ans)

        ans = lax.fori_loop(0, idx_bits, jbis, jnp.zeros((1, tq), I32))
        write_bias(lambda blk, k0: jnp.where(
            blk > thr, 0.0, jnp.where(blk == thr, jnp.where(k0 + krow <= ans, 0.0, NEG_INF), NEG_INF)))

    m_ref[...] = jnp.full(m_ref.shape, M_INIT, F32)
    acc_ref[...] = jnp.zeros(acc_ref.shape, F32)

    mu_ref[...] = jnp.full(mu_ref.shape, M_INIT, F32)

    def logits(t, buf, g, m_before):
        k0 = pl.multiple_of(t * tk, tk)
        bias = sch_ref[pl.ds(k0, tk), :].astype(BF16)
        qg = q_ref[0, g * group:(g + 1) * group].reshape(group * tq, HEAD_DIM)
        s = lax.dot_general(k_ref[0, g, pl.ds(k0, tk), :], qg, (((1,), (1,)), ((), ())),
                            preferred_element_type=F32).astype(BF16)
        s = jnp.concatenate([s[:, r * tq:(r + 1) * tq] + bias for r in range(group)], axis=1)
        buf[g] = s
        return jnp.maximum(m_before, jnp.max(s, axis=0, keepdims=True).astype(F32))

    def values(t, buf, g, m_t):
        p = jnp.exp2(buf[g] - m_t.astype(BF16))
        pv = jnp.dot(vt_ref[0, g, :, pl.ds(pl.multiple_of(t * tk, tk), tk)], p.astype(BF16),
                     preferred_element_type=F32)
        acc_ref[g] = jnp.exp2(mu_ref[g] - m_t) * acc_ref[g] + pv
        mu_ref[g] = m_t

    def step(t_next, buf_next, t, buf):
        for g in range(N_KV_HEADS):
            m_t = m_ref[g]
            m_ref[g] = logits(t_next, buf_next, g, m_t)
            values(t, buf, g, m_t)

    for g in range(N_KV_HEADS):
        m_ref[g] = logits(0, s_ref, g, m_ref[g])

    def tile_pair(j, carry):
        step(2 * j + 1, s2_ref, 2 * j, s_ref)
        step(jnp.minimum(2 * j + 2, n_tiles - 1), s_ref, 2 * j + 1, s2_ref)
        return carry

    lax.fori_loop(0, n_tiles >> 1, tile_pair, 0)

    @pl.when((n_tiles & 1) == 1)
    def _():
        for g in range(N_KV_HEADS):
            values(n_tiles - 1, s_ref, g, m_ref[g])

    for g in range(N_KV_HEADS):
        acc = acc_ref[g]
        o = acc[:HEAD_DIM, :] / acc[HEAD_DIM:HEAD_DIM + 1, :]
        for r in range(0, group, 2):
            pair = jnp.concatenate([o[:, r * tq:(r + 1) * tq], o[:, (r + 1) * tq:(r + 2) * tq]], axis=0)
            h0 = g * group + r
            o_ref[:, h0 * HEAD_DIM:(h0 + 2) * HEAD_DIM] = pair.T.astype(BF16)


def _dsa_t(qh, qih, wt, kh, vt, kih, *, tk, k_sel, n_valid_end):
    nb, _, s, _ = qh.shape
    lp = kh.shape[2]
    tq = LANES
    nq = s // tq
    group = N_HEADS // N_KV_HEADS
    kern = functools.partial(_dsa_t_kernel, tk=tk, k_sel=k_sel, n_valid_end=n_valid_end,
                             idx_bits=max(1, int(lp).bit_length()))
    per_batch = lambda shp: pl.BlockSpec((1,) + shp, lambda b, i: (b,) + (0,) * len(shp), pipeline_mode=pl.Buffered(1))
    return pl.pallas_call(
        kern,
        out_shape=jax.ShapeDtypeStruct((nb * s, N_HEADS * HEAD_DIM), BF16),
        grid=(nb, nq),
        in_specs=[pl.BlockSpec((1, N_HEADS, tq, HEAD_DIM), lambda b, i: (b, 0, i, 0)),
                  pl.BlockSpec((1, IDX_HEADS, tq, IDX_DIM), lambda b, i: (b, 0, i, 0)),
                  pl.BlockSpec((1, IDX_HEADS, tq), lambda b, i: (b, 0, i)),
                  per_batch((N_KV_HEADS, lp, HEAD_DIM)), per_batch((N_KV_HEADS, VT_ROWS, lp)), per_batch((lp, IDX_DIM))],
        out_specs=pl.BlockSpec((tq, N_HEADS * HEAD_DIM), lambda b, i: (b * nq + i, 0)),
        scratch_shapes=[pltpu.VMEM((lp, tq), F32), pltpu.VMEM((lp, tq), HALF),
                        pltpu.VMEM((N_KV_HEADS, 1, group * tq), F32),
                        pltpu.VMEM((N_KV_HEADS, 1, group * tq), F32),
                        pltpu.VMEM((N_KV_HEADS, VT_ROWS, group * tq), F32),
                        pltpu.VMEM((N_KV_HEADS, tk, group * tq), BF16),
                        pltpu.VMEM((N_KV_HEADS, tk, group * tq), BF16)],
        compiler_params=_cparams(("arbitrary", "arbitrary")),
        name="dsa_t",
    )(qh, qih, wt, kh, vt, kih)


def _pad_keys(meta_rows, mid_rows, lp):
    nb = mid_rows[0].shape[0]
    tail = mid_rows[0].shape[2:]
    meta_b = jnp.broadcast_to(meta_rows[None], (nb,) + meta_rows.shape)
    used = FRONT + sum(m.shape[1] for m in mid_rows)
    parts = [jnp.zeros((nb, FRONT - N_META) + tail, meta_rows.dtype), meta_b] + list(mid_rows)
    if lp > used:
        parts.append(jnp.zeros((nb, lp - used) + tail, meta_rows.dtype))
    return jnp.concatenate(parts, axis=1)


def _attention_inputs(k_meta, v_meta, ki_meta, k_mid, v_mid, ki_mid, lp, keys_on_lanes):
    nb = k_mid[0].shape[0]
    kp = _pad_keys(k_meta, k_mid, lp).astype(BF16).reshape(nb, lp, N_KV_HEADS, HEAD_DIM).transpose(0, 2, 1, 3)
    vp = _pad_keys(v_meta, v_mid, lp).astype(BF16).reshape(nb, lp, N_KV_HEADS, HEAD_DIM)
    if keys_on_lanes:
        vp = vp.transpose(0, 2, 3, 1)
        ones = jnp.ones((nb, N_KV_HEADS, 1, lp), BF16)
        vp = jnp.concatenate([vp, ones, jnp.zeros((nb, N_KV_HEADS, VT_ROWS - HEAD_DIM - 1, lp), BF16)], axis=2)
    else:
        vp = vp.transpose(0, 2, 1, 3)
        ones = jnp.ones(vp.shape[:3] + (1,), BF16)
        vp = jnp.concatenate([vp, ones, jnp.zeros(vp.shape[:3] + (LANES - HEAD_DIM - 1,), BF16)], axis=3)
    kip = _pad_keys(ki_meta, ki_mid, lp).astype(BF16)
    return kp, vp, kip


def _merge_kernel(xp_ref, xs_ref, cp_ref, cs_ref, ap_ref, as_ref, gate_ref, wao_ref, wout_ref, gffn_ref, wr_ref, br_ref,
                  x1_ref, hn_ref, ids_ref, gts_ref, *, n_first):
    first = pl.program_id(0) < n_first
    x = jnp.where(first, xp_ref[...], xs_ref[...])
    conv = jnp.where(first, cp_ref[...], cs_ref[...]).astype(F32)
    attn = jnp.where(first, ap_ref[...], as_ref[...])
    ao = jnp.dot(attn, wao_ref[...], preferred_element_type=F32)
    gates = gate_ref[...].astype(F32)
    mix = gates[:, :D_MODEL] * conv + gates[:, D_MODEL:] * ao
    x1 = x + jnp.dot(mix.astype(BF16), wout_ref[...], preferred_element_type=F32)
    x1_ref[...] = x1
    hn = x1 * lax.rsqrt(jnp.mean(x1 * x1, axis=-1, keepdims=True) + EPS) * gffn_ref[...]
    tr = hn.shape[0]
    for c in range(SUB):
        hn_ref[pl.ds(c, tr, stride=SUB), :] = hn[:, c * LANES:(c + 1) * LANES]
    logits = jnp.dot(hn, wr_ref[...], preferred_element_type=F32, precision=lax.Precision.HIGHEST) + br_ref[...]
    lane = lax.broadcasted_iota(I32, logits.shape, 1)
    lg = jnp.where(lane < N_EXPERTS, logits, NEG_INF)
    vals, ids = [], []
    for _ in range(TOP_K_EXPERTS):
        m = jnp.max(lg, axis=1, keepdims=True)
        idx = jnp.min(jnp.where(lg == m, lane, LANES), axis=1, keepdims=True)
        vals.append(m)
        ids.append(idx)
        lg = jnp.where(lane == idx, NEG_INF, lg)
    es = [jnp.exp(v - vals[0]) for v in vals]
    den = es[0]
    for e in es[1:]:
        den = den + e
    ids_out = jnp.zeros(logits.shape, I32)
    gts_out = jnp.zeros(logits.shape, F32)
    for j in range(TOP_K_EXPERTS):
        ids_out = jnp.where(lane == j, ids[j], ids_out)
        gts_out = jnp.where(lane == j, es[j] / den, gts_out)
    ids_ref[...] = ids_out
    gts_ref[...] = gts_out


def _merge(xp, xs, conv_p, conv_s, attn_p, attn_s, gates, wao, wout, gffn, wr, br, tr):
    n_first, n_second = xp.shape[0] // tr, xs.shape[0] // tr
    rows = xp.shape[0] + xs.shape[0]
    first = lambda w: pl.BlockSpec((tr, w), lambda i: (jnp.minimum(i, n_first - 1), 0))
    second = lambda w: pl.BlockSpec((tr, w), lambda i: (jnp.maximum(i - n_first, 0), 0))
    row = lambda w: pl.BlockSpec((tr, w), lambda i: (i, 0))
    return pl.pallas_call(
        functools.partial(_merge_kernel, n_first=n_first),
        out_shape=[jax.ShapeDtypeStruct((rows, D_MODEL), F32), jax.ShapeDtypeStruct((rows * SUB, LANES), F32),
                   jax.ShapeDtypeStruct((rows, LANES), I32), jax.ShapeDtypeStruct((rows, LANES), F32)],
        grid=(n_first + n_second,),
        in_specs=[first(D_MODEL), second(D_MODEL), first(D_MODEL), second(D_MODEL), first(D_MODEL), second(D_MODEL),
                  row(2 * D_MODEL), _resident(wao), _resident(wout), _resident(gffn), _resident(wr), _resident(br)],
        out_specs=[row(D_MODEL), pl.BlockSpec((tr * SUB, LANES), lambda i: (i, 0)), row(LANES), row(LANES)],
        compiler_params=_cparams(("arbitrary",)),
        name="merge",
    )(xp, xs, conv_p, conv_s, attn_p, attn_s, gates, wao, wout, gffn, wr, br)


KEY_ALIGN = 1024


def _key_window(tm):
    return -(-(tm + KEY_ALIGN - 1) // KEY_ALIGN) * KEY_ALIGN


def _for_rows(n, fn):
    def group(gi, carry):
        for u in range(SUB):
            fn(gi * SUB + u)
        return carry

    def single(r, carry):
        fn(r)
        return carry

    shift = SUB.bit_length() - 1
    lax.fori_loop(0, n >> shift, group, 0)
    lax.fori_loop((n >> shift) << shift, n, single, 0)


def _moe_kernel(te_ref, base_ref, nval_ref, nreal_ref, keys_hbm, h_hbm, w1_ref, b1_ref, w3_ref, b3_ref, w2_ref, b2_ref,
                y_hbm, key_sm, xbuf, xb, ybuf, wb, ksem, gsem, ssem, *, tm, n_tok, pair_bits):
    i = pl.program_id(0)
    nreal = nreal_ref[0]
    window = key_sm.shape[0] // 3
    slot_shift = TOP_K_EXPERTS.bit_length() - 1

    def keys_copy(tile):
        s = lax.rem(tile, 3)
        start = pl.multiple_of(base_ref[tile] & -KEY_ALIGN, KEY_ALIGN)
        return pltpu.make_async_copy(keys_hbm.at[pl.ds(start, window)],
                                     key_sm.at[pl.ds(pl.multiple_of(s * window, KEY_ALIGN), window)], ksem.at[s])

    def pair_reader(tile):
        first = lax.rem(tile, 3) * window + (base_ref[tile] & (KEY_ALIGN - 1))
        return lambda r: key_sm[first + r] & ((1 << pair_bits) - 1)

    def gather_row(s, r, token):
        return pltpu.make_async_copy(h_hbm.at[pl.ds(token * SUB, SUB)], xbuf.at[s, pl.ds(r * SUB, SUB)], gsem.at[s])

    def scatter_row(r, out_token):
        return pltpu.make_async_copy(ybuf.at[pl.ds(r * SUB, SUB)], y_hbm.at[pl.ds(out_token * SUB, SUB)], ssem.at[0])

    def start_gather(tile):
        pair_of, s = pair_reader(tile), tile & 1
        _for_rows(tm, lambda r: gather_row(s, r, pair_of(r) >> slot_shift).start())

    @pl.when(jnp.logical_and(i == 0, nreal > 0))
    def _():
        keys_copy(0).start()

        @pl.when(nreal > 1)
        def _():
            keys_copy(1).start()

        keys_copy(0).wait()
        start_gather(0)

    @pl.when(i + 1 < nreal)
    def _():
        keys_copy(i + 1).wait()
        start_gather(i + 1)

    @pl.when(i + 2 < nreal)
    def _():
        keys_copy(i + 2).start()

    @pl.when(i < nreal)
    def _():
        s = i & 1
        _for_rows(tm, lambda r: gather_row(s, r, 0).wait())
        for c in range(SUB):
            xb[:, c * LANES:(c + 1) * LANES] = xbuf[s, pl.ds(c, tm, stride=SUB), :].astype(BF16)
        @pl.when(jnp.logical_or(i == 0, te_ref[i] != te_ref[jnp.maximum(i - 1, 0)]))
        def _():
            for j, w_ref in enumerate((w1_ref, w3_ref, w2_ref)):
                wb[j] = w_ref[0].astype(BF16)

        x = xb[...]
        g = jnp.minimum(jnp.dot(x, wb[0], preferred_element_type=F32) + b1_ref[0], SWIGLU_LIMIT)
        u = jnp.clip(jnp.dot(x, wb[1], preferred_element_type=F32) + b3_ref[0], -SWIGLU_LIMIT, SWIGLU_LIMIT)
        act = g * jax.nn.sigmoid(SWIGLU_ALPHA * g) * (u + 1.0)
        y = jnp.dot(act.astype(BF16), wb[2], preferred_element_type=F32) + b2_ref[0]

        @pl.when(i > 0)
        def _():
            _for_rows(nval_ref[i - 1], lambda r: scatter_row(r, 0).wait())

        for c in range(SUB):
            ybuf[pl.ds(c, tm, stride=SUB), :] = y[:, c * LANES:(c + 1) * LANES]

        pair_of = pair_reader(i)

        def put(r):
            pair = pair_of(r)
            scatter_row(r, (pair & (TOP_K_EXPERTS - 1)) * n_tok + (pair >> slot_shift)).start()

        _for_rows(nval_ref[i], put)

        @pl.when(i == nreal - 1)
        def _():
            _for_rows(nval_ref[i], lambda r: scatter_row(r, 0).wait())


def _moe(hn, keys, tile_expert, base, nval, n_real, w1, b1, w3, b3, w2, b2, n_tok, pair_bits, tm):
    nt = tile_expert.shape[0]
    by_expert = lambda i, te, *_: (te[i], 0, 0)
    wspec = pl.BlockSpec((1, D_MODEL, D_FF), by_expert)
    w2spec = pl.BlockSpec((1, D_FF, D_MODEL), by_expert)
    bspec = pl.BlockSpec((1, 1, D_FF), by_expert)
    any_spec = pl.BlockSpec(memory_space=pl.ANY)
    grid_spec = pltpu.PrefetchScalarGridSpec(
        num_scalar_prefetch=4,
        grid=(nt,),
        in_specs=[any_spec, any_spec, wspec, bspec, wspec, bspec, w2spec, bspec],
        out_specs=any_spec,
        scratch_shapes=[pltpu.SMEM((3 * _key_window(tm),), I32),
                        pltpu.VMEM((2, tm * SUB, LANES), F32), pltpu.VMEM((tm, D_MODEL), BF16),
                        pltpu.VMEM((tm * SUB, LANES), F32), pltpu.VMEM((3, D_MODEL, D_FF), BF16),
                        pltpu.SemaphoreType.DMA((3,)), pltpu.SemaphoreType.DMA((2,)), pltpu.SemaphoreType.DMA((1,))],
    )
    return pl.pallas_call(
        functools.partial(_moe_kernel, tm=tm, n_tok=n_tok, pair_bits=pair_bits),
        out_shape=jax.ShapeDtypeStruct((TOP_K_EXPERTS * n_tok * SUB, LANES), F32),
        grid_spec=grid_spec,
        compiler_params=_cparams(("arbitrary",)),
        name="moe",
    )(tile_expert, base, nval, n_real, keys, hn, w1, b1, w3, b3, w2, b2)


def _route(ids, n, tm):
    n_pairs = n * TOP_K_EXPERTS
    pair_bits = max(1, (n_pairs - 1).bit_length())
    assert pair_bits + (N_EXPERTS - 1).bit_length() <= 31 and tm % LANES == 0
    assert TOP_K_EXPERTS & (TOP_K_EXPERTS - 1) == 0
    pair = jnp.arange(n_pairs, dtype=I32)
    keys = jnp.sort(jnp.left_shift(ids.reshape(-1), pair_bits) | pair)
    starts = jnp.searchsorted(keys, jnp.left_shift(jnp.arange(N_EXPERTS + 1, dtype=I32), pair_bits),
                              method="compare_all").astype(I32)
    counts = starts[1:] - starts[:-1]
    tiles = (counts + tm - 1) // tm
    tile_end = jnp.cumsum(tiles)
    nt = -(-(n_pairs + N_EXPERTS * (tm - 1)) // tm)
    ti = jnp.arange(nt, dtype=I32)
    te = jnp.minimum(jnp.searchsorted(tile_end, ti, side="right", method="compare_all"), N_EXPERTS - 1).astype(I32)
    within = ti - (tile_end - tiles)[te]
    base = jnp.clip(starts[:-1][te] + within * tm, 0, n_pairs - 1).astype(I32)
    nval = jnp.clip(counts[te] - within * tm, 0, tm).astype(I32)
    padded = -(-n_pairs // KEY_ALIGN) * KEY_ALIGN + _key_window(tm)
    keys = jnp.pad(keys, (0, padded - n_pairs))
    return keys, te, base, nval, tile_end[-1:].astype(I32), pair_bits


def _final_kernel(x1_ref, y0_ref, y1_ref, y2_ref, y3_ref, gts_ref, g_ref, out_ref):
    gts = gts_ref[...]
    tr = gts.shape[0]
    cols = []
    for c in range(SUB):
        acc = x1_ref[:, c * LANES:(c + 1) * LANES]
        for j, y_ref in enumerate((y0_ref, y1_ref, y2_ref, y3_ref)):
            acc = acc + gts[:, j:j + 1] * y_ref[pl.ds(c, tr, stride=SUB), :]
        cols.append(acc)
    ss = cols[0] * cols[0]
    for acc in cols[1:]:
        ss = ss + acc * acc
    scale = lax.rsqrt(jnp.sum(ss, axis=-1, keepdims=True) / D_MODEL + EPS)
    for c, acc in enumerate(cols):
        out_ref[:, c * LANES:(c + 1) * LANES] = acc * scale * g_ref[:, c * LANES:(c + 1) * LANES]


def _final(x1, yslots, gts, g, tr):
    n = x1.shape[0]
    nblk = n // tr
    row = lambda w: pl.BlockSpec((tr, w), lambda i: (i, 0))
    slot = lambda j: pl.BlockSpec((tr * SUB, LANES), lambda i: (j * nblk + i, 0))
    return pl.pallas_call(
        _final_kernel,
        out_shape=jax.ShapeDtypeStruct((n, D_MODEL), F32),
        grid=(nblk,),
        in_specs=[row(D_MODEL)] + [slot(j) for j in range(TOP_K_EXPERTS)] + [row(LANES), _resident(g)],
        out_specs=row(D_MODEL),
        compiler_params=_cparams(("arbitrary",)),
        name="final",
    )(x1, yslots, yslots, yslots, yslots, gts, g)


def _tile(n, pref):
    t = min(n, pref)
    while n % t:
        t //= 2
    return t


def kernel(x_prompt, x_sample, cache_k, cache_v, cache_idx_k, state_conv, meta_tokens, g_norm_mix, w_in, conv_w,
           conv_b, conv_ln_g, conv_ln_b, w_conv_out, w_attn_out, w_out, g_norm_ffn, w_router, b_router,
           w1, b1, w3, b3, w2, b2, g_norm_final):
    nb, s_p, _ = x_prompt.shape
    db, s_n, _ = x_sample.shape
    past = cache_k.shape[1]
    rows_p, rows_s = nb * s_p, db * s_n
    n = rows_p + rows_s
    tr = _tile(rows_s, ROW_TILE)
    assert rows_p % tr == 0 and s_p % tr == 0 and tr % s_n == 0 and s_p % Q_TILE == 0

    cuts = [0, 2 * CONV_CH, N_HEADS * HEAD_DIM, N_KV_HEADS * HEAD_DIM, N_KV_HEADS * HEAD_DIM, IDX_HEADS * IDX_DIM,
            IDX_DIM + IDX_HEADS, 2 * D_MODEL]
    offs = [sum(cuts[:i + 1]) for i in range(len(cuts))]
    seg = [w_in[:, offs[i]:offs[i + 1]].astype(BF16) for i in range(len(cuts) - 1)]
    seg[5] = jnp.pad(seg[5], ((0, 0), (0, LANES - seg[5].shape[1])))
    g_mix = g_norm_mix.reshape(1, D_MODEL)

    pos_p = N_META + jnp.arange(s_p, dtype=I32)
    pos_s = jnp.tile(N_META + past + jnp.arange(s_n, dtype=I32), tr // s_n)
    tabs = _rope_tables(jnp.concatenate([pos_p, pos_s]))
    blocks_per_seq = s_p // tr
    n_first = rows_p // tr
    tab_map = lambda i: jnp.where(i < n_first, lax.rem(i, blocks_per_seq), blocks_per_seq)

    xp, xs = x_prompt.reshape(rows_p, D_MODEL), x_sample.reshape(rows_s, D_MODEL)
    a, q, k, v, qi, kw, gates = _project(xp, xs, tabs, tab_map, g_mix, seg, tr)

    meta = meta_tokens.astype(F32)
    tabs_m = _rope_tables(jnp.arange(N_META, dtype=I32))
    a_m, _, k_m, v_m, _, kw_m, _ = _project(meta, meta, tabs_m, lambda i: 0, g_mix, seg, N_META)
    a_m, k_m, v_m, ki_m = a_m[:N_META], k_m[:N_META], v_m[:N_META], kw_m[:N_META, :IDX_DIM]

    cw = jnp.pad(conv_w, ((0, HALO - CONV_WIDTH), (0, 0)))
    cb, lg, lb = conv_b.reshape(1, -1), conv_ln_g.reshape(1, -1), conv_ln_b.reshape(1, -1)
    wco = w_conv_out.astype(BF16)
    ctx_p = jnp.concatenate([jnp.zeros((HALO - N_META, CONV_CH), F32), a_m], axis=0)[None]
    ctx_s = jnp.pad(state_conv, ((0, 0), (HALO - (CONV_WIDTH - 1), 0), (0, 0)))
    conv_p = _conv_branch(a, 0, nb, s_p, ctx_p, cw, cb, lg, lb, wco, _tile(s_p, 256))
    conv_s = _conv_branch(a, rows_p, db, s_n, ctx_s, cw, cb, lg, lb, wco, s_n)

    k_p3, v_p3 = k[:rows_p].reshape(nb, s_p, -1), v[:rows_p].reshape(nb, s_p, -1)
    ki_p3 = kw[:rows_p, :IDX_DIM].reshape(nb, s_p, IDX_DIM)
    k_s3, v_s3 = k[rows_p:].reshape(db, s_n, -1), v[rows_p:].reshape(db, s_n, -1)
    ki_s3 = kw[rows_p:, :IDX_DIM].reshape(db, s_n, IDX_DIM)

    def heads(t, rows0, b_, s_, nh):
        return t[rows0:rows0 + b_ * s_].reshape(b_, s_, nh, -1).transpose(0, 2, 1, 3)

    tk_p = _tile(s_p, K_TILE_T)
    lp_p = -(-(FRONT + s_p) // tk_p) * tk_p
    kp, vtp, kip = _attention_inputs(k_m, v_m, ki_m, [k_p3], [v_p3], [ki_p3], lp_p, True)
    wt_p = kw[:rows_p, IDX_DIM:IDX_DIM + IDX_HEADS].reshape(nb, s_p, IDX_HEADS).transpose(0, 2, 1)
    attn_p = _dsa_t(heads(q, 0, nb, s_p, N_HEADS), heads(qi, 0, nb, s_p, IDX_HEADS), wt_p, kp, vtp, kip,
                    tk=tk_p, k_sel=min(TOPK_MAX, s_p // 4), n_valid_end=FRONT + s_p)

    tk_s = K_TILE
    lp_s = -(-(FRONT + past + s_n) // tk_s) * tk_s
    ck = cache_k.reshape(db, past, -1)
    cv = cache_v.reshape(db, past, -1)
    ks, vs, kis = _attention_inputs(k_m, v_m, ki_m, [ck, k_s3], [cv, v_s3], [cache_idx_k, ki_s3], lp_s, False)
    attn_s = _dsa(heads(q, rows_p, db, s_n, N_HEADS), heads(qi, rows_p, db, s_n, IDX_HEADS), kw, rows_p, ks, vs, kis,
                  tq=s_n, tk=tk_s, k_sel=min(TOPK_MAX, (past + s_n) // 4), q_off=past,
                  n_valid_end=FRONT + past + s_n)

    wr = jnp.pad(w_router, ((0, 0), (0, LANES - N_EXPERTS)))
    br = jnp.pad(b_router, (0, LANES - N_EXPERTS)).reshape(1, LANES)
    x1, hn, ids, gts = _merge(xp, xs, conv_p, conv_s, attn_p, attn_s, gates, w_attn_out.astype(BF16),
                              w_out.astype(BF16), g_norm_ffn.reshape(1, -1), wr, br, tr)

    tm = MOE_TILE
    keys, tile_expert, base, nval, n_real, pair_bits = _route(ids[:, :TOP_K_EXPERTS], n, tm)
    yslots = _moe(hn, keys, tile_expert, base, nval, n_real, w1, b1.reshape(N_EXPERTS, 1, D_FF),
                  w3, b3.reshape(N_EXPERTS, 1, D_FF), w2, b2.reshape(N_EXPERTS, 1, D_MODEL), n, pair_bits, tm)
    out = _final(x1, yslots, gts, g_norm_final.reshape(1, -1), tr)

    y_prompt = out[:rows_p].reshape(nb, s_p, D_MODEL)
    y_sample = out[rows_p:].reshape(db, s_n, D_MODEL)
    lead = lambda m: jnp.broadcast_to(m[None], (nb,) + m.shape)
    k_prompt = jnp.concatenate([lead(k_m), k_p3], axis=1).reshape(nb, N_META + s_p, N_KV_HEADS, HEAD_DIM)
    v_prompt = jnp.concatenate([lead(v_m), v_p3], axis=1).reshape(nb, N_META + s_p, N_KV_HEADS, HEAD_DIM)
    idxk_prompt = jnp.concatenate([lead(ki_m), ki_p3], axis=1)
    a_p3 = a[:rows_p].reshape(nb, s_p, CONV_CH)
    a_s3 = a[rows_p:].reshape(db, s_n, CONV_CH)
    keep = CONV_WIDTH - 1
    conv_state_p = jnp.concatenate([lead(a_m), a_p3], axis=1)[:, N_META + s_p - keep:]
    conv_state_s = jnp.concatenate([state_conv, a_s3], axis=1)[:, s_n:]
    return (y_prompt, y_sample, k_prompt, v_prompt, idxk_prompt, conv_state_p,
            k_s3.reshape(db, s_n, N_KV_HEADS, HEAD_DIM), v_s3.reshape(db, s_n, N_KV_HEADS, HEAD_DIM), ki_s3,
            conv_state_s)
```

```python
import functools

import jax
import jax.numpy as jnp
from jax import lax
from jax.experimental import pallas as pl
from jax.experimental.pallas import tpu as pltpu

F32, BF16, I32 = jnp.float32, jnp.bfloat16, jnp.int32
HALF = jnp.bfloat16

D_MODEL = 1024
CHUNK = 64
N_META = 16
CONV_CH = D_MODEL
CONV_WIDTH = 31
N_HEADS = 16
N_KV_HEADS = 4
HEAD_DIM = 64
ROT_DIM = HEAD_DIM // 4
ROPE_THETA = 500000.0
IDX_HEADS = 8
IDX_DIM = 64
TOPK_MAX = 256
N_EXPERTS = 32
TOP_K_EXPERTS = 4
D_FF = D_MODEL
SWIGLU_LIMIT = 7.0
SWIGLU_ALPHA = 1.702
EPS = 1e-5
Q_SCALE = HEAD_DIM ** -0.5 * 1.4426950408889634

LANES = 128
SUB = 8
VT_ROWS = HEAD_DIM + 2 * SUB
FRONT = 128
HALO = 32
VMEM_LIMIT = 56 * 1024 * 1024
NEG_INF = float("-inf")
INT_MIN = -2 ** 31
M_INIT = -1e30

ROW_TILE = 512
Q_TILE = 128
K_TILE = 512
K_TILE_T = 1024
MOE_TILE = 512


def _cparams(sem):
    return pltpu.CompilerParams(dimension_semantics=sem, vmem_limit_bytes=VMEM_LIMIT)


def _resident(arr):
    nd = arr.ndim
    return pl.BlockSpec(arr.shape, lambda *_: (0,) * nd, pipeline_mode=pl.Buffered(1))


def _rope_cols(x, c, s1, s2):
    outs = []
    for j in range(x.shape[1] // LANES):
        xj = x[:, j * LANES:(j + 1) * LANES]
        outs.append(xj * c + pltpu.roll(xj, LANES - ROT_DIM // 2, 1) * s1 + pltpu.roll(xj, ROT_DIM // 2, 1) * s2)
    return outs


def _proj_kernel(xp_ref, xs_ref, g_ref, wa_ref, wq_ref, wk_ref, wv_ref, wqi_ref, wkw_ref, wgl_ref,
                 c_ref, s1_ref, s2_ref,
                 a_ref, q_ref, k_ref, v_ref, qi_ref, kw_ref, gate_ref, *, n_first):
    x = jnp.where(pl.program_id(0) < n_first, xp_ref[...], xs_ref[...])
    h = x * lax.rsqrt(jnp.mean(x * x, axis=-1, keepdims=True) + EPS) * g_ref[...]
    hb = h.astype(BF16)
    c, s1, s2 = c_ref[...], s1_ref[...], s2_ref[...]

    def dot(w_ref):
        return jnp.dot(hb, w_ref[...], preferred_element_type=F32)

    glu = dot(wa_ref)
    a_ref[...] = glu[:, :CONV_CH] * jax.nn.sigmoid(glu[:, CONV_CH:])
    for j, col in enumerate(_rope_cols(dot(wq_ref), c, s1, s2)):
        q_ref[:, j * LANES:(j + 1) * LANES] = (col * Q_SCALE).astype(BF16)
    for j, col in enumerate(_rope_cols(dot(wk_ref), c, s1, s2)):
        k_ref[:, j * LANES:(j + 1) * LANES] = col
    v_ref[...] = dot(wv_ref)
    for j, col in enumerate(_rope_cols(dot(wqi_ref), c, s1, s2)):
        qi_ref[:, j * LANES:(j + 1) * LANES] = (col * IDX_DIM ** -0.5).astype(BF16)
    pkw = dot(wkw_ref)
    lane = lax.broadcasted_iota(I32, pkw.shape, 1)
    kw_ref[...] = jnp.where(lane < IDX_DIM, _rope_cols(pkw, c, s1, s2)[0], pkw * IDX_HEADS ** -0.5)
    gate_ref[...] = jax.nn.sigmoid(dot(wgl_ref)).astype(BF16)


def _rope_tables(pos):
    half = ROT_DIM // 2
    inv = ROPE_THETA ** (-jnp.arange(half, dtype=F32) / half)
    ang = pos.astype(F32)[:, None] * inv[None, :]
    cos, sin = jnp.cos(ang), jnp.sin(ang)
    r = pos.shape[0]
    z = lambda n: jnp.zeros((r, n), F32)
    c = jnp.concatenate([cos, cos, jnp.ones((r, HEAD_DIM - ROT_DIM), F32)], axis=1)
    s1 = jnp.concatenate([-sin, z(HEAD_DIM - half)], axis=1)
    s2 = jnp.concatenate([z(half), sin, z(HEAD_DIM - ROT_DIM)], axis=1)
    two = lambda t: jnp.concatenate([t, t], axis=1)
    return two(c), two(s1), two(s2)


def _project(xp, xs, tabs, tab_map, g, ws, tr):
    n_first, n_second = xp.shape[0] // tr, xs.shape[0] // tr
    rows = xp.shape[0] + xs.shape[0]
    row = lambda w: pl.BlockSpec((tr, w), lambda i: (i, 0))
    tab = pl.BlockSpec((tr, LANES), lambda i: (tab_map(i), 0))
    in_specs = [pl.BlockSpec((tr, D_MODEL), lambda i: (jnp.minimum(i, n_first - 1), 0)),
                pl.BlockSpec((tr, D_MODEL), lambda i: (jnp.maximum(i - n_first, 0), 0)),
                _resident(g)] + [_resident(w) for w in ws] + [tab, tab, tab]
    widths = [(CONV_CH, F32), (N_HEADS * HEAD_DIM, BF16), (N_KV_HEADS * HEAD_DIM, F32), (N_KV_HEADS * HEAD_DIM, F32),
              (IDX_HEADS * IDX_DIM, BF16), (LANES, F32), (2 * D_MODEL, BF16)]
    return pl.pallas_call(
        functools.partial(_proj_kernel, n_first=n_first),
        out_shape=[jax.ShapeDtypeStruct((rows, w), dt) for w, dt in widths],
        grid=(n_first + n_second,),
        in_specs=in_specs,
        out_specs=[row(w) for w, _ in widths],
        compiler_params=_cparams(("arbitrary",)),
        name="proj",
    )(xp, xs, g, *ws, *tabs)


def _proj_seq_kernel(x_ref, g_ref, wa_ref, wq_ref, wk_ref, wv_ref, wqi_ref, wkw_ref, wgl_ref, c_ref, s1_ref, s2_ref,
                     a_ref, q_ref, k_ref, v_ref, qi_ref, kw_ref, gate_ref, kh_ref, vt_ref, kih_ref, wt_ref):
    x = x_ref[...]
    h = x * lax.rsqrt(jnp.mean(x * x, axis=-1, keepdims=True) + EPS) * g_ref[...]
    hb = h.astype(BF16)
    c, s1, s2 = c_ref[...], s1_ref[...], s2_ref[...]

    def dot(w_ref):
        return jnp.dot(hb, w_ref[...], preferred_element_type=F32)

    def put_heads(ref, j, col):
        ref[0, 2 * j] = col[:, :HEAD_DIM].astype(BF16)
        ref[0, 2 * j + 1] = col[:, HEAD_DIM:].astype(BF16)

    glu = dot(wa_ref)
    a_ref[...] = glu[:, :CONV_CH] * jax.nn.sigmoid(glu[:, CONV_CH:])
    for j, col in enumerate(_rope_cols(dot(wq_ref), c, s1, s2)):
        put_heads(q_ref, j, col * Q_SCALE)
    for j, col in enumerate(_rope_cols(dot(wk_ref), c, s1, s2)):
        k_ref[:, j * LANES:(j + 1) * LANES] = col
        put_heads(kh_ref, j, col)
    v = dot(wv_ref)
    v_ref[...] = v
    tr = v.shape[0]
    ones_row = jnp.where(lax.broadcasted_iota(I32, (VT_ROWS - HEAD_DIM, tr), 0) == 0, 1.0, 0.0).astype(BF16)
    for j in range(v.shape[1] // LANES):
        vt = v[:, j * LANES:(j + 1) * LANES].T
        for half in range(2):
            vt_ref[0, 2 * j + half, 0:HEAD_DIM, :] = vt[half * HEAD_DIM:(half + 1) * HEAD_DIM].astype(BF16)
            vt_ref[0, 2 * j + half, HEAD_DIM:VT_ROWS, :] = ones_row
    for j, col in enumerate(_rope_cols(dot(wqi_ref), c, s1, s2)):
        put_heads(qi_ref, j, col * IDX_DIM ** -0.5)
    pkw = dot(wkw_ref)
    lane = lax.broadcasted_iota(I32, pkw.shape, 1)
    kw = jnp.where(lane < IDX_DIM, _rope_cols(pkw, c, s1, s2)[0], pkw * IDX_HEADS ** -0.5)
    kw_ref[...] = kw
    kih_ref[0] = kw[:, :IDX_DIM].astype(BF16)
    wt_ref[0] = kw.T[IDX_DIM:IDX_DIM + IDX_HEADS, :]
    gate_ref[...] = jax.nn.sigmoid(dot(wgl_ref)).astype(BF16)


def _project_seq(x, nb, s, tabs, g, ws, tr):
    nt = s // tr
    rows = nb * s
    flat = lambda w: pl.BlockSpec((tr, w), lambda b, i: (b * nt + i, 0))
    tab = pl.BlockSpec((tr, LANES), lambda b, i: (i, 0))
    heads = lambda nh: pl.BlockSpec((1, nh, tr, HEAD_DIM), lambda b, i: (b, 0, i, 0))
    widths = [(CONV_CH, F32), None, (N_KV_HEADS * HEAD_DIM, F32), (N_KV_HEADS * HEAD_DIM, F32), None, (LANES, F32),
              (2 * D_MODEL, BF16)]
    shapes = [jax.ShapeDtypeStruct((rows, w[0]), w[1]) if w else None for w in widths]
    specs = [flat(w[0]) if w else None for w in widths]
    shapes[1], specs[1] = jax.ShapeDtypeStruct((nb, N_HEADS, s, HEAD_DIM), BF16), heads(N_HEADS)
    shapes[4], specs[4] = jax.ShapeDtypeStruct((nb, IDX_HEADS, s, IDX_DIM), BF16), heads(IDX_HEADS)
    shapes += [jax.ShapeDtypeStruct((nb, N_KV_HEADS, s, HEAD_DIM), BF16),
               jax.ShapeDtypeStruct((nb, N_KV_HEADS, VT_ROWS, s), BF16),
               jax.ShapeDtypeStruct((nb, s, IDX_DIM), BF16),
               jax.ShapeDtypeStruct((nb, IDX_HEADS, s), F32)]
    specs += [heads(N_KV_HEADS),
              pl.BlockSpec((1, N_KV_HEADS, VT_ROWS, tr), lambda b, i: (b, 0, 0, i)),
              pl.BlockSpec((1, tr, IDX_DIM), lambda b, i: (b, i, 0)),
              pl.BlockSpec((1, IDX_HEADS, tr), lambda b, i: (b, 0, i))]
    return pl.pallas_call(
        _proj_seq_kernel,
        out_shape=shapes,
        grid=(nb, nt),
        in_specs=[flat(D_MODEL), _resident(g)] + [_resident(w) for w in ws] + [tab, tab, tab],
        out_specs=specs,
        compiler_params=_cparams(("arbitrary", "arbitrary")),
        name="proj_seq",
    )(x, g, *ws, *tabs)


def _conv_kernel(cur_ref, halo_ref, lc_ref, cw_ref, cb_ref, lg_ref, lb_ref, wo_ref, out_ref, e_ref, c_ref, *, tr, rc):
    halo = jnp.where(pl.program_id(1) == 0, lc_ref[0], halo_ref[...])
    e_ref[0, 0:HALO, :] = halo
    e_ref[0, HALO:HALO + tr, :] = cur_ref[...]
    n = tr + HALO - 8
    for b in range(1, 8):
        e_ref[b, 0:n, :] = e_ref[0, b:b + n, :]
    first = HALO - (CONV_WIDTH - 1)

    def chunk(ci, carry):
        r0 = pl.multiple_of(ci * rc, rc)
        acc = jnp.zeros((rc, CONV_CH), F32) + cb_ref[...]
        for w in range(CONV_WIDTH):
            s = first + w
            acc = acc + e_ref[s % 8, pl.ds(r0 + 8 * (s // 8), rc), :] * cw_ref[pl.ds(w, 1), :]
        c_ref[pl.ds(r0, rc), :] = acc
        return carry

    lax.fori_loop(0, tr // rc, chunk, 0)
    c = c_ref[...]
    mu = jnp.mean(c, axis=-1, keepdims=True)
    xc = c - mu
    y = xc * lax.rsqrt(jnp.mean(xc * xc, axis=-1, keepdims=True) + EPS) * lg_ref[...] + lb_ref[...]
    act = (y * jax.nn.sigmoid(y)).astype(BF16)
    out_ref[...] = jnp.dot(act, wo_ref[...], preferred_element_type=F32).astype(BF16)


def _conv_branch(a_flat, row_off, nb, t, left_ctx, cw, cb, lg, lb, wo, tr):
    nt = t // tr
    rc = min(tr, 32)
    per_batch_ctx = left_ctx.shape[0] > 1
    cur = pl.BlockSpec((tr, CONV_CH), lambda b, i: (row_off // tr + b * nt + i, 0))
    halo = pl.BlockSpec((HALO, CONV_CH),
                        lambda b, i: (jnp.maximum((row_off + b * t + i * tr) // HALO - 1, 0), 0))
    lctx = pl.BlockSpec((1, HALO, CONV_CH), lambda b, i: (b if per_batch_ctx else 0, 0, 0))
    return pl.pallas_call(
        functools.partial(_conv_kernel, tr=tr, rc=rc),
        out_shape=jax.ShapeDtypeStruct((nb * t, D_MODEL), BF16),
        grid=(nb, nt),
        in_specs=[cur, halo, lctx, _resident(cw), _resident(cb), _resident(lg), _resident(lb), _resident(wo)],
        out_specs=pl.BlockSpec((tr, D_MODEL), lambda b, i: (b * nt + i, 0)),
        scratch_shapes=[pltpu.VMEM((8, tr + HALO, CONV_CH), F32), pltpu.VMEM((tr, CONV_CH), F32)],
        compiler_params=_cparams(("arbitrary", "arbitrary")),
        name="conv",
    )(a_flat, a_flat, left_ctx, cw, cb, lg, lb, wo)


def _float_to_skey(f):
    bits = lax.bitcast_convert_type(f, I32)
    return bits ^ ((bits >> 31) & 0x7FFFFFFF)


def _skey_to_float(key):
    return lax.bitcast_convert_type(key ^ ((key >> 31) & 0x7FFFFFFF), F32)


SKEY_LOWEST_FINITE = INT_MIN + 0x00800000


def _high_half(f):
    return lax.bitcast_convert_type(lax.bitcast_convert_type(f, I32) & -65536, F32)


def _key_to_float(u):
    key = u ^ INT_MIN
    bits = key ^ ((key >> 31) & 0x7FFFFFFF)
    return lax.bitcast_convert_type(bits, F32)


def _dsa_kernel(q_ref, qi_ref, kw_ref, k_ref, v_ref, ki_ref, o_ref, sc_ref, m_ref, acc_ref,
                *, tq, tk, k_sel, q_off, n_valid_end, idx_bits):
    i = pl.program_id(1)
    qf0 = q_off + i * tq
    klim = jnp.minimum(FRONT + ((qf0 + tq - 1) // CHUNK + 1) * CHUNK, n_valid_end)
    n_tiles = (klim + tk - 1) // tk
    ncol = tk // LANES
    kf = float(k_sel)
    group = N_HEADS // N_KV_HEADS

    kwv = kw_ref[...]
    qrow = lax.broadcasted_iota(I32, (tq, 1), 0)
    lim_row = jnp.minimum(FRONT + ((qf0 + qrow) // CHUNK + 1) * CHUNK, n_valid_end)

    def score_tile(t, carry):
        k0 = pl.multiple_of(t * tk, tk)
        kt = ki_ref[0, pl.ds(k0, tk), :]
        sc = jnp.zeros((tq, tk), F32)
        for h in range(IDX_HEADS):
            rel = lax.dot_general(qi_ref[0, h], kt, (((1,), (1,)), ((), ())), preferred_element_type=F32)
            sc = sc + kwv[:, IDX_DIM + h:IDX_DIM + h + 1] * jnp.maximum(rel, 0.0)
        j = k0 + lax.broadcasted_iota(I32, (tq, tk), 1)
        j = jnp.where(j >= FRONT - N_META, j, n_valid_end)
        sc_ref[:, pl.ds(k0, tk)] = jnp.where(j < lim_row, sc, NEG_INF)
        return carry

    lax.fori_loop(0, n_tiles, score_tile, 0)

    lane = lax.broadcasted_iota(I32, (tq, LANES), 1)

    def count(pred):
        def body(t, acc):
            k0 = pl.multiple_of(t * tk, tk)
            for c in range(ncol):
                blk = sc_ref[:, pl.ds(k0 + c * LANES, LANES)]
                acc = acc + pred(blk, k0 + c * LANES)
            return acc
        acc = lax.fori_loop(0, n_tiles, body, jnp.zeros((tq, LANES), F32))
        return jnp.broadcast_to(jnp.sum(acc, axis=1, keepdims=True), (tq, LANES))

    def bisect(it, u):
        trial = u | jnp.left_shift(jnp.int32(1), 31 - it)
        thr = _key_to_float(trial)
        cnt = count(lambda blk, j0: jnp.where(blk >= thr, 1.0, 0.0))
        return jnp.where(cnt >= kf, trial, u)

    u = lax.fori_loop(0, 32, bisect, jnp.zeros((tq, LANES), I32))
    key = jnp.maximum(u ^ INT_MIN, INT_MIN + 0x00800000)
    thr = _key_to_float(key ^ INT_MIN)
    n_ge = count(lambda blk, j0: jnp.where(blk >= thr, 1.0, 0.0))
    tied = jnp.max(n_ge) > kf

    def write_bias(sel):
        def body(t, carry):
            k0 = pl.multiple_of(t * tk, tk)
            for c in range(ncol):
                blk = sc_ref[:, pl.ds(k0 + c * LANES, LANES)]
                sc_ref[:, pl.ds(k0 + c * LANES, LANES)] = sel(blk, k0 + c * LANES)
            return carry
        lax.fori_loop(0, n_tiles, body, 0)

    @pl.when(jnp.logical_not(tied))
    def _():
        write_bias(lambda blk, j0: jnp.where(blk >= thr, 0.0, NEG_INF))

    @pl.when(tied)
    def _():
        need = kf - count(lambda blk, j0: jnp.where(blk > thr, 1.0, 0.0))

        def jbis(it, ans):
            trial = ans + jnp.left_shift(jnp.int32(1), idx_bits - 1 - it)
            cnt = count(lambda blk, j0: jnp.where(blk == thr, jnp.where(j0 + lane < trial, 1.0, 0.0), 0.0))
            return jnp.where(cnt < need, trial, ans)

        ans = lax.fori_loop(0, idx_bits, jbis, jnp.zeros((tq, LANES), I32))
        write_bias(lambda blk, j0: jnp.where(
            blk > thr, 0.0, jnp.where(blk == thr, jnp.where(j0 + lane <= ans, 0.0, NEG_INF), NEG_INF)))

    m_ref[...] = jnp.full(m_ref.shape, M_INIT, F32)
    acc_ref[...] = jnp.zeros(acc_ref.shape, F32)

    def attend_tile(t, carry):
        k0 = pl.multiple_of(t * tk, tk)
        bias = sc_ref[:, pl.ds(k0, tk)]
        for g in range(N_KV_HEADS):
            qg = q_ref[0, g * group:(g + 1) * group].reshape(group * tq, HEAD_DIM)
            s = lax.dot_general(qg, k_ref[0, g, pl.ds(k0, tk), :], (((1,), (1,)), ((), ())),
                                preferred_element_type=F32)
            s = (s.reshape(group, tq, tk) + bias[None]).reshape(group * tq, tk)
            m_prev = m_ref[g]
            m_new = jnp.maximum(m_prev, jnp.max(s, axis=1, keepdims=True))
            p = jnp.exp2(s - jnp.concatenate([m_new] * ncol, axis=1))
            pv = jnp.dot(p.astype(BF16), v_ref[0, g, pl.ds(k0, tk), :], preferred_element_type=F32)
            acc_ref[g] = jnp.exp2(m_prev - m_new) * acc_ref[g] + pv
            m_ref[g] = m_new
        return carry

    lax.fori_loop(0, n_tiles, attend_tile, 0)

    for g in range(N_KV_HEADS):
        acc = acc_ref[g]
        o = acc[:, :HEAD_DIM] / acc[:, HEAD_DIM:HEAD_DIM + 1]
        for r in range(0, group, 2):
            pair = jnp.concatenate([o[r * tq:(r + 1) * tq], o[(r + 1) * tq:(r + 2) * tq]], axis=1)
            h0 = g * group + r
            o_ref[:, h0 * HEAD_DIM:(h0 + 2) * HEAD_DIM] = pair.astype(BF16)


def _dsa(qh, qih, kw_flat, kw_row_off, kh, vh, kih, *, tq, tk, k_sel, q_off, n_valid_end):
    nb, _, s, _ = qh.shape
    lp = kh.shape[2]
    nq = s // tq
    kern = functools.partial(_dsa_kernel, tq=tq, tk=tk, k_sel=k_sel, q_off=q_off, n_valid_end=n_valid_end,
                             idx_bits=max(1, int(lp).bit_length()))
    per_batch = lambda shp: pl.BlockSpec((1,) + shp, lambda b, i: (b,) + (0,) * len(shp), pipeline_mode=pl.Buffered(1))
    return pl.pallas_call(
        kern,
        out_shape=jax.ShapeDtypeStruct((nb * s, N_HEADS * HEAD_DIM), BF16),
        grid=(nb, nq),
        in_specs=[pl.BlockSpec((1, N_HEADS, tq, HEAD_DIM), lambda b, i: (b, 0, i, 0)),
                  pl.BlockSpec((1, IDX_HEADS, tq, IDX_DIM), lambda b, i: (b, 0, i, 0)),
                  pl.BlockSpec((tq, LANES), lambda b, i: (kw_row_off // tq + b * nq + i, 0)),
                  per_batch((N_KV_HEADS, lp, HEAD_DIM)), per_batch((N_KV_HEADS, lp, LANES)), per_batch((lp, IDX_DIM))],
        out_specs=pl.BlockSpec((tq, N_HEADS * HEAD_DIM), lambda b, i: (b * nq + i, 0)),
        scratch_shapes=[pltpu.VMEM((tq, (lp // LANES | 1) * LANES), F32),
                        pltpu.VMEM((N_KV_HEADS, (N_HEADS // N_KV_HEADS) * tq, LANES), F32),
                        pltpu.VMEM((N_KV_HEADS, (N_HEADS // N_KV_HEADS) * tq, LANES), F32)],
        compiler_params=_cparams(("arbitrary", "arbitrary")),
        name="dsa",
    )(qh, qih, kw_flat, kh, vh, kih)


def _dsa_t_kernel(q_ref, qi_ref, wt_ref, k_ref, vt_ref, ki_ref, o_ref, sc_ref, sch_ref, m_ref, mu_ref, acc_ref,
                  s_ref, s2_ref,
                  *, tk, k_sel, n_valid_end, idx_bits):
    tq = LANES
    i = pl.program_id(1)
    qf0 = i * tq
    klim = jnp.minimum(FRONT + qf0 + tq, n_valid_end)
    n_tiles = (klim + tk - 1) // tk
    kf = float(k_sel)
    group = N_HEADS // N_KV_HEADS
    fold = 32
    fold_h = 64
    assert sch_ref.shape[0] // fold_h <= 256

    wt = wt_ref[0]
    qlane = lax.broadcasted_iota(I32, (1, tq), 1)
    lim = jnp.minimum(FRONT + ((qf0 + qlane) // CHUNK + 1) * CHUNK, n_valid_end)
    qi_all = qi_ref[0].reshape(IDX_HEADS * tq, IDX_DIM)

    def score_tile(t, carry):
        th = tk // 2
        for half in range(2):
            k0 = pl.multiple_of(t * tk, tk) + half * th
            rel = lax.dot_general(ki_ref[0, pl.ds(k0, th), :], qi_all, (((1,), (1,)), ((), ())),
                                  preferred_element_type=F32)
            sc = jnp.zeros((th, tq), F32)
            for h in range(IDX_HEADS):
                sc = sc + wt[h:h + 1, :] * jnp.maximum(rel[:, h * tq:(h + 1) * tq], 0.0)
            j = k0 + lax.broadcasted_iota(I32, (th, tq), 0)
            j = jnp.where(j >= FRONT - N_META, j, n_valid_end)
            sc = jnp.where(j < lim, sc, NEG_INF)
            sc_ref[pl.ds(k0, th), :] = sc
            sch_ref[pl.ds(k0, th), :] = _high_half(sc).astype(HALF)
        return carry

    lax.fori_loop(0, n_tiles, score_tile, 0)

    def count(pred):
        def body(t, acc):
            k0 = pl.multiple_of(t * tk, tk)
            ones = pred(sc_ref[pl.ds(k0, tk), :], k0)
            return acc + jnp.sum(ones.reshape(tk // fold, fold, tq), axis=0)
        acc = lax.fori_loop(0, n_tiles, body, jnp.zeros((fold, tq), F32))
        return jnp.sum(acc, axis=0, keepdims=True)

    def count_high(thr):
        thr_h = _high_half(thr).astype(HALF)
        one, zero = jnp.ones((), HALF), jnp.zeros((), HALF)

        def body(t, acc):
            ones = jnp.where(sch_ref[pl.ds(pl.multiple_of(t * tk, tk), tk), :] >= thr_h, one, zero)
            for c in range(tk // fold_h):
                acc = acc + ones[c * fold_h:(c + 1) * fold_h]
            return acc
        acc = lax.fori_loop(0, n_tiles, body, jnp.zeros((fold_h, tq), HALF))
        return jnp.sum(acc.astype(F32), axis=0, keepdims=True)

    def bisect(counter):
        def step(it, u):
            trial = u | jnp.left_shift(jnp.int32(1), 31 - it)
            return jnp.where(counter(_key_to_float(trial)) >= kf, trial, u)
        return step

    u = lax.fori_loop(0, 16, bisect(count_high), jnp.zeros((1, tq), I32))
    u = lax.fori_loop(16, 32, bisect(lambda thr: count(lambda blk, k0: jnp.where(blk >= thr, 1.0, 0.0))), u)
    key = jnp.maximum(u ^ INT_MIN, SKEY_LOWEST_FINITE)
    thr = _skey_to_float(key)
    n_ge = count(lambda blk, k0: jnp.where(blk >= thr, 1.0, 0.0))
    tied = jnp.max(n_ge) > kf

    def write_bias(sel):
        def body(t, carry):
            k0 = pl.multiple_of(t * tk, tk)
            sc_ref[pl.ds(k0, tk), :] = sel(sc_ref[pl.ds(k0, tk), :], k0)
            return carry
        lax.fori_loop(0, n_tiles, body, 0)

    @pl.when(jnp.logical_not(tied))
    def _():
        write_bias(lambda blk, k0: jnp.where(blk >= thr, 0.0, NEG_INF))

    @pl.when(tied)
    def _():
        need = kf - count(lambda blk, k0: jnp.where(blk > thr, 1.0, 0.0))
        krow = lax.broadcasted_iota(I32, (tk, tq), 0)

        def jbis(it, ans):
            trial = ans + jnp.left_shift(jnp.int32(1), idx_bits - 1 - it)
            cnt = count(lambda blk, k0: jnp.where(blk == thr, jnp.where(k0 + krow < trial, 1.0, 0.0), 0.0))
            return jnp.where(cnt < need, trial, ans)

        ans = lax.fori_loop(0, idx_bits, jbis, jnp.zeros((1, tq), I32))
        write_bias(lambda blk, k0: jnp.where(
            blk > thr, 0.0, jnp.where(blk == thr, jnp.where(k0 + krow <= ans, 0.0, NEG_INF), NEG_INF)))

    m_ref[...] = jnp.full(m_ref.shape, M_INIT, F32)
    acc_ref[...] = jnp.zeros(acc_ref.shape, F32)

    mu_ref[...] = jnp.full(mu_ref.shape, M_INIT, F32)

    def logits(t, buf, g, m_before):
        k0 = pl.multiple_of(t * tk, tk)
        bias = sc_ref[pl.ds(k0, tk), :]
        qg = q_ref[0, g * group:(g + 1) * group].reshape(group * tq, HEAD_DIM)
        s = lax.dot_general(k_ref[0, g, pl.ds(k0, tk), :], qg, (((1,), (1,)), ((), ())),
                            preferred_element_type=F32)
        s = jnp.concatenate([s[:, r * tq:(r + 1) * tq] + bias for r in range(group)], axis=1)
        buf[g] = s
        return jnp.maximum(m_before, jnp.max(s, axis=0, keepdims=True))

    def values(t, buf, g, m_t):
        p = jnp.exp2(buf[g] - m_t)
        pv = jnp.dot(vt_ref[0, g, :, pl.ds(pl.multiple_of(t * tk, tk), tk)], p.astype(BF16),
                     preferred_element_type=F32)
        acc_ref[g] = jnp.exp2(mu_ref[g] - m_t) * acc_ref[g] + pv
        mu_ref[g] = m_t

    def step(t_next, buf_next, t, buf):
        for g in range(N_KV_HEADS):
            m_t = m_ref[g]
            m_ref[g] = logits(t_next, buf_next, g, m_t)
            values(t, buf, g, m_t)

    for g in range(N_KV_HEADS):
        m_ref[g] = logits(0, s_ref, g, m_ref[g])

    def tile_pair(j, carry):
        step(2 * j + 1, s2_ref, 2 * j, s_ref)
        step(jnp.minimum(2 * j + 2, n_tiles - 1), s_ref, 2 * j + 1, s2_ref)
        return carry

    lax.fori_loop(0, n_tiles >> 1, tile_pair, 0)

    @pl.when((n_tiles & 1) == 1)
    def _():
        for g in range(N_KV_HEADS):
            values(n_tiles - 1, s_ref, g, m_ref[g])

    for g in range(N_KV_HEADS):
        acc = acc_ref[g]
        o = acc[:HEAD_DIM, :] / acc[HEAD_DIM:HEAD_DIM + 1, :]
        for r in range(0, group, 2):
            pair = jnp.concatenate([o[:, r * tq:(r + 1) * tq], o[:, (r + 1) * tq:(r + 2) * tq]], axis=0)
            h0 = g * group + r
            o_ref[:, h0 * HEAD_DIM:(h0 + 2) * HEAD_DIM] = pair.T.astype(BF16)


def _dsa_t(qh, qih, wt, kh, vt, kih, *, tk, k_sel, n_valid_end):
    nb, _, s, _ = qh.shape
    lp = kh.shape[2]
    tq = LANES
    nq = s // tq
    group = N_HEADS // N_KV_HEADS
    kern = functools.partial(_dsa_t_kernel, tk=tk, k_sel=k_sel, n_valid_end=n_valid_end,
                             idx_bits=max(1, int(lp).bit_length()))
    per_batch = lambda shp: pl.BlockSpec((1,) + shp, lambda b, i: (b,) + (0,) * len(shp), pipeline_mode=pl.Buffered(1))
    return pl.pallas_call(
        kern,
        out_shape=jax.ShapeDtypeStruct((nb * s, N_HEADS * HEAD_DIM), BF16),
        grid=(nb, nq),
        in_specs=[pl.BlockSpec((1, N_HEADS, tq, HEAD_DIM), lambda b, i: (b, 0, i, 0)),
                  pl.BlockSpec((1, IDX_HEADS, tq, IDX_DIM), lambda b, i: (b, 0, i, 0)),
                  pl.BlockSpec((1, IDX_HEADS, tq), lambda b, i: (b, 0, i)),
                  per_batch((N_KV_HEADS, lp, HEAD_DIM)), per_batch((N_KV_HEADS, VT_ROWS, lp)), per_batch((lp, IDX_DIM))],
        out_specs=pl.BlockSpec((tq, N_HEADS * HEAD_DIM), lambda b, i: (b * nq + i, 0)),
        scratch_shapes=[pltpu.VMEM((lp, tq), F32), pltpu.VMEM((lp, tq), HALF),
                        pltpu.VMEM((N_KV_HEADS, 1, group * tq), F32),
                        pltpu.VMEM((N_KV_HEADS, 1, group * tq), F32),
                        pltpu.VMEM((N_KV_HEADS, VT_ROWS, group * tq), F32),
                        pltpu.VMEM((N_KV_HEADS, tk, group * tq), F32),
                        pltpu.VMEM((N_KV_HEADS, tk, group * tq), F32)],
        compiler_params=_cparams(("arbitrary", "arbitrary")),
        name="dsa_t",
    )(qh, qih, wt, kh, vt, kih)


def _pad_keys(meta_rows, mid_rows, lp):
    nb = mid_rows[0].shape[0]
    tail = mid_rows[0].shape[2:]
    meta_b = jnp.broadcast_to(meta_rows[None], (nb,) + meta_rows.shape)
    used = FRONT + sum(m.shape[1] for m in mid_rows)
    parts = [jnp.zeros((nb, FRONT - N_META) + tail, meta_rows.dtype), meta_b] + list(mid_rows)
    if lp > used:
        parts.append(jnp.zeros((nb, lp - used) + tail, meta_rows.dtype))
    return jnp.concatenate(parts, axis=1)


def _attention_inputs(k_meta, v_meta, ki_meta, k_mid, v_mid, ki_mid, lp, keys_on_lanes):
    nb = k_mid[0].shape[0]
    kp = _pad_keys(k_meta, k_mid, lp).astype(BF16).reshape(nb, lp, N_KV_HEADS, HEAD_DIM).transpose(0, 2, 1, 3)
    vp = _pad_keys(v_meta, v_mid, lp).astype(BF16).reshape(nb, lp, N_KV_HEADS, HEAD_DIM)
    if keys_on_lanes:
        vp = vp.transpose(0, 2, 3, 1)
        ones = jnp.ones((nb, N_KV_HEADS, 1, lp), BF16)
        vp = jnp.concatenate([vp, ones, jnp.zeros((nb, N_KV_HEADS, VT_ROWS - HEAD_DIM - 1, lp), BF16)], axis=2)
    else:
        vp = vp.transpose(0, 2, 1, 3)
        ones = jnp.ones(vp.shape[:3] + (1,), BF16)
        vp = jnp.concatenate([vp, ones, jnp.zeros(vp.shape[:3] + (LANES - HEAD_DIM - 1,), BF16)], axis=3)
    kip = _pad_keys(ki_meta, ki_mid, lp).astype(BF16)
    return kp, vp, kip


def _merge_kernel(xp_ref, xs_ref, cp_ref, cs_ref, ap_ref, as_ref, gp_ref, gs_ref, wao_ref, wout_ref, gffn_ref, wr_ref,
                  br_ref, x1_ref, hn_ref, ids_ref, gts_ref, *, n_first):
    first = pl.program_id(0) < n_first
    x = jnp.where(first, xp_ref[...], xs_ref[...])
    conv = jnp.where(first, cp_ref[...], cs_ref[...]).astype(F32)
    attn = jnp.where(first, ap_ref[...], as_ref[...])
    ao = jnp.dot(attn, wao_ref[...], preferred_element_type=F32)
    gates = jnp.where(first, gp_ref[...], gs_ref[...]).astype(F32)
    mix = gates[:, :D_MODEL] * conv + gates[:, D_MODEL:] * ao
    x1 = x + jnp.dot(mix.astype(BF16), wout_ref[...], preferred_element_type=F32)
    x1_ref[...] = x1
    hn = x1 * lax.rsqrt(jnp.mean(x1 * x1, axis=-1, keepdims=True) + EPS) * gffn_ref[...]
    tr = hn.shape[0]
    for c in range(SUB):
        hn_ref[pl.ds(c, tr, stride=SUB), :] = hn[:, c * LANES:(c + 1) * LANES]
    logits = jnp.dot(hn, wr_ref[...], preferred_element_type=F32, precision=lax.Precision.HIGHEST) + br_ref[...]
    lane = lax.broadcasted_iota(I32, logits.shape, 1)
    lg = jnp.where(lane < N_EXPERTS, logits, NEG_INF)
    vals, ids = [], []
    for _ in range(TOP_K_EXPERTS):
        m = jnp.max(lg, axis=1, keepdims=True)
        idx = jnp.min(jnp.where(lg == m, lane, LANES), axis=1, keepdims=True)
        vals.append(m)
        ids.append(idx)
        lg = jnp.where(lane == idx, NEG_INF, lg)
    es = [jnp.exp(v - vals[0]) for v in vals]
    den = es[0]
    for e in es[1:]:
        den = den + e
    ids_out = jnp.zeros(logits.shape, I32)
    gts_out = jnp.zeros(logits.shape, F32)
    for j in range(TOP_K_EXPERTS):
        ids_out = jnp.where(lane == j, ids[j], ids_out)
        gts_out = jnp.where(lane == j, es[j] / den, gts_out)
    ids_ref[...] = ids_out
    gts_ref[...] = gts_out


def _merge(xp, xs, conv_p, conv_s, attn_p, attn_s, gates_p, gates_s, wao, wout, gffn, wr, br, tr):
    n_first, n_second = xp.shape[0] // tr, xs.shape[0] // tr
    rows = xp.shape[0] + xs.shape[0]
    first = lambda w: pl.BlockSpec((tr, w), lambda i: (jnp.minimum(i, n_first - 1), 0))
    second = lambda w: pl.BlockSpec((tr, w), lambda i: (jnp.maximum(i - n_first, 0), 0))
    row = lambda w: pl.BlockSpec((tr, w), lambda i: (i, 0))
    return pl.pallas_call(
        functools.partial(_merge_kernel, n_first=n_first),
        out_shape=[jax.ShapeDtypeStruct((rows, D_MODEL), F32), jax.ShapeDtypeStruct((rows * SUB, LANES), F32),
                   jax.ShapeDtypeStruct((rows, LANES), I32), jax.ShapeDtypeStruct((rows, LANES), F32)],
        grid=(n_first + n_second,),
        in_specs=[first(D_MODEL), second(D_MODEL), first(D_MODEL), second(D_MODEL), first(D_MODEL), second(D_MODEL),
                  first(2 * D_MODEL), second(2 * D_MODEL), _resident(wao), _resident(wout), _resident(gffn),
                  _resident(wr), _resident(br)],
        out_specs=[row(D_MODEL), pl.BlockSpec((tr * SUB, LANES), lambda i: (i, 0)), row(LANES), row(LANES)],
        compiler_params=_cparams(("arbitrary",)),
        name="merge",
    )(xp, xs, conv_p, conv_s, attn_p, attn_s, gates_p, gates_s, wao, wout, gffn, wr, br)


KEY_ALIGN = 1024


def _key_window(tm):
    return -(-(tm + KEY_ALIGN - 1) // KEY_ALIGN) * KEY_ALIGN


def _for_rows(n, fn):
    def group(gi, carry):
        for u in range(SUB):
            fn(gi * SUB + u)
        return carry

    def single(r, carry):
        fn(r)
        return carry

    shift = SUB.bit_length() - 1
    lax.fori_loop(0, n >> shift, group, 0)
    lax.fori_loop((n >> shift) << shift, n, single, 0)


def _moe_kernel(te_ref, base_ref, nval_ref, nreal_ref, keys_hbm, h_hbm, w1_ref, b1_ref, w3_ref, b3_ref, w2_ref, b2_ref,
                y_hbm, key_sm, xbuf, xb, ybuf, wb, ksem, gsem, ssem, *, tm, n_tok, pair_bits):
    i = pl.program_id(0)
    nreal = nreal_ref[0]
    window = key_sm.shape[0] // 3
    slot_shift = TOP_K_EXPERTS.bit_length() - 1

    def keys_copy(tile):
        s = lax.rem(tile, 3)
        start = pl.multiple_of(base_ref[tile] & -KEY_ALIGN, KEY_ALIGN)
        return pltpu.make_async_copy(keys_hbm.at[pl.ds(start, window)],
                                     key_sm.at[pl.ds(pl.multiple_of(s * window, KEY_ALIGN), window)], ksem.at[s])

    def pair_reader(tile):
        first = lax.rem(tile, 3) * window + (base_ref[tile] & (KEY_ALIGN - 1))
        return lambda r: key_sm[first + r] & ((1 << pair_bits) - 1)

    def gather_row(s, r, token):
        return pltpu.make_async_copy(h_hbm.at[pl.ds(token * SUB, SUB)], xbuf.at[s, pl.ds(r * SUB, SUB)], gsem.at[s])

    def scatter_row(r, out_token):
        return pltpu.make_async_copy(ybuf.at[pl.ds(r * SUB, SUB)], y_hbm.at[pl.ds(out_token * SUB, SUB)], ssem.at[0])

    def start_gather(tile):
        pair_of, s = pair_reader(tile), tile & 1
        _for_rows(tm, lambda r: gather_row(s, r, pair_of(r) >> slot_shift).start())

    @pl.when(jnp.logical_and(i == 0, nreal > 0))
    def _():
        keys_copy(0).start()

        @pl.when(nreal > 1)
        def _():
            keys_copy(1).start()

        keys_copy(0).wait()
        start_gather(0)

    @pl.when(i + 1 < nreal)
    def _():
        keys_copy(i + 1).wait()
        start_gather(i + 1)

    @pl.when(i + 2 < nreal)
    def _():
        keys_copy(i + 2).start()

    @pl.when(i < nreal)
    def _():
        s = i & 1
        _for_rows(tm, lambda r: gather_row(s, r, 0).wait())
        for c in range(SUB):
            xb[:, c * LANES:(c + 1) * LANES] = xbuf[s, pl.ds(c, tm, stride=SUB), :].astype(BF16)
        @pl.when(jnp.logical_or(i == 0, te_ref[i] != te_ref[jnp.maximum(i - 1, 0)]))
        def _():
            for j, w_ref in enumerate((w1_ref, w3_ref, w2_ref)):
                wb[j] = w_ref[0].astype(BF16)

        x = xb[...]
        g = jnp.minimum(jnp.dot(x, wb[0], preferred_element_type=F32) + b1_ref[0], SWIGLU_LIMIT)
        u = jnp.clip(jnp.dot(x, wb[1], preferred_element_type=F32) + b3_ref[0], -SWIGLU_LIMIT, SWIGLU_LIMIT)
        act = g * jax.nn.sigmoid(SWIGLU_ALPHA * g) * (u + 1.0)
        y = jnp.dot(act.astype(BF16), wb[2], preferred_element_type=F32) + b2_ref[0]

        @pl.when(i > 0)
        def _():
            _for_rows(nval_ref[i - 1], lambda r: scatter_row(r, 0).wait())

        for c in range(SUB):
            ybuf[pl.ds(c, tm, stride=SUB), :] = y[:, c * LANES:(c + 1) * LANES]

        pair_of = pair_reader(i)

        def put(r):
            pair = pair_of(r)
            scatter_row(r, (pair & (TOP_K_EXPERTS - 1)) * n_tok + (pair >> slot_shift)).start()

        _for_rows(nval_ref[i], put)

        @pl.when(i == nreal - 1)
        def _():
            _for_rows(nval_ref[i], lambda r: scatter_row(r, 0).wait())


def _moe(hn, keys, tile_expert, base, nval, n_real, w1, b1, w3, b3, w2, b2, n_tok, pair_bits, tm):
    nt = tile_expert.shape[0]
    by_expert = lambda i, te, *_: (te[i], 0, 0)
    wspec = pl.BlockSpec((1, D_MODEL, D_FF), by_expert)
    w2spec = pl.BlockSpec((1, D_FF, D_MODEL), by_expert)
    bspec = pl.BlockSpec((1, 1, D_FF), by_expert)
    any_spec = pl.BlockSpec(memory_space=pl.ANY)
    grid_spec = pltpu.PrefetchScalarGridSpec(
        num_scalar_prefetch=4,
        grid=(nt,),
        in_specs=[any_spec, any_spec, wspec, bspec, wspec, bspec, w2spec, bspec],
        out_specs=any_spec,
        scratch_shapes=[pltpu.SMEM((3 * _key_window(tm),), I32),
                        pltpu.VMEM((2, tm * SUB, LANES), F32), pltpu.VMEM((tm, D_MODEL), BF16),
                        pltpu.VMEM((tm * SUB, LANES), F32), pltpu.VMEM((3, D_MODEL, D_FF), BF16),
                        pltpu.SemaphoreType.DMA((3,)), pltpu.SemaphoreType.DMA((2,)), pltpu.SemaphoreType.DMA((1,))],
    )
    return pl.pallas_call(
        functools.partial(_moe_kernel, tm=tm, n_tok=n_tok, pair_bits=pair_bits),
        out_shape=jax.ShapeDtypeStruct((TOP_K_EXPERTS * n_tok * SUB, LANES), F32),
        grid_spec=grid_spec,
        compiler_params=_cparams(("arbitrary",)),
        name="moe",
    )(tile_expert, base, nval, n_real, keys, hn, w1, b1, w3, b3, w2, b2)


def _route(ids, n, tm):
    n_pairs = n * TOP_K_EXPERTS
    pair_bits = max(1, (n_pairs - 1).bit_length())
    assert pair_bits + (N_EXPERTS - 1).bit_length() <= 31 and tm % LANES == 0
    assert TOP_K_EXPERTS & (TOP_K_EXPERTS - 1) == 0
    pair = jnp.arange(n_pairs, dtype=I32)
    keys = jnp.sort(jnp.left_shift(ids.reshape(-1), pair_bits) | pair)
    starts = jnp.searchsorted(keys, jnp.left_shift(jnp.arange(N_EXPERTS + 1, dtype=I32), pair_bits),
                              method="compare_all").astype(I32)
    counts = starts[1:] - starts[:-1]
    tiles = (counts + tm - 1) // tm
    tile_end = jnp.cumsum(tiles)
    nt = -(-(n_pairs + N_EXPERTS * (tm - 1)) // tm)
    ti = jnp.arange(nt, dtype=I32)
    te = jnp.minimum(jnp.searchsorted(tile_end, ti, side="right", method="compare_all"), N_EXPERTS - 1).astype(I32)
    within = ti - (tile_end - tiles)[te]
    base = jnp.clip(starts[:-1][te] + within * tm, 0, n_pairs - 1).astype(I32)
    nval = jnp.clip(counts[te] - within * tm, 0, tm).astype(I32)
    padded = -(-n_pairs // KEY_ALIGN) * KEY_ALIGN + _key_window(tm)
    keys = jnp.pad(keys, (0, padded - n_pairs))
    return keys, te, base, nval, tile_end[-1:].astype(I32), pair_bits


def _final_kernel(x1_ref, y0_ref, y1_ref, y2_ref, y3_ref, gts_ref, g_ref, out_ref):
    gts = gts_ref[...]
    tr = gts.shape[0]
    cols = []
    for c in range(SUB):
        acc = x1_ref[:, c * LANES:(c + 1) * LANES]
        for j, y_ref in enumerate((y0_ref, y1_ref, y2_ref, y3_ref)):
            acc = acc + gts[:, j:j + 1] * y_ref[pl.ds(c, tr, stride=SUB), :]
        cols.append(acc)
    ss = cols[0] * cols[0]
    for acc in cols[1:]:
        ss = ss + acc * acc
    scale = lax.rsqrt(jnp.sum(ss, axis=-1, keepdims=True) / D_MODEL + EPS)
    for c, acc in enumerate(cols):
        out_ref[:, c * LANES:(c + 1) * LANES] = acc * scale * g_ref[:, c * LANES:(c + 1) * LANES]


def _final(x1, yslots, gts, g, tr):
    n = x1.shape[0]
    nblk = n // tr
    row = lambda w: pl.BlockSpec((tr, w), lambda i: (i, 0))
    slot = lambda j: pl.BlockSpec((tr * SUB, LANES), lambda i: (j * nblk + i, 0))
    return pl.pallas_call(
        _final_kernel,
        out_shape=jax.ShapeDtypeStruct((n, D_MODEL), F32),
        grid=(nblk,),
        in_specs=[row(D_MODEL)] + [slot(j) for j in range(TOP_K_EXPERTS)] + [row(LANES), _resident(g)],
        out_specs=row(D_MODEL),
        compiler_params=_cparams(("arbitrary",)),
        name="final",
    )(x1, yslots, yslots, yslots, yslots, gts, g)


def _tile(n, pref):
    t = min(n, pref)
    while n % t:
        t //= 2
    return t


def kernel(x_prompt, x_sample, cache_k, cache_v, cache_idx_k, state_conv, meta_tokens, g_norm_mix, w_in, conv_w,
           conv_b, conv_ln_g, conv_ln_b, w_conv_out, w_attn_out, w_out, g_norm_ffn, w_router, b_router,
           w1, b1, w3, b3, w2, b2, g_norm_final):
    nb, s_p, _ = x_prompt.shape
    db, s_n, _ = x_sample.shape
    past = cache_k.shape[1]
    rows_p, rows_s = nb * s_p, db * s_n
    n = rows_p + rows_s
    tr = _tile(rows_s, ROW_TILE)
    assert rows_p % tr == 0 and s_p % tr == 0 and tr % s_n == 0 and s_p % Q_TILE == 0

    cuts = [0, 2 * CONV_CH, N_HEADS * HEAD_DIM, N_KV_HEADS * HEAD_DIM, N_KV_HEADS * HEAD_DIM, IDX_HEADS * IDX_DIM,
            IDX_DIM + IDX_HEADS, 2 * D_MODEL]
    offs = [sum(cuts[:i + 1]) for i in range(len(cuts))]
    seg = [w_in[:, offs[i]:offs[i + 1]].astype(BF16) for i in range(len(cuts) - 1)]
    seg[5] = jnp.pad(seg[5], ((0, 0), (0, LANES - seg[5].shape[1])))
    g_mix = g_norm_mix.reshape(1, D_MODEL)

    tabs_p = _rope_tables(N_META + jnp.arange(s_p, dtype=I32))
    tabs_s = _rope_tables(jnp.tile(N_META + past + jnp.arange(s_n, dtype=I32), tr // s_n))
    tabs_m = _rope_tables(jnp.arange(N_META, dtype=I32))

    xp, xs = x_prompt.reshape(rows_p, D_MODEL), x_sample.reshape(rows_s, D_MODEL)
    a_p, qh_p, k_p, v_p, qih_p, kw_p, gates_p, kh_mid, vt_mid, ki_mid, wt_p = _project_seq(
        xp, nb, s_p, tabs_p, g_mix, seg, tr)
    a_s, q_s, k_s, v_s, qi_s, kw_s, gates_s = [t[:rows_s] for t in _project(xs, xs, tabs_s, lambda i: 0, g_mix, seg, tr)]
    meta = meta_tokens.astype(F32)
    a_m, _, k_m, v_m, _, kw_m, _ = _project(meta, meta, tabs_m, lambda i: 0, g_mix, seg, N_META)
    a_m, k_m, v_m, ki_m = a_m[:N_META], k_m[:N_META], v_m[:N_META], kw_m[:N_META, :IDX_DIM]

    cw = jnp.pad(conv_w, ((0, HALO - CONV_WIDTH), (0, 0)))
    cb, lg, lb = conv_b.reshape(1, -1), conv_ln_g.reshape(1, -1), conv_ln_b.reshape(1, -1)
    wco = w_conv_out.astype(BF16)
    ctx_p = jnp.concatenate([jnp.zeros((HALO - N_META, CONV_CH), F32), a_m], axis=0)[None]
    ctx_s = jnp.pad(state_conv, ((0, 0), (HALO - (CONV_WIDTH - 1), 0), (0, 0)))
    conv_p = _conv_branch(a_p, 0, nb, s_p, ctx_p, cw, cb, lg, lb, wco, _tile(s_p, 256))
    conv_s = _conv_branch(a_s, 0, db, s_n, ctx_s, cw, cb, lg, lb, wco, s_n)

    k_p3, v_p3 = k_p.reshape(nb, s_p, -1), v_p.reshape(nb, s_p, -1)
    ki_p3 = kw_p[:, :IDX_DIM].reshape(nb, s_p, IDX_DIM)
    k_s3, v_s3 = k_s.reshape(db, s_n, -1), v_s.reshape(db, s_n, -1)
    ki_s3 = kw_s[:, :IDX_DIM].reshape(db, s_n, IDX_DIM)

    tk_p = _tile(s_p, K_TILE_T)
    lp_p = -(-(FRONT + s_p) // tk_p) * tk_p
    front = lambda rows: jnp.pad(rows, ((FRONT - N_META, 0), (0, 0))).astype(BF16)
    per_head = lambda t: t.reshape(FRONT, N_KV_HEADS, HEAD_DIM).transpose(1, 0, 2)
    rep = lambda t: jnp.broadcast_to(t[None], (nb,) + t.shape)
    tail = lp_p - FRONT - s_p
    kp = jnp.concatenate([rep(per_head(front(k_m))), kh_mid, jnp.zeros((nb, N_KV_HEADS, tail, HEAD_DIM), BF16)], axis=2)
    vt_front = jnp.concatenate([per_head(front(v_m)).transpose(0, 2, 1), jnp.ones((N_KV_HEADS, 1, FRONT), BF16),
                                jnp.zeros((N_KV_HEADS, VT_ROWS - HEAD_DIM - 1, FRONT), BF16)], axis=1)
    vtp = jnp.concatenate([rep(vt_front), vt_mid, jnp.zeros((nb, N_KV_HEADS, VT_ROWS, tail), BF16)], axis=3)
    kip = jnp.concatenate([rep(front(ki_m)), ki_mid, jnp.zeros((nb, tail, IDX_DIM), BF16)], axis=1)
    attn_p = _dsa_t(qh_p, qih_p, wt_p, kp, vtp, kip, tk=tk_p, k_sel=min(TOPK_MAX, s_p // 4),
                    n_valid_end=FRONT + s_p)

    def heads(t, b_, s_, nh):
        return t.reshape(b_, s_, nh, -1).transpose(0, 2, 1, 3)

    tk_s = K_TILE
    lp_s = -(-(FRONT + past + s_n) // tk_s) * tk_s
    ck = cache_k.reshape(db, past, -1)
    cv = cache_v.reshape(db, past, -1)
    ks, vs, kis = _attention_inputs(k_m, v_m, ki_m, [ck, k_s3], [cv, v_s3], [cache_idx_k, ki_s3], lp_s, False)
    attn_s = _dsa(heads(q_s, db, s_n, N_HEADS), heads(qi_s, db, s_n, IDX_HEADS), kw_s, 0, ks, vs, kis,
                  tq=s_n, tk=tk_s, k_sel=min(TOPK_MAX, (past + s_n) // 4), q_off=past,
                  n_valid_end=FRONT + past + s_n)

    wr = jnp.pad(w_router, ((0, 0), (0, LANES - N_EXPERTS)))
    br = jnp.pad(b_router, (0, LANES - N_EXPERTS)).reshape(1, LANES)
    x1, hn, ids, gts = _merge(xp, xs, conv_p, conv_s, attn_p, attn_s, gates_p, gates_s, w_attn_out.astype(BF16),
                              w_out.astype(BF16), g_norm_ffn.reshape(1, -1), wr, br, tr)

    tm = MOE_TILE
    keys, tile_expert, base, nval, n_real, pair_bits = _route(ids[:, :TOP_K_EXPERTS], n, tm)
    yslots = _moe(hn, keys, tile_expert, base, nval, n_real, w1, b1.reshape(N_EXPERTS, 1, D_FF),
                  w3, b3.reshape(N_EXPERTS, 1, D_FF), w2, b2.reshape(N_EXPERTS, 1, D_MODEL), n, pair_bits, tm)
    out = _final(x1, yslots, gts, g_norm_final.reshape(1, -1), tr)

    y_prompt = out[:rows_p].reshape(nb, s_p, D_MODEL)
    y_sample = out[rows_p:].reshape(db, s_n, D_MODEL)
    lead = lambda m: jnp.broadcast_to(m[None], (nb,) + m.shape)
    k_prompt = jnp.concatenate([lead(k_m), k_p3], axis=1).reshape(nb, N_META + s_p, N_KV_HEADS, HEAD_DIM)
    v_prompt = jnp.concatenate([lead(v_m), v_p3], axis=1).reshape(nb, N_META + s_p, N_KV_HEADS, HEAD_DIM)
    idxk_prompt = jnp.concatenate([lead(ki_m), ki_p3], axis=1)
    a_p3 = a_p.reshape(nb, s_p, CONV_CH)
    a_s3 = a_s.reshape(db, s_n, CONV_CH)
    keep = CONV_WIDTH - 1
    conv_state_p = jnp.concatenate([lead(a_m), a_p3], axis=1)[:, N_META + s_p - keep:]
    conv_state_s = jnp.concatenate([state_conv, a_s3], axis=1)[:, s_n:]
    return (y_prompt, y_sample, k_prompt, v_prompt, idxk_prompt, conv_state_p,
            k_s3.reshape(db, s_n, N_KV_HEADS, HEAD_DIM), v_s3.reshape(db, s_n, N_KV_HEADS, HEAD_DIM), ki_s3,
            conv_state_s)
```

```python
import functools

import jax
import jax.numpy as jnp
from jax import lax
from jax.experimental import pallas as pl
from jax.experimental.pallas import tpu as pltpu

F32, BF16, I32 = jnp.float32, jnp.bfloat16, jnp.int32
HALF = jnp.bfloat16

D_MODEL = 1024
CHUNK = 64
N_META = 16
CONV_CH = D_MODEL
CONV_WIDTH = 31
N_HEADS = 16
N_KV_HEADS = 4
HEAD_DIM = 64
ROT_DIM = HEAD_DIM // 4
ROPE_THETA = 500000.0
IDX_HEADS = 8
IDX_DIM = 64
TOPK_MAX = 256
N_EXPERTS = 32
TOP_K_EXPERTS = 4
D_FF = D_MODEL
SWIGLU_LIMIT = 7.0
SWIGLU_ALPHA = 1.702
EPS = 1e-5
Q_SCALE = HEAD_DIM ** -0.5 * 1.4426950408889634

LANES = 128
SUB = 8
VT_ROWS = HEAD_DIM + 2 * SUB
FRONT = 128
HALO = 32
VMEM_LIMIT = 56 * 1024 * 1024
NEG_INF = float("-inf")
INT_MIN = -2 ** 31
M_INIT = -1e30

ROW_TILE = 512
Q_TILE = 128
K_TILE = 512
K_TILE_T = 1024
LOW_GROUP = 4
MOE_TILE = 512


def _cparams(sem):
    return pltpu.CompilerParams(dimension_semantics=sem, vmem_limit_bytes=VMEM_LIMIT)


def _resident(arr):
    nd = arr.ndim
    return pl.BlockSpec(arr.shape, lambda *_: (0,) * nd, pipeline_mode=pl.Buffered(1))


def _rope_cols(x, c, s1, s2):
    outs = []
    for j in range(x.shape[1] // LANES):
        xj = x[:, j * LANES:(j + 1) * LANES]
        outs.append(xj * c + pltpu.roll(xj, LANES - ROT_DIM // 2, 1) * s1 + pltpu.roll(xj, ROT_DIM // 2, 1) * s2)
    return outs


def _proj_kernel(xp_ref, xs_ref, g_ref, wa_ref, wq_ref, wk_ref, wv_ref, wqi_ref, wkw_ref, wgl_ref,
                 c_ref, s1_ref, s2_ref,
                 a_ref, q_ref, k_ref, v_ref, qi_ref, kw_ref, gate_ref, *, n_first):
    x = jnp.where(pl.program_id(0) < n_first, xp_ref[...], xs_ref[...])
    h = x * lax.rsqrt(jnp.mean(x * x, axis=-1, keepdims=True) + EPS) * g_ref[...]
    hb = h.astype(BF16)
    c, s1, s2 = c_ref[...], s1_ref[...], s2_ref[...]

    def dot(w_ref):
        return jnp.dot(hb, w_ref[...], preferred_element_type=F32)

    glu = dot(wa_ref)
    a_ref[...] = glu[:, :CONV_CH] * jax.nn.sigmoid(glu[:, CONV_CH:])
    for j, col in enumerate(_rope_cols(dot(wq_ref), c, s1, s2)):
        q_ref[:, j * LANES:(j + 1) * LANES] = (col * Q_SCALE).astype(BF16)
    for j, col in enumerate(_rope_cols(dot(wk_ref), c, s1, s2)):
        k_ref[:, j * LANES:(j + 1) * LANES] = col
    v_ref[...] = dot(wv_ref)
    for j, col in enumerate(_rope_cols(dot(wqi_ref), c, s1, s2)):
        qi_ref[:, j * LANES:(j + 1) * LANES] = (col * IDX_DIM ** -0.5).astype(BF16)
    pkw = dot(wkw_ref)
    lane = lax.broadcasted_iota(I32, pkw.shape, 1)
    kw_ref[...] = jnp.where(lane < IDX_DIM, _rope_cols(pkw, c, s1, s2)[0], pkw * IDX_HEADS ** -0.5)
    gate_ref[...] = jax.nn.sigmoid(dot(wgl_ref)).astype(BF16)


def _rope_tables(pos):
    half = ROT_DIM // 2
    inv = ROPE_THETA ** (-jnp.arange(half, dtype=F32) / half)
    ang = pos.astype(F32)[:, None] * inv[None, :]
    cos, sin = jnp.cos(ang), jnp.sin(ang)
    r = pos.shape[0]
    z = lambda n: jnp.zeros((r, n), F32)
    c = jnp.concatenate([cos, cos, jnp.ones((r, HEAD_DIM - ROT_DIM), F32)], axis=1)
    s1 = jnp.concatenate([-sin, z(HEAD_DIM - half)], axis=1)
    s2 = jnp.concatenate([z(half), sin, z(HEAD_DIM - ROT_DIM)], axis=1)
    two = lambda t: jnp.concatenate([t, t], axis=1)
    return two(c), two(s1), two(s2)


def _project(xp, xs, tabs, tab_map, g, ws, tr):
    n_first, n_second = xp.shape[0] // tr, xs.shape[0] // tr
    rows = xp.shape[0] + xs.shape[0]
    row = lambda w: pl.BlockSpec((tr, w), lambda i: (i, 0))
    tab = pl.BlockSpec((tr, LANES), lambda i: (tab_map(i), 0))
    in_specs = [pl.BlockSpec((tr, D_MODEL), lambda i: (jnp.minimum(i, n_first - 1), 0)),
                pl.BlockSpec((tr, D_MODEL), lambda i: (jnp.maximum(i - n_first, 0), 0)),
                _resident(g)] + [_resident(w) for w in ws] + [tab, tab, tab]
    widths = [(CONV_CH, F32), (N_HEADS * HEAD_DIM, BF16), (N_KV_HEADS * HEAD_DIM, F32), (N_KV_HEADS * HEAD_DIM, F32),
              (IDX_HEADS * IDX_DIM, BF16), (LANES, F32), (2 * D_MODEL, BF16)]
    return pl.pallas_call(
        functools.partial(_proj_kernel, n_first=n_first),
        out_shape=[jax.ShapeDtypeStruct((rows, w), dt) for w, dt in widths],
        grid=(n_first + n_second,),
        in_specs=in_specs,
        out_specs=[row(w) for w, _ in widths],
        compiler_params=_cparams(("arbitrary",)),
        name="proj",
    )(xp, xs, g, *ws, *tabs)


def _proj_seq_kernel(x_ref, g_ref, wa_ref, wq_ref, wk_ref, wv_ref, wqi_ref, wkw_ref, wgl_ref, c_ref, s1_ref, s2_ref,
                     a_ref, q_ref, k_ref, v_ref, qi_ref, kw_ref, gate_ref, kh_ref, vt_ref, kih_ref, wt_ref):
    x = x_ref[...]
    h = x * lax.rsqrt(jnp.mean(x * x, axis=-1, keepdims=True) + EPS) * g_ref[...]
    hb = h.astype(BF16)
    c, s1, s2 = c_ref[...], s1_ref[...], s2_ref[...]

    def dot(w_ref):
        return jnp.dot(hb, w_ref[...], preferred_element_type=F32)

    def put_heads(ref, j, col):
        ref[0, 2 * j] = col[:, :HEAD_DIM].astype(BF16)
        ref[0, 2 * j + 1] = col[:, HEAD_DIM:].astype(BF16)

    glu = dot(wa_ref)
    a_ref[...] = glu[:, :CONV_CH] * jax.nn.sigmoid(glu[:, CONV_CH:])
    for j, col in enumerate(_rope_cols(dot(wq_ref), c, s1, s2)):
        put_heads(q_ref, j, col * Q_SCALE)
    for j, col in enumerate(_rope_cols(dot(wk_ref), c, s1, s2)):
        k_ref[:, j * LANES:(j + 1) * LANES] = col
        put_heads(kh_ref, j, col)
    v = dot(wv_ref)
    v_ref[...] = v
    tr = v.shape[0]
    ones_row = jnp.where(lax.broadcasted_iota(I32, (VT_ROWS - HEAD_DIM, tr), 0) == 0, 1.0, 0.0).astype(BF16)
    for j in range(v.shape[1] // LANES):
        vt = v[:, j * LANES:(j + 1) * LANES].T
        for half in range(2):
            vt_ref[0, 2 * j + half, 0:HEAD_DIM, :] = vt[half * HEAD_DIM:(half + 1) * HEAD_DIM].astype(BF16)
            vt_ref[0, 2 * j + half, HEAD_DIM:VT_ROWS, :] = ones_row
    for j, col in enumerate(_rope_cols(dot(wqi_ref), c, s1, s2)):
        put_heads(qi_ref, j, col * IDX_DIM ** -0.5)
    pkw = dot(wkw_ref)
    lane = lax.broadcasted_iota(I32, pkw.shape, 1)
    kw = jnp.where(lane < IDX_DIM, _rope_cols(pkw, c, s1, s2)[0], pkw * IDX_HEADS ** -0.5)
    kw_ref[...] = kw
    kih_ref[0] = kw[:, :IDX_DIM].astype(BF16)
    wt_ref[0] = kw.T[IDX_DIM:IDX_DIM + IDX_HEADS, :]
    gate_ref[...] = jax.nn.sigmoid(dot(wgl_ref)).astype(BF16)


def _project_seq(x, nb, s, tabs, g, ws, tr):
    nt = s // tr
    rows = nb * s
    flat = lambda w: pl.BlockSpec((tr, w), lambda b, i: (b * nt + i, 0))
    tab = pl.BlockSpec((tr, LANES), lambda b, i: (i, 0))
    heads = lambda nh: pl.BlockSpec((1, nh, tr, HEAD_DIM), lambda b, i: (b, 0, i, 0))
    widths = [(CONV_CH, F32), None, (N_KV_HEADS * HEAD_DIM, F32), (N_KV_HEADS * HEAD_DIM, F32), None, (LANES, F32),
              (2 * D_MODEL, BF16)]
    shapes = [jax.ShapeDtypeStruct((rows, w[0]), w[1]) if w else None for w in widths]
    specs = [flat(w[0]) if w else None for w in widths]
    shapes[1], specs[1] = jax.ShapeDtypeStruct((nb, N_HEADS, s, HEAD_DIM), BF16), heads(N_HEADS)
    shapes[4], specs[4] = jax.ShapeDtypeStruct((nb, IDX_HEADS, s, IDX_DIM), BF16), heads(IDX_HEADS)
    shapes += [jax.ShapeDtypeStruct((nb, N_KV_HEADS, s, HEAD_DIM), BF16),
               jax.ShapeDtypeStruct((nb, N_KV_HEADS, VT_ROWS, s), BF16),
               jax.ShapeDtypeStruct((nb, s, IDX_DIM), BF16),
               jax.ShapeDtypeStruct((nb, IDX_HEADS, s), F32)]
    specs += [heads(N_KV_HEADS),
              pl.BlockSpec((1, N_KV_HEADS, VT_ROWS, tr), lambda b, i: (b, 0, 0, i)),
              pl.BlockSpec((1, tr, IDX_DIM), lambda b, i: (b, i, 0)),
              pl.BlockSpec((1, IDX_HEADS, tr), lambda b, i: (b, 0, i))]
    return pl.pallas_call(
        _proj_seq_kernel,
        out_shape=shapes,
        grid=(nb, nt),
        in_specs=[flat(D_MODEL), _resident(g)] + [_resident(w) for w in ws] + [tab, tab, tab],
        out_specs=specs,
        compiler_params=_cparams(("arbitrary", "arbitrary")),
        name="proj_seq",
    )(x, g, *ws, *tabs)


def _conv_kernel(cur_ref, halo_ref, lc_ref, cw_ref, cb_ref, lg_ref, lb_ref, wo_ref, out_ref, e_ref, c_ref, *, tr, rc):
    halo = jnp.where(pl.program_id(1) == 0, lc_ref[0], halo_ref[...])
    e_ref[0, 0:HALO, :] = halo
    e_ref[0, HALO:HALO + tr, :] = cur_ref[...]
    n = tr + HALO - 8
    for b in range(1, 8):
        e_ref[b, 0:n, :] = e_ref[0, b:b + n, :]
    first = HALO - (CONV_WIDTH - 1)

    def chunk(ci, carry):
        r0 = pl.multiple_of(ci * rc, rc)
        acc = jnp.zeros((rc, CONV_CH), F32) + cb_ref[...]
        for w in range(CONV_WIDTH):
            s = first + w
            acc = acc + e_ref[s % 8, pl.ds(r0 + 8 * (s // 8), rc), :] * cw_ref[pl.ds(w, 1), :]
        c_ref[pl.ds(r0, rc), :] = acc
        return carry

    lax.fori_loop(0, tr // rc, chunk, 0)
    c = c_ref[...]
    mu = jnp.mean(c, axis=-1, keepdims=True)
    xc = c - mu
    y = xc * lax.rsqrt(jnp.mean(xc * xc, axis=-1, keepdims=True) + EPS) * lg_ref[...] + lb_ref[...]
    act = (y * jax.nn.sigmoid(y)).astype(BF16)
    out_ref[...] = jnp.dot(act, wo_ref[...], preferred_element_type=F32).astype(BF16)


def _conv_branch(a_flat, row_off, nb, t, left_ctx, cw, cb, lg, lb, wo, tr):
    nt = t // tr
    rc = min(tr, 32)
    per_batch_ctx = left_ctx.shape[0] > 1
    cur = pl.BlockSpec((tr, CONV_CH), lambda b, i: (row_off // tr + b * nt + i, 0))
    halo = pl.BlockSpec((HALO, CONV_CH),
                        lambda b, i: (jnp.maximum((row_off + b * t + i * tr) // HALO - 1, 0), 0))
    lctx = pl.BlockSpec((1, HALO, CONV_CH), lambda b, i: (b if per_batch_ctx else 0, 0, 0))
    return pl.pallas_call(
        functools.partial(_conv_kernel, tr=tr, rc=rc),
        out_shape=jax.ShapeDtypeStruct((nb * t, D_MODEL), BF16),
        grid=(nb, nt),
        in_specs=[cur, halo, lctx, _resident(cw), _resident(cb), _resident(lg), _resident(lb), _resident(wo)],
        out_specs=pl.BlockSpec((tr, D_MODEL), lambda b, i: (b * nt + i, 0)),
        scratch_shapes=[pltpu.VMEM((8, tr + HALO, CONV_CH), F32), pltpu.VMEM((tr, CONV_CH), F32)],
        compiler_params=_cparams(("arbitrary", "arbitrary")),
        name="conv",
    )(a_flat, a_flat, left_ctx, cw, cb, lg, lb, wo)


def _float_to_skey(f):
    bits = lax.bitcast_convert_type(f, I32)
    return bits ^ ((bits >> 31) & 0x7FFFFFFF)


def _skey_to_float(key):
    return lax.bitcast_convert_type(key ^ ((key >> 31) & 0x7FFFFFFF), F32)


SKEY_LOWEST_FINITE = INT_MIN + 0x00800000


def _high_half(f):
    return lax.bitcast_convert_type(lax.bitcast_convert_type(f, I32) & -65536, F32)


def _key_to_float(u):
    key = u ^ INT_MIN
    bits = key ^ ((key >> 31) & 0x7FFFFFFF)
    return lax.bitcast_convert_type(bits, F32)


def _dsa_kernel(q_ref, qi_ref, kw_ref, k_ref, v_ref, ki_ref, o_ref, sc_ref, m_ref, acc_ref,
                *, tq, tk, k_sel, q_off, n_valid_end, idx_bits):
    i = pl.program_id(1)
    qf0 = q_off + i * tq
    klim = jnp.minimum(FRONT + ((qf0 + tq - 1) // CHUNK + 1) * CHUNK, n_valid_end)
    n_tiles = (klim + tk - 1) // tk
    ncol = tk // LANES
    kf = float(k_sel)
    group = N_HEADS // N_KV_HEADS

    kwv = kw_ref[...]
    qrow = lax.broadcasted_iota(I32, (tq, 1), 0)
    lim_row = jnp.minimum(FRONT + ((qf0 + qrow) // CHUNK + 1) * CHUNK, n_valid_end)

    def score_tile(t, carry):
        k0 = pl.multiple_of(t * tk, tk)
        kt = ki_ref[0, pl.ds(k0, tk), :]
        sc = jnp.zeros((tq, tk), F32)
        for h in range(IDX_HEADS):
            rel = lax.dot_general(qi_ref[0, h], kt, (((1,), (1,)), ((), ())), preferred_element_type=F32)
            sc = sc + kwv[:, IDX_DIM + h:IDX_DIM + h + 1] * jnp.maximum(rel, 0.0)
        j = k0 + lax.broadcasted_iota(I32, (tq, tk), 1)
        j = jnp.where(j >= FRONT - N_META, j, n_valid_end)
        sc_ref[:, pl.ds(k0, tk)] = jnp.where(j < lim_row, sc, NEG_INF)
        return carry

    lax.fori_loop(0, n_tiles, score_tile, 0)

    lane = lax.broadcasted_iota(I32, (tq, LANES), 1)

    def count(pred):
        def body(t, acc):
            k0 = pl.multiple_of(t * tk, tk)
            for c in range(ncol):
                blk = sc_ref[:, pl.ds(k0 + c * LANES, LANES)]
                acc = acc + pred(blk, k0 + c * LANES)
            return acc
        acc = lax.fori_loop(0, n_tiles, body, jnp.zeros((tq, LANES), F32))
        return jnp.broadcast_to(jnp.sum(acc, axis=1, keepdims=True), (tq, LANES))

    def bisect(it, u):
        trial = u | jnp.left_shift(jnp.int32(1), 31 - it)
        thr = _key_to_float(trial)
        cnt = count(lambda blk, j0: jnp.where(blk >= thr, 1.0, 0.0))
        return jnp.where(cnt >= kf, trial, u)

    u = lax.fori_loop(0, 32, bisect, jnp.zeros((tq, LANES), I32))
    key = jnp.maximum(u ^ INT_MIN, INT_MIN + 0x00800000)
    thr = _key_to_float(key ^ INT_MIN)
    n_ge = count(lambda blk, j0: jnp.where(blk >= thr, 1.0, 0.0))
    tied = jnp.max(n_ge) > kf

    def write_bias(sel):
        def body(t, carry):
            k0 = pl.multiple_of(t * tk, tk)
            for c in range(ncol):
                blk = sc_ref[:, pl.ds(k0 + c * LANES, LANES)]
                sc_ref[:, pl.ds(k0 + c * LANES, LANES)] = sel(blk, k0 + c * LANES)
            return carry
        lax.fori_loop(0, n_tiles, body, 0)

    @pl.when(jnp.logical_not(tied))
    def _():
        write_bias(lambda blk, j0: jnp.where(blk >= thr, 0.0, NEG_INF))

    @pl.when(tied)
    def _():
        need = kf - count(lambda blk, j0: jnp.where(blk > thr, 1.0, 0.0))

        def jbis(it, ans):
            trial = ans + jnp.left_shift(jnp.int32(1), idx_bits - 1 - it)
            cnt = count(lambda blk, j0: jnp.where(blk == thr, jnp.where(j0 + lane < trial, 1.0, 0.0), 0.0))
            return jnp.where(cnt < need, trial, ans)

        ans = lax.fori_loop(0, idx_bits, jbis, jnp.zeros((tq, LANES), I32))
        write_bias(lambda blk, j0: jnp.where(
            blk > thr, 0.0, jnp.where(blk == thr, jnp.where(j0 + lane <= ans, 0.0, NEG_INF), NEG_INF)))

    m_ref[...] = jnp.full(m_ref.shape, M_INIT, F32)
    acc_ref[...] = jnp.zeros(acc_ref.shape, F32)

    def attend_tile(t, carry):
        k0 = pl.multiple_of(t * tk, tk)
        bias = sc_ref[:, pl.ds(k0, tk)]
        for g in range(N_KV_HEADS):
            qg = q_ref[0, g * group:(g + 1) * group].reshape(group * tq, HEAD_DIM)
            s = lax.dot_general(qg, k_ref[0, g, pl.ds(k0, tk), :], (((1,), (1,)), ((), ())),
                                preferred_element_type=F32)
            s = (s.reshape(group, tq, tk) + bias[None]).reshape(group * tq, tk)
            m_prev = m_ref[g]
            m_new = jnp.maximum(m_prev, jnp.max(s, axis=1, keepdims=True))
            p = jnp.exp2(s - jnp.concatenate([m_new] * ncol, axis=1))
            pv = jnp.dot(p.astype(BF16), v_ref[0, g, pl.ds(k0, tk), :], preferred_element_type=F32)
            acc_ref[g] = jnp.exp2(m_prev - m_new) * acc_ref[g] + pv
            m_ref[g] = m_new
        return carry

    lax.fori_loop(0, n_tiles, attend_tile, 0)

    for g in range(N_KV_HEADS):
        acc = acc_ref[g]
        o = acc[:, :HEAD_DIM] / acc[:, HEAD_DIM:HEAD_DIM + 1]
        for r in range(0, group, 2):
            pair = jnp.concatenate([o[r * tq:(r + 1) * tq], o[(r + 1) * tq:(r + 2) * tq]], axis=1)
            h0 = g * group + r
            o_ref[:, h0 * HEAD_DIM:(h0 + 2) * HEAD_DIM] = pair.astype(BF16)


def _dsa(qh, qih, kw_flat, kw_row_off, kh, vh, kih, *, tq, tk, k_sel, q_off, n_valid_end):
    nb, _, s, _ = qh.shape
    lp = kh.shape[2]
    nq = s // tq
    kern = functools.partial(_dsa_kernel, tq=tq, tk=tk, k_sel=k_sel, q_off=q_off, n_valid_end=n_valid_end,
                             idx_bits=max(1, int(lp).bit_length()))
    per_batch = lambda shp: pl.BlockSpec((1,) + shp, lambda b, i: (b,) + (0,) * len(shp), pipeline_mode=pl.Buffered(1))
    return pl.pallas_call(
        kern,
        out_shape=jax.ShapeDtypeStruct((nb * s, N_HEADS * HEAD_DIM), BF16),
        grid=(nb, nq),
        in_specs=[pl.BlockSpec((1, N_HEADS, tq, HEAD_DIM), lambda b, i: (b, 0, i, 0)),
                  pl.BlockSpec((1, IDX_HEADS, tq, IDX_DIM), lambda b, i: (b, 0, i, 0)),
                  pl.BlockSpec((tq, LANES), lambda b, i: (kw_row_off // tq + b * nq + i, 0)),
                  per_batch((N_KV_HEADS, lp, HEAD_DIM)), per_batch((N_KV_HEADS, lp, LANES)), per_batch((lp, IDX_DIM))],
        out_specs=pl.BlockSpec((tq, N_HEADS * HEAD_DIM), lambda b, i: (b * nq + i, 0)),
        scratch_shapes=[pltpu.VMEM((tq, (lp // LANES | 1) * LANES), F32),
                        pltpu.VMEM((N_KV_HEADS, (N_HEADS // N_KV_HEADS) * tq, LANES), F32),
                        pltpu.VMEM((N_KV_HEADS, (N_HEADS // N_KV_HEADS) * tq, LANES), F32)],
        compiler_params=_cparams(("arbitrary", "arbitrary")),
        name="dsa",
    )(qh, qih, kw_flat, kh, vh, kih)


def _dsa_t_kernel(q_ref, qi_ref, wt_ref, k_ref, vt_ref, ki_ref, o_ref, sc_ref, sch_ref, m_ref, mu_ref, acc_ref,
                  s_ref, s2_ref,
                  *, tk, k_sel, n_valid_end, idx_bits):
    tq = LANES
    i = pl.program_id(1)
    qf0 = i * tq
    klim = jnp.minimum(FRONT + qf0 + tq, n_valid_end)
    n_tiles = (klim + tk - 1) // tk
    kf = float(k_sel)
    group = N_HEADS // N_KV_HEADS
    fold = 32
    fold_h = 64
    assert sch_ref.shape[0] // fold_h <= 256

    wt = wt_ref[0]
    qlane = lax.broadcasted_iota(I32, (1, tq), 1)
    lim = jnp.minimum(FRONT + ((qf0 + qlane) // CHUNK + 1) * CHUNK, n_valid_end)
    qi_all = qi_ref[0].reshape(IDX_HEADS * tq, IDX_DIM)

    def score_tile(t, carry):
        th = tk // 2
        for half in range(2):
            k0 = pl.multiple_of(t * tk, tk) + half * th
            rel = lax.dot_general(ki_ref[0, pl.ds(k0, th), :], qi_all, (((1,), (1,)), ((), ())),
                                  preferred_element_type=F32)
            sc = jnp.zeros((th, tq), F32)
            for h in range(IDX_HEADS):
                sc = sc + wt[h:h + 1, :] * jnp.maximum(rel[:, h * tq:(h + 1) * tq], 0.0)
            j = k0 + lax.broadcasted_iota(I32, (th, tq), 0)
            j = jnp.where(j >= FRONT - N_META, j, n_valid_end)
            sc = jnp.where(j < lim, sc, NEG_INF)
            sc_ref[pl.ds(k0, th), :] = sc
            sch_ref[pl.ds(k0, th), :] = _high_half(sc).astype(HALF)
        return carry

    lax.fori_loop(0, n_tiles, score_tile, 0)

    def count(pred):
        def body(t, acc):
            k0 = pl.multiple_of(t * tk, tk)
            ones = pred(sc_ref[pl.ds(k0, tk), :], k0)
            return acc + jnp.sum(ones.reshape(tk // fold, fold, tq), axis=0)
        acc = lax.fori_loop(0, n_tiles, body, jnp.zeros((fold, tq), F32))
        return jnp.sum(acc, axis=0, keepdims=True)

    def count_high(thr):
        thr_h = _high_half(thr).astype(HALF)
        one, zero = jnp.ones((), HALF), jnp.zeros((), HALF)

        def body(t, acc):
            ones = jnp.where(sch_ref[pl.ds(pl.multiple_of(t * tk, tk), tk), :] >= thr_h, one, zero)
            for c in range(tk // fold_h):
                acc = acc + ones[c * fold_h:(c + 1) * fold_h]
            return acc
        acc = lax.fori_loop(0, n_tiles, body, jnp.zeros((fold_h, tq), HALF))
        return jnp.sum(acc.astype(F32), axis=0, keepdims=True)

    def bisect(counter):
        def step(it, carry):
            u, c_u = carry
            trial = u | jnp.left_shift(jnp.int32(1), 31 - it)
            cnt = counter(_key_to_float(trial))
            take = cnt >= kf
            return jnp.where(take, trial, u), jnp.where(take, cnt, c_u)
        return step

    n_adm = (lim - (FRONT - N_META)).astype(F32)
    low_step = bisect(lambda thr: count(lambda blk, k0: jnp.where(blk >= thr, 1.0, 0.0)))

    def unsettled(c_u):
        return jnp.max(jnp.where((c_u == kf) | (n_adm <= kf), 0, 1))

    def low_bits(state):
        it, _, u, c_u = state
        for j in range(LOW_GROUP):
            u, c_u = low_step(it + j, (u, c_u))
        return it + LOW_GROUP, unsettled(c_u), u, c_u

    u, c_u = lax.fori_loop(0, 16, bisect(count_high), (jnp.zeros((1, tq), I32), jnp.full((1, tq), jnp.inf, F32)))
    _, _, u, c_u = lax.while_loop(lambda state: (state[0] < 32) & (state[1] > 0), low_bits,
                                  (jnp.int32(16), unsettled(c_u), u, c_u))
    key = jnp.maximum(u ^ INT_MIN, SKEY_LOWEST_FINITE)
    thr = _skey_to_float(key)
    n_ge = jnp.minimum(c_u, n_adm)
    tied = jnp.max(n_ge) > kf

    def write_bias(sel):
        def body(t, carry):
            k0 = pl.multiple_of(t * tk, tk)
            sc_ref[pl.ds(k0, tk), :] = sel(sc_ref[pl.ds(k0, tk), :], k0)
            return carry
        lax.fori_loop(0, n_tiles, body, 0)

    @pl.when(jnp.logical_not(tied))
    def _():
        write_bias(lambda blk, k0: jnp.where(blk >= thr, 0.0, NEG_INF))

    @pl.when(tied)
    def _():
        need = kf - count(lambda blk, k0: jnp.where(blk > thr, 1.0, 0.0))
        krow = lax.broadcasted_iota(I32, (tk, tq), 0)

        def jbis(it, ans):
            trial = ans + jnp.left_shift(jnp.int32(1), idx_bits - 1 - it)
            cnt = count(lambda blk, k0: jnp.where(blk == thr, jnp.where(k0 + krow < trial, 1.0, 0.0), 0.0))
            return jnp.where(cnt < need, trial, ans)

        ans = lax.fori_loop(0, idx_bits, jbis, jnp.zeros((1, tq), I32))
        write_bias(lambda blk, k0: jnp.where(
            blk > thr, 0.0, jnp.where(blk == thr, jnp.where(k0 + krow <= ans, 0.0, NEG_INF), NEG_INF)))

    m_ref[...] = jnp.full(m_ref.shape, M_INIT, F32)
    acc_ref[...] = jnp.zeros(acc_ref.shape, F32)

    mu_ref[...] = jnp.full(mu_ref.shape, M_INIT, F32)

    def logits(t, buf, g, m_before):
        k0 = pl.multiple_of(t * tk, tk)
        bias = sc_ref[pl.ds(k0, tk), :]
        qg = q_ref[0, g * group:(g + 1) * group].reshape(group * tq, HEAD_DIM)
        s = lax.dot_general(k_ref[0, g, pl.ds(k0, tk), :], qg, (((1,), (1,)), ((), ())),
                            preferred_element_type=F32)
        s = jnp.concatenate([s[:, r * tq:(r + 1) * tq] + bias for r in range(group)], axis=1)
        buf[g] = s
        return jnp.maximum(m_before, jnp.max(s, axis=0, keepdims=True))

    def values(t, buf, g, m_t):
        p = jnp.exp2(buf[g] - m_t)
        pv = jnp.dot(vt_ref[0, g, :, pl.ds(pl.multiple_of(t * tk, tk), tk)], p.astype(BF16),
                     preferred_element_type=F32)
        acc_ref[g] = jnp.exp2(mu_ref[g] - m_t) * acc_ref[g] + pv
        mu_ref[g] = m_t

    def step(t_next, buf_next, t, buf):
        for g in range(N_KV_HEADS):
            m_t = m_ref[g]
            m_ref[g] = logits(t_next, buf_next, g, m_t)
            values(t, buf, g, m_t)

    for g in range(N_KV_HEADS):
        m_ref[g] = logits(0, s_ref, g, m_ref[g])

    def tile_pair(j, carry):
        step(2 * j + 1, s2_ref, 2 * j, s_ref)
        step(jnp.minimum(2 * j + 2, n_tiles - 1), s_ref, 2 * j + 1, s2_ref)
        return carry

    lax.fori_loop(0, n_tiles >> 1, tile_pair, 0)

    @pl.when((n_tiles & 1) == 1)
    def _():
        for g in range(N_KV_HEADS):
            values(n_tiles - 1, s_ref, g, m_ref[g])

    for g in range(N_KV_HEADS):
        acc = acc_ref[g]
        o = acc[:HEAD_DIM, :] / acc[HEAD_DIM:HEAD_DIM + 1, :]
        for r in range(0, group, 2):
            pair = jnp.concatenate([o[:, r * tq:(r + 1) * tq], o[:, (r + 1) * tq:(r + 2) * tq]], axis=0)
            h0 = g * group + r
            o_ref[:, h0 * HEAD_DIM:(h0 + 2) * HEAD_DIM] = pair.T.astype(BF16)


def _dsa_t(qh, qih, wt, kh, vt, kih, *, tk, k_sel, n_valid_end):
    nb, _, s, _ = qh.shape
    lp = kh.shape[2]
    tq = LANES
    nq = s // tq
    group = N_HEADS // N_KV_HEADS
    kern = functools.partial(_dsa_t_kernel, tk=tk, k_sel=k_sel, n_valid_end=n_valid_end,
                             idx_bits=max(1, int(lp).bit_length()))
    per_batch = lambda shp: pl.BlockSpec((1,) + shp, lambda b, i: (b,) + (0,) * len(shp), pipeline_mode=pl.Buffered(1))
    return pl.pallas_call(
        kern,
        out_shape=jax.ShapeDtypeStruct((nb * s, N_HEADS * HEAD_DIM), BF16),
        grid=(nb, nq),
        in_specs=[pl.BlockSpec((1, N_HEADS, tq, HEAD_DIM), lambda b, i: (b, 0, i, 0)),
                  pl.BlockSpec((1, IDX_HEADS, tq, IDX_DIM), lambda b, i: (b, 0, i, 0)),
                  pl.BlockSpec((1, IDX_HEADS, tq), lambda b, i: (b, 0, i)),
                  per_batch((N_KV_HEADS, lp, HEAD_DIM)), per_batch((N_KV_HEADS, VT_ROWS, lp)), per_batch((lp, IDX_DIM))],
        out_specs=pl.BlockSpec((tq, N_HEADS * HEAD_DIM), lambda b, i: (b * nq + i, 0)),
        scratch_shapes=[pltpu.VMEM((lp, tq), F32), pltpu.VMEM((lp, tq), HALF),
                        pltpu.VMEM((N_KV_HEADS, 1, group * tq), F32),
                        pltpu.VMEM((N_KV_HEADS, 1, group * tq), F32),
                        pltpu.VMEM((N_KV_HEADS, VT_ROWS, group * tq), F32),
                        pltpu.VMEM((N_KV_HEADS, tk, group * tq), F32),
                        pltpu.VMEM((N_KV_HEADS, tk, group * tq), F32)],
        compiler_params=_cparams(("arbitrary", "arbitrary")),
        name="dsa_t",
    )(qh, qih, wt, kh, vt, kih)


def _pad_keys(meta_rows, mid_rows, lp):
    nb = mid_rows[0].shape[0]
    tail = mid_rows[0].shape[2:]
    meta_b = jnp.broadcast_to(meta_rows[None], (nb,) + meta_rows.shape)
    used = FRONT + sum(m.shape[1] for m in mid_rows)
    parts = [jnp.zeros((nb, FRONT - N_META) + tail, meta_rows.dtype), meta_b] + list(mid_rows)
    if lp > used:
        parts.append(jnp.zeros((nb, lp - used) + tail, meta_rows.dtype))
    return jnp.concatenate(parts, axis=1)


def _attention_inputs(k_meta, v_meta, ki_meta, k_mid, v_mid, ki_mid, lp, keys_on_lanes):
    nb = k_mid[0].shape[0]
    kp = _pad_keys(k_meta, k_mid, lp).astype(BF16).reshape(nb, lp, N_KV_HEADS, HEAD_DIM).transpose(0, 2, 1, 3)
    vp = _pad_keys(v_meta, v_mid, lp).astype(BF16).reshape(nb, lp, N_KV_HEADS, HEAD_DIM)
    if keys_on_lanes:
        vp = vp.transpose(0, 2, 3, 1)
        ones = jnp.ones((nb, N_KV_HEADS, 1, lp), BF16)
        vp = jnp.concatenate([vp, ones, jnp.zeros((nb, N_KV_HEADS, VT_ROWS - HEAD_DIM - 1, lp), BF16)], axis=2)
    else:
        vp = vp.transpose(0, 2, 1, 3)
        ones = jnp.ones(vp.shape[:3] + (1,), BF16)
        vp = jnp.concatenate([vp, ones, jnp.zeros(vp.shape[:3] + (LANES - HEAD_DIM - 1,), BF16)], axis=3)
    kip = _pad_keys(ki_meta, ki_mid, lp).astype(BF16)
    return kp, vp, kip


def _merge_kernel(xp_ref, xs_ref, cp_ref, cs_ref, ap_ref, as_ref, gp_ref, gs_ref, wao_ref, wout_ref, gffn_ref, wr_ref,
                  br_ref, x1_ref, hn_ref, ids_ref, gts_ref, *, n_first):
    first = pl.program_id(0) < n_first
    x = jnp.where(first, xp_ref[...], xs_ref[...])
    conv = jnp.where(first, cp_ref[...], cs_ref[...]).astype(F32)
    attn = jnp.where(first, ap_ref[...], as_ref[...])
    ao = jnp.dot(attn, wao_ref[...], preferred_element_type=F32)
    gates = jnp.where(first, gp_ref[...], gs_ref[...]).astype(F32)
    mix = gates[:, :D_MODEL] * conv + gates[:, D_MODEL:] * ao
    x1 = x + jnp.dot(mix.astype(BF16), wout_ref[...], preferred_element_type=F32)
    x1_ref[...] = x1
    hn = x1 * lax.rsqrt(jnp.mean(x1 * x1, axis=-1, keepdims=True) + EPS) * gffn_ref[...]
    tr = hn.shape[0]
    for c in range(SUB):
        hn_ref[pl.ds(c, tr, stride=SUB), :] = hn[:, c * LANES:(c + 1) * LANES]
    logits = jnp.dot(hn, wr_ref[...], preferred_element_type=F32, precision=lax.Precision.HIGHEST) + br_ref[...]
    lane = lax.broadcasted_iota(I32, logits.shape, 1)
    lg = jnp.where(lane < N_EXPERTS, logits, NEG_INF)
    vals, ids = [], []
    for _ in range(TOP_K_EXPERTS):
        m = jnp.max(lg, axis=1, keepdims=True)
        idx = jnp.min(jnp.where(lg == m, lane, LANES), axis=1, keepdims=True)
        vals.append(m)
        ids.append(idx)
        lg = jnp.where(lane == idx, NEG_INF, lg)
    es = [jnp.exp(v - vals[0]) for v in vals]
    den = es[0]
    for e in es[1:]:
        den = den + e
    ids_out = jnp.zeros(logits.shape, I32)
    gts_out = jnp.zeros(logits.shape, F32)
    for j in range(TOP_K_EXPERTS):
        ids_out = jnp.where(lane == j, ids[j], ids_out)
        gts_out = jnp.where(lane == j, es[j] / den, gts_out)
    ids_ref[...] = ids_out
    gts_ref[...] = gts_out


def _merge(xp, xs, conv_p, conv_s, attn_p, attn_s, gates_p, gates_s, wao, wout, gffn, wr, br, tr):
    n_first, n_second = xp.shape[0] // tr, xs.shape[0] // tr
    rows = xp.shape[0] + xs.shape[0]
    first = lambda w: pl.BlockSpec((tr, w), lambda i: (jnp.minimum(i, n_first - 1), 0))
    second = lambda w: pl.BlockSpec((tr, w), lambda i: (jnp.maximum(i - n_first, 0), 0))
    row = lambda w: pl.BlockSpec((tr, w), lambda i: (i, 0))
    return pl.pallas_call(
        functools.partial(_merge_kernel, n_first=n_first),
        out_shape=[jax.ShapeDtypeStruct((rows, D_MODEL), F32), jax.ShapeDtypeStruct((rows * SUB, LANES), F32),
                   jax.ShapeDtypeStruct((rows, LANES), I32), jax.ShapeDtypeStruct((rows, LANES), F32)],
        grid=(n_first + n_second,),
        in_specs=[first(D_MODEL), second(D_MODEL), first(D_MODEL), second(D_MODEL), first(D_MODEL), second(D_MODEL),
                  first(2 * D_MODEL), second(2 * D_MODEL), _resident(wao), _resident(wout), _resident(gffn),
                  _resident(wr), _resident(br)],
        out_specs=[row(D_MODEL), pl.BlockSpec((tr * SUB, LANES), lambda i: (i, 0)), row(LANES), row(LANES)],
        compiler_params=_cparams(("arbitrary",)),
        name="merge",
    )(xp, xs, conv_p, conv_s, attn_p, attn_s, gates_p, gates_s, wao, wout, gffn, wr, br)


KEY_ALIGN = 1024


def _key_window(tm):
    return -(-(tm + KEY_ALIGN - 1) // KEY_ALIGN) * KEY_ALIGN


def _for_rows(n, fn):
    def group(gi, carry):
        for u in range(SUB):
            fn(gi * SUB + u)
        return carry

    def single(r, carry):
        fn(r)
        return carry

    shift = SUB.bit_length() - 1
    lax.fori_loop(0, n >> shift, group, 0)
    lax.fori_loop((n >> shift) << shift, n, single, 0)


def _moe_kernel(te_ref, base_ref, nval_ref, nreal_ref, keys_hbm, h_hbm, w1_ref, b1_ref, w3_ref, b3_ref, w2_ref, b2_ref,
                y_hbm, key_sm, xbuf, xb, ybuf, wb, ksem, gsem, ssem, *, tm, n_tok, pair_bits):
    i = pl.program_id(0)
    nreal = nreal_ref[0]
    window = key_sm.shape[0] // 3
    slot_shift = TOP_K_EXPERTS.bit_length() - 1

    def keys_copy(tile):
        s = lax.rem(tile, 3)
        start = pl.multiple_of(base_ref[tile] & -KEY_ALIGN, KEY_ALIGN)
        return pltpu.make_async_copy(keys_hbm.at[pl.ds(start, window)],
                                     key_sm.at[pl.ds(pl.multiple_of(s * window, KEY_ALIGN), window)], ksem.at[s])

    def pair_reader(tile):
        first = lax.rem(tile, 3) * window + (base_ref[tile] & (KEY_ALIGN - 1))
        return lambda r: key_sm[first + r] & ((1 << pair_bits) - 1)

    def gather_row(s, r, token):
        return pltpu.make_async_copy(h_hbm.at[pl.ds(token * SUB, SUB)], xbuf.at[s, pl.ds(r * SUB, SUB)], gsem.at[s])

    def scatter_row(r, out_token):
        return pltpu.make_async_copy(ybuf.at[pl.ds(r * SUB, SUB)], y_hbm.at[pl.ds(out_token * SUB, SUB)], ssem.at[0])

    def start_gather(tile):
        pair_of, s = pair_reader(tile), tile & 1
        _for_rows(tm, lambda r: gather_row(s, r, pair_of(r) >> slot_shift).start())

    @pl.when(jnp.logical_and(i == 0, nreal > 0))
    def _():
        keys_copy(0).start()

        @pl.when(nreal > 1)
        def _():
            keys_copy(1).start()

        keys_copy(0).wait()
        start_gather(0)

    @pl.when(i + 1 < nreal)
    def _():
        keys_copy(i + 1).wait()
        start_gather(i + 1)

    @pl.when(i + 2 < nreal)
    def _():
        keys_copy(i + 2).start()

    @pl.when(i < nreal)
    def _():
        s = i & 1
        _for_rows(tm, lambda r: gather_row(s, r, 0).wait())
        for c in range(SUB):
            xb[:, c * LANES:(c + 1) * LANES] = xbuf[s, pl.ds(c, tm, stride=SUB), :].astype(BF16)
        @pl.when(jnp.logical_or(i == 0, te_ref[i] != te_ref[jnp.maximum(i - 1, 0)]))
        def _():
            for j, w_ref in enumerate((w1_ref, w3_ref, w2_ref)):
                wb[j] = w_ref[0].astype(BF16)

        x = xb[...]
        g = jnp.minimum(jnp.dot(x, wb[0], preferred_element_type=F32) + b1_ref[0], SWIGLU_LIMIT)
        u = jnp.clip(jnp.dot(x, wb[1], preferred_element_type=F32) + b3_ref[0], -SWIGLU_LIMIT, SWIGLU_LIMIT)
        act = g * jax.nn.sigmoid(SWIGLU_ALPHA * g) * (u + 1.0)
        y = jnp.dot(act.astype(BF16), wb[2], preferred_element_type=F32) + b2_ref[0]

        @pl.when(i > 0)
        def _():
            _for_rows(nval_ref[i - 1], lambda r: scatter_row(r, 0).wait())

        for c in range(SUB):
            ybuf[pl.ds(c, tm, stride=SUB), :] = y[:, c * LANES:(c + 1) * LANES]

        pair_of = pair_reader(i)

        def put(r):
            pair = pair_of(r)
            scatter_row(r, (pair & (TOP_K_EXPERTS - 1)) * n_tok + (pair >> slot_shift)).start()

        _for_rows(nval_ref[i], put)

        @pl.when(i == nreal - 1)
        def _():
            _for_rows(nval_ref[i], lambda r: scatter_row(r, 0).wait())


def _moe(hn, keys, tile_expert, base, nval, n_real, w1, b1, w3, b3, w2, b2, n_tok, pair_bits, tm):
    nt = tile_expert.shape[0]
    by_expert = lambda i, te, *_: (te[i], 0, 0)
    wspec = pl.BlockSpec((1, D_MODEL, D_FF), by_expert)
    w2spec = pl.BlockSpec((1, D_FF, D_MODEL), by_expert)
    bspec = pl.BlockSpec((1, 1, D_FF), by_expert)
    any_spec = pl.BlockSpec(memory_space=pl.ANY)
    grid_spec = pltpu.PrefetchScalarGridSpec(
        num_scalar_prefetch=4,
        grid=(nt,),
        in_specs=[any_spec, any_spec, wspec, bspec, wspec, bspec, w2spec, bspec],
        out_specs=any_spec,
        scratch_shapes=[pltpu.SMEM((3 * _key_window(tm),), I32),
                        pltpu.VMEM((2, tm * SUB, LANES), F32), pltpu.VMEM((tm, D_MODEL), BF16),
                        pltpu.VMEM((tm * SUB, LANES), F32), pltpu.VMEM((3, D_MODEL, D_FF), BF16),
                        pltpu.SemaphoreType.DMA((3,)), pltpu.SemaphoreType.DMA((2,)), pltpu.SemaphoreType.DMA((1,))],
    )
    return pl.pallas_call(
        functools.partial(_moe_kernel, tm=tm, n_tok=n_tok, pair_bits=pair_bits),
        out_shape=jax.ShapeDtypeStruct((TOP_K_EXPERTS * n_tok * SUB, LANES), F32),
        grid_spec=grid_spec,
        compiler_params=_cparams(("arbitrary",)),
        name="moe",
    )(tile_expert, base, nval, n_real, keys, hn, w1, b1, w3, b3, w2, b2)


def _route(ids, n, tm):
    n_pairs = n * TOP_K_EXPERTS
    pair_bits = max(1, (n_pairs - 1).bit_length())
    assert pair_bits + (N_EXPERTS - 1).bit_length() <= 31 and tm % LANES == 0
    assert TOP_K_EXPERTS & (TOP_K_EXPERTS - 1) == 0
    pair = jnp.arange(n_pairs, dtype=I32)
    keys = jnp.sort(jnp.left_shift(ids.reshape(-1), pair_bits) | pair)
    starts = jnp.searchsorted(keys, jnp.left_shift(jnp.arange(N_EXPERTS + 1, dtype=I32), pair_bits),
                              method="compare_all").astype(I32)
    counts = starts[1:] - starts[:-1]
    tiles = (counts + tm - 1) // tm
    tile_end = jnp.cumsum(tiles)
    nt = -(-(n_pairs + N_EXPERTS * (tm - 1)) // tm)
    ti = jnp.arange(nt, dtype=I32)
    te = jnp.minimum(jnp.searchsorted(tile_end, ti, side="right", method="compare_all"), N_EXPERTS - 1).astype(I32)
    within = ti - (tile_end - tiles)[te]
    base = jnp.clip(starts[:-1][te] + within * tm, 0, n_pairs - 1).astype(I32)
    nval = jnp.clip(counts[te] - within * tm, 0, tm).astype(I32)
    padded = -(-n_pairs // KEY_ALIGN) * KEY_ALIGN + _key_window(tm)
    keys = jnp.pad(keys, (0, padded - n_pairs))
    return keys, te, base, nval, tile_end[-1:].astype(I32), pair_bits


def _final_kernel(x1_ref, y0_ref, y1_ref, y2_ref, y3_ref, gts_ref, g_ref, out_ref):
    gts = gts_ref[...]
    tr = gts.shape[0]
    cols = []
    for c in range(SUB):
        acc = x1_ref[:, c * LANES:(c + 1) * LANES]
        for j, y_ref in enumerate((y0_ref, y1_ref, y2_ref, y3_ref)):
            acc = acc + gts[:, j:j + 1] * y_ref[pl.ds(c, tr, stride=SUB), :]
        cols.append(acc)
    ss = cols[0] * cols[0]
    for acc in cols[1:]:
        ss = ss + acc * acc
    scale = lax.rsqrt(jnp.sum(ss, axis=-1, keepdims=True) / D_MODEL + EPS)
    for c, acc in enumerate(cols):
        out_ref[:, c * LANES:(c + 1) * LANES] = acc * scale * g_ref[:, c * LANES:(c + 1) * LANES]


def _final(x1, yslots, gts, g, tr):
    n = x1.shape[0]
    nblk = n // tr
    row = lambda w: pl.BlockSpec((tr, w), lambda i: (i, 0))
    slot = lambda j: pl.BlockSpec((tr * SUB, LANES), lambda i: (j * nblk + i, 0))
    return pl.pallas_call(
        _final_kernel,
        out_shape=jax.ShapeDtypeStruct((n, D_MODEL), F32),
        grid=(nblk,),
        in_specs=[row(D_MODEL)] + [slot(j) for j in range(TOP_K_EXPERTS)] + [row(LANES), _resident(g)],
        out_specs=row(D_MODEL),
        compiler_params=_cparams(("arbitrary",)),
        name="final",
    )(x1, yslots, yslots, yslots, yslots, gts, g)


def _tile(n, pref):
    t = min(n, pref)
    while n % t:
        t //= 2
    return t


def kernel(x_prompt, x_sample, cache_k, cache_v, cache_idx_k, state_conv, meta_tokens, g_norm_mix, w_in, conv_w,
           conv_b, conv_ln_g, conv_ln_b, w_conv_out, w_attn_out, w_out, g_norm_ffn, w_router, b_router,
           w1, b1, w3, b3, w2, b2, g_norm_final):
    nb, s_p, _ = x_prompt.shape
    db, s_n, _ = x_sample.shape
    past = cache_k.shape[1]
    rows_p, rows_s = nb * s_p, db * s_n
    n = rows_p + rows_s
    tr = _tile(rows_s, ROW_TILE)
    assert rows_p % tr == 0 and s_p % tr == 0 and tr % s_n == 0 and s_p % Q_TILE == 0

    cuts = [0, 2 * CONV_CH, N_HEADS * HEAD_DIM, N_KV_HEADS * HEAD_DIM, N_KV_HEADS * HEAD_DIM, IDX_HEADS * IDX_DIM,
            IDX_DIM + IDX_HEADS, 2 * D_MODEL]
    offs = [sum(cuts[:i + 1]) for i in range(len(cuts))]
    seg = [w_in[:, offs[i]:offs[i + 1]].astype(BF16) for i in range(len(cuts) - 1)]
    seg[5] = jnp.pad(seg[5], ((0, 0), (0, LANES - seg[5].shape[1])))
    g_mix = g_norm_mix.reshape(1, D_MODEL)

    tabs_p = _rope_tables(N_META + jnp.arange(s_p, dtype=I32))
    tabs_s = _rope_tables(jnp.tile(N_META + past + jnp.arange(s_n, dtype=I32), tr // s_n))
    tabs_m = _rope_tables(jnp.arange(N_META, dtype=I32))

    xp, xs = x_prompt.reshape(rows_p, D_MODEL), x_sample.reshape(rows_s, D_MODEL)
    a_p, qh_p, k_p, v_p, qih_p, kw_p, gates_p, kh_mid, vt_mid, ki_mid, wt_p = _project_seq(
        xp, nb, s_p, tabs_p, g_mix, seg, tr)
    a_s, q_s, k_s, v_s, qi_s, kw_s, gates_s = [t[:rows_s] for t in _project(xs, xs, tabs_s, lambda i: 0, g_mix, seg, tr)]
    meta = meta_tokens.astype(F32)
    a_m, _, k_m, v_m, _, kw_m, _ = _project(meta, meta, tabs_m, lambda i: 0, g_mix, seg, N_META)
    a_m, k_m, v_m, ki_m = a_m[:N_META], k_m[:N_META], v_m[:N_META], kw_m[:N_META, :IDX_DIM]

    cw = jnp.pad(conv_w, ((0, HALO - CONV_WIDTH), (0, 0)))
    cb, lg, lb = conv_b.reshape(1, -1), conv_ln_g.reshape(1, -1), conv_ln_b.reshape(1, -1)
    wco = w_conv_out.astype(BF16)
    ctx_p = jnp.concatenate([jnp.zeros((HALO - N_META, CONV_CH), F32), a_m], axis=0)[None]
    ctx_s = jnp.pad(state_conv, ((0, 0), (HALO - (CONV_WIDTH - 1), 0), (0, 0)))
    conv_p = _conv_branch(a_p, 0, nb, s_p, ctx_p, cw, cb, lg, lb, wco, _tile(s_p, 256))
    conv_s = _conv_branch(a_s, 0, db, s_n, ctx_s, cw, cb, lg, lb, wco, s_n)

    k_p3, v_p3 = k_p.reshape(nb, s_p, -1), v_p.reshape(nb, s_p, -1)
    ki_p3 = kw_p[:, :IDX_DIM].reshape(nb, s_p, IDX_DIM)
    k_s3, v_s3 = k_s.reshape(db, s_n, -1), v_s.reshape(db, s_n, -1)
    ki_s3 = kw_s[:, :IDX_DIM].reshape(db, s_n, IDX_DIM)

    tk_p = _tile(s_p, K_TILE_T)
    lp_p = -(-(FRONT + s_p) // tk_p) * tk_p
    front = lambda rows: jnp.pad(rows, ((FRONT - N_META, 0), (0, 0))).astype(BF16)
    per_head = lambda t: t.reshape(FRONT, N_KV_HEADS, HEAD_DIM).transpose(1, 0, 2)
    rep = lambda t: jnp.broadcast_to(t[None], (nb,) + t.shape)
    tail = lp_p - FRONT - s_p
    kp = jnp.concatenate([rep(per_head(front(k_m))), kh_mid, jnp.zeros((nb, N_KV_HEADS, tail, HEAD_DIM), BF16)], axis=2)
    vt_front = jnp.concatenate([per_head(front(v_m)).transpose(0, 2, 1), jnp.ones((N_KV_HEADS, 1, FRONT), BF16),
                                jnp.zeros((N_KV_HEADS, VT_ROWS - HEAD_DIM - 1, FRONT), BF16)], axis=1)
    vtp = jnp.concatenate([rep(vt_front), vt_mid, jnp.zeros((nb, N_KV_HEADS, VT_ROWS, tail), BF16)], axis=3)
    kip = jnp.concatenate([rep(front(ki_m)), ki_mid, jnp.zeros((nb, tail, IDX_DIM), BF16)], axis=1)
    attn_p = _dsa_t(qh_p, qih_p, wt_p, kp, vtp, kip, tk=tk_p, k_sel=min(TOPK_MAX, s_p // 4),
                    n_valid_end=FRONT + s_p)

    def heads(t, b_, s_, nh):
        return t.reshape(b_, s_, nh, -1).transpose(0, 2, 1, 3)

    tk_s = K_TILE
    lp_s = -(-(FRONT + past + s_n) // tk_s) * tk_s
    ck = cache_k.reshape(db, past, -1)
    cv = cache_v.reshape(db, past, -1)
    ks, vs, kis = _attention_inputs(k_m, v_m, ki_m, [ck, k_s3], [cv, v_s3], [cache_idx_k, ki_s3], lp_s, False)
    attn_s = _dsa(heads(q_s, db, s_n, N_HEADS), heads(qi_s, db, s_n, IDX_HEADS), kw_s, 0, ks, vs, kis,
                  tq=s_n, tk=tk_s, k_sel=min(TOPK_MAX, (past + s_n) // 4), q_off=past,
                  n_valid_end=FRONT + past + s_n)

    wr = jnp.pad(w_router, ((0, 0), (0, LANES - N_EXPERTS)))
    br = jnp.pad(b_router, (0, LANES - N_EXPERTS)).reshape(1, LANES)
    x1, hn, ids, gts = _merge(xp, xs, conv_p, conv_s, attn_p, attn_s, gates_p, gates_s, w_attn_out.astype(BF16),
                              w_out.astype(BF16), g_norm_ffn.reshape(1, -1), wr, br, tr)

    tm = MOE_TILE
    keys, tile_expert, base, nval, n_real, pair_bits = _route(ids[:, :TOP_K_EXPERTS], n, tm)
    yslots = _moe(hn, keys, tile_expert, base, nval, n_real, w1, b1.reshape(N_EXPERTS, 1, D_FF),
                  w3, b3.reshape(N_EXPERTS, 1, D_FF), w2, b2.reshape(N_EXPERTS, 1, D_MODEL), n, pair_bits, tm)
    out = _final(x1, yslots, gts, g_norm_final.reshape(1, -1), tr)

    y_prompt = out[:rows_p].reshape(nb, s_p, D_MODEL)
    y_sample = out[rows_p:].reshape(db, s_n, D_MODEL)
    lead = lambda m: jnp.broadcast_to(m[None], (nb,) + m.shape)
    k_prompt = jnp.concatenate([lead(k_m), k_p3], axis=1).reshape(nb, N_META + s_p, N_KV_HEADS, HEAD_DIM)
    v_prompt = jnp.concatenate([lead(v_m), v_p3], axis=1).reshape(nb, N_META + s_p, N_KV_HEADS, HEAD_DIM)
    idxk_prompt = jnp.concatenate([lead(ki_m), ki_p3], axis=1)
    a_p3 = a_p.reshape(nb, s_p, CONV_CH)
    a_s3 = a_s.reshape(db, s_n, CONV_CH)
    keep = CONV_WIDTH - 1
    conv_state_p = jnp.concatenate([lead(a_m), a_p3], axis=1)[:, N_META + s_p - keep:]
    conv_state_s = jnp.concatenate([state_conv, a_s3], axis=1)[:, s_n:]
    return (y_prompt, y_sample, k_prompt, v_prompt, idxk_prompt, conv_state_p,
            k_s3.reshape(db, s_n, N_KV_HEADS, HEAD_DIM), v_s3.reshape(db, s_n, N_KV_HEADS, HEAD_DIM), ki_s3,
            conv_state_s)
```

```python
import functools

import jax
import jax.numpy as jnp
from jax import lax
from jax.experimental import pallas as pl
from jax.experimental.pallas import tpu as pltpu

F32, BF16, I32 = jnp.float32, jnp.bfloat16, jnp.int32
HALF = jnp.bfloat16

D_MODEL = 1024
CHUNK = 64
N_META = 16
CONV_CH = D_MODEL
CONV_WIDTH = 31
N_HEADS = 16
N_KV_HEADS = 4
HEAD_DIM = 64
ROT_DIM = HEAD_DIM // 4
ROPE_THETA = 500000.0
IDX_HEADS = 8
IDX_DIM = 64
TOPK_MAX = 256
N_EXPERTS = 32
TOP_K_EXPERTS = 4
D_FF = D_MODEL
SWIGLU_LIMIT = 7.0
SWIGLU_ALPHA = 1.702
EPS = 1e-5
Q_SCALE = HEAD_DIM ** -0.5 * 1.4426950408889634

LANES = 128
SUB = 8
VT_ROWS = HEAD_DIM + 2 * SUB
FRONT = 128
HALO = 32
VMEM_LIMIT = 56 * 1024 * 1024
NEG_INF = float("-inf")
INT_MIN = -2 ** 31
M_INIT = -1e30

ROW_TILE = 512
Q_TILE = 128
K_TILE = 512
K_TILE_T = 1024
LOW_GROUP = 4
MOE_TILE = 512


def _cparams(sem):
    return pltpu.CompilerParams(dimension_semantics=sem, vmem_limit_bytes=VMEM_LIMIT)


def _resident(arr):
    nd = arr.ndim
    return pl.BlockSpec(arr.shape, lambda *_: (0,) * nd, pipeline_mode=pl.Buffered(1))


def _rope_cols(x, c, s1, s2):
    outs = []
    for j in range(x.shape[1] // LANES):
        xj = x[:, j * LANES:(j + 1) * LANES]
        outs.append(xj * c + pltpu.roll(xj, LANES - ROT_DIM // 2, 1) * s1 + pltpu.roll(xj, ROT_DIM // 2, 1) * s2)
    return outs


def _proj_kernel(xp_ref, xs_ref, g_ref, wa_ref, wq_ref, wk_ref, wv_ref, wqi_ref, wkw_ref, wgl_ref,
                 c_ref, s1_ref, s2_ref,
                 a_ref, q_ref, k_ref, v_ref, qi_ref, kw_ref, gate_ref, *, n_first):
    x = jnp.where(pl.program_id(0) < n_first, xp_ref[...], xs_ref[...])
    h = x * lax.rsqrt(jnp.mean(x * x, axis=-1, keepdims=True) + EPS) * g_ref[...]
    hb = h.astype(BF16)
    c, s1, s2 = c_ref[...], s1_ref[...], s2_ref[...]

    def dot(w_ref):
        return jnp.dot(hb, w_ref[...], preferred_element_type=F32)

    glu = dot(wa_ref)
    a_ref[...] = glu[:, :CONV_CH] * jax.nn.sigmoid(glu[:, CONV_CH:])
    for j, col in enumerate(_rope_cols(dot(wq_ref), c, s1, s2)):
        q_ref[:, j * LANES:(j + 1) * LANES] = (col * Q_SCALE).astype(BF16)
    for j, col in enumerate(_rope_cols(dot(wk_ref), c, s1, s2)):
        k_ref[:, j * LANES:(j + 1) * LANES] = col
    v_ref[...] = dot(wv_ref)
    for j, col in enumerate(_rope_cols(dot(wqi_ref), c, s1, s2)):
        qi_ref[:, j * LANES:(j + 1) * LANES] = (col * IDX_DIM ** -0.5).astype(BF16)
    pkw = dot(wkw_ref)
    lane = lax.broadcasted_iota(I32, pkw.shape, 1)
    kw_ref[...] = jnp.where(lane < IDX_DIM, _rope_cols(pkw, c, s1, s2)[0], pkw * IDX_HEADS ** -0.5)
    gate_ref[...] = jax.nn.sigmoid(dot(wgl_ref)).astype(BF16)


def _rope_tables(pos):
    half = ROT_DIM // 2
    inv = ROPE_THETA ** (-jnp.arange(half, dtype=F32) / half)
    ang = pos.astype(F32)[:, None] * inv[None, :]
    cos, sin = jnp.cos(ang), jnp.sin(ang)
    r = pos.shape[0]
    z = lambda n: jnp.zeros((r, n), F32)
    c = jnp.concatenate([cos, cos, jnp.ones((r, HEAD_DIM - ROT_DIM), F32)], axis=1)
    s1 = jnp.concatenate([-sin, z(HEAD_DIM - half)], axis=1)
    s2 = jnp.concatenate([z(half), sin, z(HEAD_DIM - ROT_DIM)], axis=1)
    two = lambda t: jnp.concatenate([t, t], axis=1)
    return two(c), two(s1), two(s2)


def _project(xp, xs, tabs, tab_map, g, ws, tr):
    n_first, n_second = xp.shape[0] // tr, xs.shape[0] // tr
    rows = xp.shape[0] + xs.shape[0]
    row = lambda w: pl.BlockSpec((tr, w), lambda i: (i, 0))
    tab = pl.BlockSpec((tr, LANES), lambda i: (tab_map(i), 0))
    in_specs = [pl.BlockSpec((tr, D_MODEL), lambda i: (jnp.minimum(i, n_first - 1), 0)),
                pl.BlockSpec((tr, D_MODEL), lambda i: (jnp.maximum(i - n_first, 0), 0)),
                _resident(g)] + [_resident(w) for w in ws] + [tab, tab, tab]
    widths = [(CONV_CH, F32), (N_HEADS * HEAD_DIM, BF16), (N_KV_HEADS * HEAD_DIM, F32), (N_KV_HEADS * HEAD_DIM, F32),
              (IDX_HEADS * IDX_DIM, BF16), (LANES, F32), (2 * D_MODEL, BF16)]
    return pl.pallas_call(
        functools.partial(_proj_kernel, n_first=n_first),
        out_shape=[jax.ShapeDtypeStruct((rows, w), dt) for w, dt in widths],
        grid=(n_first + n_second,),
        in_specs=in_specs,
        out_specs=[row(w) for w, _ in widths],
        compiler_params=_cparams(("arbitrary",)),
        name="proj",
    )(xp, xs, g, *ws, *tabs)


def _proj_seq_kernel(x_ref, g_ref, wa_ref, wq_ref, wk_ref, wv_ref, wqi_ref, wkw_ref, wgl_ref, c_ref, s1_ref, s2_ref,
                     a_ref, q_ref, k_ref, v_ref, qi_ref, kw_ref, gate_ref, kh_ref, vt_ref, kih_ref, wt_ref):
    x = x_ref[...]
    h = x * lax.rsqrt(jnp.mean(x * x, axis=-1, keepdims=True) + EPS) * g_ref[...]
    hb = h.astype(BF16)
    c, s1, s2 = c_ref[...], s1_ref[...], s2_ref[...]

    def dot(w_ref):
        return jnp.dot(hb, w_ref[...], preferred_element_type=F32)

    def put_heads(ref, j, col):
        ref[0, 2 * j] = col[:, :HEAD_DIM].astype(BF16)
        ref[0, 2 * j + 1] = col[:, HEAD_DIM:].astype(BF16)

    glu = dot(wa_ref)
    a_ref[...] = glu[:, :CONV_CH] * jax.nn.sigmoid(glu[:, CONV_CH:])
    for j, col in enumerate(_rope_cols(dot(wq_ref), c, s1, s2)):
        put_heads(q_ref, j, col * Q_SCALE)
    for j, col in enumerate(_rope_cols(dot(wk_ref), c, s1, s2)):
        k_ref[:, j * LANES:(j + 1) * LANES] = col
        put_heads(kh_ref, j, col)
    v = dot(wv_ref)
    v_ref[...] = v
    tr = v.shape[0]
    ones_row = jnp.where(lax.broadcasted_iota(I32, (VT_ROWS - HEAD_DIM, tr), 0) == 0, 1.0, 0.0).astype(BF16)
    for j in range(v.shape[1] // LANES):
        vt = v[:, j * LANES:(j + 1) * LANES].T
        for half in range(2):
            vt_ref[0, 2 * j + half, 0:HEAD_DIM, :] = vt[half * HEAD_DIM:(half + 1) * HEAD_DIM].astype(BF16)
            vt_ref[0, 2 * j + half, HEAD_DIM:VT_ROWS, :] = ones_row
    for j, col in enumerate(_rope_cols(dot(wqi_ref), c, s1, s2)):
        put_heads(qi_ref, j, col * IDX_DIM ** -0.5)
    pkw = dot(wkw_ref)
    lane = lax.broadcasted_iota(I32, pkw.shape, 1)
    kw = jnp.where(lane < IDX_DIM, _rope_cols(pkw, c, s1, s2)[0], pkw * IDX_HEADS ** -0.5)
    kw_ref[...] = kw
    kih_ref[0] = kw[:, :IDX_DIM].astype(BF16)
    wt_ref[0] = kw.T[IDX_DIM:IDX_DIM + IDX_HEADS, :]
    gate_ref[...] = jax.nn.sigmoid(dot(wgl_ref)).astype(BF16)


def _project_seq(x, nb, s, tabs, g, ws, tr):
    nt = s // tr
    rows = nb * s
    flat = lambda w: pl.BlockSpec((tr, w), lambda b, i: (b * nt + i, 0))
    tab = pl.BlockSpec((tr, LANES), lambda b, i: (i, 0))
    heads = lambda nh: pl.BlockSpec((1, nh, tr, HEAD_DIM), lambda b, i: (b, 0, i, 0))
    widths = [(CONV_CH, F32), None, (N_KV_HEADS * HEAD_DIM, F32), (N_KV_HEADS * HEAD_DIM, F32), None, (LANES, F32),
              (2 * D_MODEL, BF16)]
    shapes = [jax.ShapeDtypeStruct((rows, w[0]), w[1]) if w else None for w in widths]
    specs = [flat(w[0]) if w else None for w in widths]
    shapes[1], specs[1] = jax.ShapeDtypeStruct((nb, N_HEADS, s, HEAD_DIM), BF16), heads(N_HEADS)
    shapes[4], specs[4] = jax.ShapeDtypeStruct((nb, IDX_HEADS, s, IDX_DIM), BF16), heads(IDX_HEADS)
    shapes += [jax.ShapeDtypeStruct((nb, N_KV_HEADS, s, HEAD_DIM), BF16),
               jax.ShapeDtypeStruct((nb, N_KV_HEADS, VT_ROWS, s), BF16),
               jax.ShapeDtypeStruct((nb, s, IDX_DIM), BF16),
               jax.ShapeDtypeStruct((nb, IDX_HEADS, s), F32)]
    specs += [heads(N_KV_HEADS),
              pl.BlockSpec((1, N_KV_HEADS, VT_ROWS, tr), lambda b, i: (b, 0, 0, i)),
              pl.BlockSpec((1, tr, IDX_DIM), lambda b, i: (b, i, 0)),
              pl.BlockSpec((1, IDX_HEADS, tr), lambda b, i: (b, 0, i))]
    return pl.pallas_call(
        _proj_seq_kernel,
        out_shape=shapes,
        grid=(nb, nt),
        in_specs=[flat(D_MODEL), _resident(g)] + [_resident(w) for w in ws] + [tab, tab, tab],
        out_specs=specs,
        compiler_params=_cparams(("arbitrary", "arbitrary")),
        name="proj_seq",
    )(x, g, *ws, *tabs)


def _conv_kernel(cur_ref, halo_ref, lc_ref, cw_ref, cb_ref, lg_ref, lb_ref, wo_ref, out_ref, e_ref, c_ref, *, tr, rc):
    halo = jnp.where(pl.program_id(1) == 0, lc_ref[0], halo_ref[...])
    e_ref[0, 0:HALO, :] = halo
    e_ref[0, HALO:HALO + tr, :] = cur_ref[...]
    n = tr + HALO - 8
    for b in range(1, 8):
        e_ref[b, 0:n, :] = e_ref[0, b:b + n, :]
    first = HALO - (CONV_WIDTH - 1)

    def chunk(ci, carry):
        r0 = pl.multiple_of(ci * rc, rc)
        acc = jnp.zeros((rc, CONV_CH), F32) + cb_ref[...]
        for w in range(CONV_WIDTH):
            s = first + w
            acc = acc + e_ref[s % 8, pl.ds(r0 + 8 * (s // 8), rc), :] * cw_ref[pl.ds(w, 1), :]
        c_ref[pl.ds(r0, rc), :] = acc
        return carry

    lax.fori_loop(0, tr // rc, chunk, 0)
    c = c_ref[...]
    mu = jnp.mean(c, axis=-1, keepdims=True)
    xc = c - mu
    y = xc * lax.rsqrt(jnp.mean(xc * xc, axis=-1, keepdims=True) + EPS) * lg_ref[...] + lb_ref[...]
    act = (y * jax.nn.sigmoid(y)).astype(BF16)
    out_ref[...] = jnp.dot(act, wo_ref[...], preferred_element_type=F32).astype(BF16)


def _conv_branch(a_flat, row_off, nb, t, left_ctx, cw, cb, lg, lb, wo, tr):
    nt = t // tr
    rc = min(tr, 32)
    per_batch_ctx = left_ctx.shape[0] > 1
    cur = pl.BlockSpec((tr, CONV_CH), lambda b, i: (row_off // tr + b * nt + i, 0))
    halo = pl.BlockSpec((HALO, CONV_CH),
                        lambda b, i: (jnp.maximum((row_off + b * t + i * tr) // HALO - 1, 0), 0))
    lctx = pl.BlockSpec((1, HALO, CONV_CH), lambda b, i: (b if per_batch_ctx else 0, 0, 0))
    return pl.pallas_call(
        functools.partial(_conv_kernel, tr=tr, rc=rc),
        out_shape=jax.ShapeDtypeStruct((nb * t, D_MODEL), BF16),
        grid=(nb, nt),
        in_specs=[cur, halo, lctx, _resident(cw), _resident(cb), _resident(lg), _resident(lb), _resident(wo)],
        out_specs=pl.BlockSpec((tr, D_MODEL), lambda b, i: (b * nt + i, 0)),
        scratch_shapes=[pltpu.VMEM((8, tr + HALO, CONV_CH), F32), pltpu.VMEM((tr, CONV_CH), F32)],
        compiler_params=_cparams(("arbitrary", "arbitrary")),
        name="conv",
    )(a_flat, a_flat, left_ctx, cw, cb, lg, lb, wo)


def _float_to_skey(f):
    bits = lax.bitcast_convert_type(f, I32)
    return bits ^ ((bits >> 31) & 0x7FFFFFFF)


def _skey_to_float(key):
    return lax.bitcast_convert_type(key ^ ((key >> 31) & 0x7FFFFFFF), F32)


SKEY_LOWEST_FINITE = INT_MIN + 0x00800000


def _high_half(f):
    return lax.bitcast_convert_type(lax.bitcast_convert_type(f, I32) & -65536, F32)


def _key_to_float(u):
    key = u ^ INT_MIN
    bits = key ^ ((key >> 31) & 0x7FFFFFFF)
    return lax.bitcast_convert_type(bits, F32)


def _dsa_kernel(q_ref, qi_ref, kw_ref, k_ref, v_ref, ki_ref, o_ref, sc_ref, m_ref, acc_ref,
                *, tq, tk, k_sel, q_off, n_valid_end, idx_bits):
    i = pl.program_id(1)
    qf0 = q_off + i * tq
    klim = jnp.minimum(FRONT + ((qf0 + tq - 1) // CHUNK + 1) * CHUNK, n_valid_end)
    n_tiles = (klim + tk - 1) // tk
    ncol = tk // LANES
    kf = float(k_sel)
    group = N_HEADS // N_KV_HEADS

    kwv = kw_ref[...]
    qrow = lax.broadcasted_iota(I32, (tq, 1), 0)
    lim_row = jnp.minimum(FRONT + ((qf0 + qrow) // CHUNK + 1) * CHUNK, n_valid_end)

    def score_tile(t, carry):
        k0 = pl.multiple_of(t * tk, tk)
        kt = ki_ref[0, pl.ds(k0, tk), :]
        sc = jnp.zeros((tq, tk), F32)
        for h in range(IDX_HEADS):
            rel = lax.dot_general(qi_ref[0, h], kt, (((1,), (1,)), ((), ())), preferred_element_type=F32)
            sc = sc + kwv[:, IDX_DIM + h:IDX_DIM + h + 1] * jnp.maximum(rel, 0.0)
        j = k0 + lax.broadcasted_iota(I32, (tq, tk), 1)
        j = jnp.where(j >= FRONT - N_META, j, n_valid_end)
        sc_ref[:, pl.ds(k0, tk)] = jnp.where(j < lim_row, sc, NEG_INF)
        return carry

    lax.fori_loop(0, n_tiles, score_tile, 0)

    lane = lax.broadcasted_iota(I32, (tq, LANES), 1)

    def count(pred):
        def body(t, acc):
            k0 = pl.multiple_of(t * tk, tk)
            for c in range(ncol):
                blk = sc_ref[:, pl.ds(k0 + c * LANES, LANES)]
                acc = acc + pred(blk, k0 + c * LANES)
            return acc
        acc = lax.fori_loop(0, n_tiles, body, jnp.zeros((tq, LANES), F32))
        return jnp.broadcast_to(jnp.sum(acc, axis=1, keepdims=True), (tq, LANES))

    def bisect(it, u):
        trial = u | jnp.left_shift(jnp.int32(1), 31 - it)
        thr = _key_to_float(trial)
        cnt = count(lambda blk, j0: jnp.where(blk >= thr, 1.0, 0.0))
        return jnp.where(cnt >= kf, trial, u)

    u = lax.fori_loop(0, 32, bisect, jnp.zeros((tq, LANES), I32))
    key = jnp.maximum(u ^ INT_MIN, INT_MIN + 0x00800000)
    thr = _key_to_float(key ^ INT_MIN)
    n_ge = count(lambda blk, j0: jnp.where(blk >= thr, 1.0, 0.0))
    tied = jnp.max(n_ge) > kf

    def write_bias(sel):
        def body(t, carry):
            k0 = pl.multiple_of(t * tk, tk)
            for c in range(ncol):
                blk = sc_ref[:, pl.ds(k0 + c * LANES, LANES)]
                sc_ref[:, pl.ds(k0 + c * LANES, LANES)] = sel(blk, k0 + c * LANES)
            return carry
        lax.fori_loop(0, n_tiles, body, 0)

    @pl.when(jnp.logical_not(tied))
    def _():
        write_bias(lambda blk, j0: jnp.where(blk >= thr, 0.0, NEG_INF))

    @pl.when(tied)
    def _():
        need = kf - count(lambda blk, j0: jnp.where(blk > thr, 1.0, 0.0))

        def jbis(it, ans):
            trial = ans + jnp.left_shift(jnp.int32(1), idx_bits - 1 - it)
            cnt = count(lambda blk, j0: jnp.where(blk == thr, jnp.where(j0 + lane < trial, 1.0, 0.0), 0.0))
            return jnp.where(cnt < need, trial, ans)

        ans = lax.fori_loop(0, idx_bits, jbis, jnp.zeros((tq, LANES), I32))
        write_bias(lambda blk, j0: jnp.where(
            blk > thr, 0.0, jnp.where(blk == thr, jnp.where(j0 + lane <= ans, 0.0, NEG_INF), NEG_INF)))

    m_ref[...] = jnp.full(m_ref.shape, M_INIT, F32)
    acc_ref[...] = jnp.zeros(acc_ref.shape, F32)

    def attend_tile(t, carry):
        k0 = pl.multiple_of(t * tk, tk)
        bias = sc_ref[:, pl.ds(k0, tk)]
        for g in range(N_KV_HEADS):
            qg = q_ref[0, g * group:(g + 1) * group].reshape(group * tq, HEAD_DIM)
            s = lax.dot_general(qg, k_ref[0, g, pl.ds(k0, tk), :], (((1,), (1,)), ((), ())),
                                preferred_element_type=F32)
            s = (s.reshape(group, tq, tk) + bias[None]).reshape(group * tq, tk)
            m_prev = m_ref[g]
            m_new = jnp.maximum(m_prev, jnp.max(s, axis=1, keepdims=True))
            p = jnp.exp2(s - jnp.concatenate([m_new] * ncol, axis=1))
            pv = jnp.dot(p.astype(BF16), v_ref[0, g, pl.ds(k0, tk), :], preferred_element_type=F32)
            acc_ref[g] = jnp.exp2(m_prev - m_new) * acc_ref[g] + pv
            m_ref[g] = m_new
        return carry

    lax.fori_loop(0, n_tiles, attend_tile, 0)

    for g in range(N_KV_HEADS):
        acc = acc_ref[g]
        o = acc[:, :HEAD_DIM] / acc[:, HEAD_DIM:HEAD_DIM + 1]
        for r in range(0, group, 2):
            pair = jnp.concatenate([o[r * tq:(r + 1) * tq], o[(r + 1) * tq:(r + 2) * tq]], axis=1)
            h0 = g * group + r
            o_ref[:, h0 * HEAD_DIM:(h0 + 2) * HEAD_DIM] = pair.astype(BF16)


def _dsa(qh, qih, kw_flat, kw_row_off, kh, vh, kih, *, tq, tk, k_sel, q_off, n_valid_end):
    nb, _, s, _ = qh.shape
    lp = kh.shape[2]
    nq = s // tq
    kern = functools.partial(_dsa_kernel, tq=tq, tk=tk, k_sel=k_sel, q_off=q_off, n_valid_end=n_valid_end,
                             idx_bits=max(1, int(lp).bit_length()))
    per_batch = lambda shp: pl.BlockSpec((1,) + shp, lambda b, i: (b,) + (0,) * len(shp),
                                         pipeline_mode=pl.Buffered(2 if nq == 1 else 1))
    return pl.pallas_call(
        kern,
        out_shape=jax.ShapeDtypeStruct((nb * s, N_HEADS * HEAD_DIM), BF16),
        grid=(nb, nq),
        in_specs=[pl.BlockSpec((1, N_HEADS, tq, HEAD_DIM), lambda b, i: (b, 0, i, 0)),
                  pl.BlockSpec((1, IDX_HEADS, tq, IDX_DIM), lambda b, i: (b, 0, i, 0)),
                  pl.BlockSpec((tq, LANES), lambda b, i: (kw_row_off // tq + b * nq + i, 0)),
                  per_batch((N_KV_HEADS, lp, HEAD_DIM)), per_batch((N_KV_HEADS, lp, LANES)), per_batch((lp, IDX_DIM))],
        out_specs=pl.BlockSpec((tq, N_HEADS * HEAD_DIM), lambda b, i: (b * nq + i, 0)),
        scratch_shapes=[pltpu.VMEM((tq, (lp // LANES | 1) * LANES), F32),
                        pltpu.VMEM((N_KV_HEADS, (N_HEADS // N_KV_HEADS) * tq, LANES), F32),
                        pltpu.VMEM((N_KV_HEADS, (N_HEADS // N_KV_HEADS) * tq, LANES), F32)],
        compiler_params=_cparams(("arbitrary", "arbitrary")),
        name="dsa",
    )(qh, qih, kw_flat, kh, vh, kih)


def _dsa_t_kernel(q_ref, qi_ref, wt_ref, k_ref, vt_ref, ki_ref, o_ref, sc_ref, sch_ref, m_ref, mu_ref, acc_ref,
                  s_ref, s2_ref,
                  *, tk, k_sel, n_valid_end, idx_bits):
    tq = LANES
    i = pl.program_id(1)
    qf0 = i * tq
    klim = jnp.minimum(FRONT + qf0 + tq, n_valid_end)
    n_tiles = (klim + tk - 1) // tk
    kf = float(k_sel)
    group = N_HEADS // N_KV_HEADS
    fold = 32
    fold_h = 64
    assert sch_ref.shape[0] // fold_h <= 256

    wt = wt_ref[0]
    qlane = lax.broadcasted_iota(I32, (1, tq), 1)
    lim = jnp.minimum(FRONT + ((qf0 + qlane) // CHUNK + 1) * CHUNK, n_valid_end)
    qi_all = qi_ref[0].reshape(IDX_HEADS * tq, IDX_DIM)

    def score_tile(t, carry):
        th = tk // 2
        for half in range(2):
            k0 = pl.multiple_of(t * tk, tk) + half * th
            rel = lax.dot_general(ki_ref[0, pl.ds(k0, th), :], qi_all, (((1,), (1,)), ((), ())),
                                  preferred_element_type=F32)
            sc = jnp.zeros((th, tq), F32)
            for h in range(IDX_HEADS):
                sc = sc + wt[h:h + 1, :] * jnp.maximum(rel[:, h * tq:(h + 1) * tq], 0.0)
            j = k0 + lax.broadcasted_iota(I32, (th, tq), 0)
            j = jnp.where(j >= FRONT - N_META, j, n_valid_end)
            sc = jnp.where(j < lim, sc, NEG_INF)
            sc_ref[pl.ds(k0, th), :] = sc
            sch_ref[pl.ds(k0, th), :] = _high_half(sc).astype(HALF)
        return carry

    lax.fori_loop(0, n_tiles, score_tile, 0)

    def count(pred):
        def body(t, acc):
            k0 = pl.multiple_of(t * tk, tk)
            ones = pred(sc_ref[pl.ds(k0, tk), :], k0)
            return acc + jnp.sum(ones.reshape(tk // fold, fold, tq), axis=0)
        acc = lax.fori_loop(0, n_tiles, body, jnp.zeros((fold, tq), F32))
        return jnp.sum(acc, axis=0, keepdims=True)

    def count_high(thr):
        thr_h = _high_half(thr).astype(HALF)
        one, zero = jnp.ones((), HALF), jnp.zeros((), HALF)

        def body(t, acc):
            ones = jnp.where(sch_ref[pl.ds(pl.multiple_of(t * tk, tk), tk), :] >= thr_h, one, zero)
            for c in range(tk // fold_h):
                acc = acc + ones[c * fold_h:(c + 1) * fold_h]
            return acc
        acc = lax.fori_loop(0, n_tiles, body, jnp.zeros((fold_h, tq), HALF))
        return jnp.sum(acc.astype(F32), axis=0, keepdims=True)

    def bisect(counter):
        def step(it, carry):
            u, c_u = carry
            trial = u | jnp.left_shift(jnp.int32(1), 31 - it)
            cnt = counter(_key_to_float(trial))
            take = cnt >= kf
            return jnp.where(take, trial, u), jnp.where(take, cnt, c_u)
        return step

    n_adm = (lim - (FRONT - N_META)).astype(F32)
    low_step = bisect(lambda thr: count(lambda blk, k0: jnp.where(blk >= thr, 1.0, 0.0)))

    def unsettled(c_u):
        return jnp.max(jnp.where((c_u == kf) | (n_adm <= kf), 0, 1))

    def low_bits(state):
        it, _, u, c_u = state
        for j in range(LOW_GROUP):
            u, c_u = low_step(it + j, (u, c_u))
        return it + LOW_GROUP, unsettled(c_u), u, c_u

    u, c_u = lax.fori_loop(0, 16, bisect(count_high), (jnp.zeros((1, tq), I32), jnp.full((1, tq), jnp.inf, F32)))
    _, _, u, c_u = lax.while_loop(lambda state: (state[0] < 32) & (state[1] > 0), low_bits,
                                  (jnp.int32(16), unsettled(c_u), u, c_u))
    key = jnp.maximum(u ^ INT_MIN, SKEY_LOWEST_FINITE)
    thr = _skey_to_float(key)
    n_ge = jnp.minimum(c_u, n_adm)
    tied = jnp.max(n_ge) > kf

    def write_bias(sel):
        def body(t, carry):
            k0 = pl.multiple_of(t * tk, tk)
            sc_ref[pl.ds(k0, tk), :] = sel(sc_ref[pl.ds(k0, tk), :], k0)
            return carry
        lax.fori_loop(0, n_tiles, body, 0)

    @pl.when(jnp.logical_not(tied))
    def _():
        write_bias(lambda blk, k0: jnp.where(blk >= thr, 0.0, NEG_INF))

    @pl.when(tied)
    def _():
        need = kf - count(lambda blk, k0: jnp.where(blk > thr, 1.0, 0.0))
        krow = lax.broadcasted_iota(I32, (tk, tq), 0)

        def jbis(it, ans):
            trial = ans + jnp.left_shift(jnp.int32(1), idx_bits - 1 - it)
            cnt = count(lambda blk, k0: jnp.where(blk == thr, jnp.where(k0 + krow < trial, 1.0, 0.0), 0.0))
            return jnp.where(cnt < need, trial, ans)

        ans = lax.fori_loop(0, idx_bits, jbis, jnp.zeros((1, tq), I32))
        write_bias(lambda blk, k0: jnp.where(
            blk > thr, 0.0, jnp.where(blk == thr, jnp.where(k0 + krow <= ans, 0.0, NEG_INF), NEG_INF)))

    m_ref[...] = jnp.full(m_ref.shape, M_INIT, F32)
    acc_ref[...] = jnp.zeros(acc_ref.shape, F32)

    mu_ref[...] = jnp.full(mu_ref.shape, M_INIT, F32)

    def logits(t, buf, g, m_before):
        k0 = pl.multiple_of(t * tk, tk)
        bias = sc_ref[pl.ds(k0, tk), :]
        qg = q_ref[0, g * group:(g + 1) * group].reshape(group * tq, HEAD_DIM)
        s = lax.dot_general(k_ref[0, g, pl.ds(k0, tk), :], qg, (((1,), (1,)), ((), ())),
                            preferred_element_type=F32)
        s = jnp.concatenate([s[:, r * tq:(r + 1) * tq] + bias for r in range(group)], axis=1)
        buf[g] = s
        return jnp.maximum(m_before, jnp.max(s, axis=0, keepdims=True))

    def values(t, buf, g, m_t):
        p = jnp.exp2(buf[g] - m_t)
        pv = jnp.dot(vt_ref[0, g, :, pl.ds(pl.multiple_of(t * tk, tk), tk)], p.astype(BF16),
                     preferred_element_type=F32)
        acc_ref[g] = jnp.exp2(mu_ref[g] - m_t) * acc_ref[g] + pv
        mu_ref[g] = m_t

    def step(t_next, buf_next, t, buf):
        for g in range(N_KV_HEADS):
            m_t = m_ref[g]
            m_ref[g] = logits(t_next, buf_next, g, m_t)
            values(t, buf, g, m_t)

    for g in range(N_KV_HEADS):
        m_ref[g] = logits(0, s_ref, g, m_ref[g])

    def tile_pair(j, carry):
        step(2 * j + 1, s2_ref, 2 * j, s_ref)
        step(jnp.minimum(2 * j + 2, n_tiles - 1), s_ref, 2 * j + 1, s2_ref)
        return carry

    lax.fori_loop(0, n_tiles >> 1, tile_pair, 0)

    @pl.when((n_tiles & 1) == 1)
    def _():
        for g in range(N_KV_HEADS):
            values(n_tiles - 1, s_ref, g, m_ref[g])

    for g in range(N_KV_HEADS):
        acc = acc_ref[g]
        o = acc[:HEAD_DIM, :] / acc[HEAD_DIM:HEAD_DIM + 1, :]
        for r in range(0, group, 2):
            pair = jnp.concatenate([o[:, r * tq:(r + 1) * tq], o[:, (r + 1) * tq:(r + 2) * tq]], axis=0)
            h0 = g * group + r
            o_ref[:, h0 * HEAD_DIM:(h0 + 2) * HEAD_DIM] = pair.T.astype(BF16)


def _dsa_t(qh, qih, wt, kh, vt, kih, *, tk, k_sel, n_valid_end):
    nb, _, s, _ = qh.shape
    lp = kh.shape[2]
    tq = LANES
    nq = s // tq
    group = N_HEADS // N_KV_HEADS
    kern = functools.partial(_dsa_t_kernel, tk=tk, k_sel=k_sel, n_valid_end=n_valid_end,
                             idx_bits=max(1, int(lp).bit_length()))
    per_batch = lambda shp: pl.BlockSpec((1,) + shp, lambda b, i: (b,) + (0,) * len(shp), pipeline_mode=pl.Buffered(1))
    return pl.pallas_call(
        kern,
        out_shape=jax.ShapeDtypeStruct((nb * s, N_HEADS * HEAD_DIM), BF16),
        grid=(nb, nq),
        in_specs=[pl.BlockSpec((1, N_HEADS, tq, HEAD_DIM), lambda b, i: (b, 0, i, 0)),
                  pl.BlockSpec((1, IDX_HEADS, tq, IDX_DIM), lambda b, i: (b, 0, i, 0)),
                  pl.BlockSpec((1, IDX_HEADS, tq), lambda b, i: (b, 0, i)),
                  per_batch((N_KV_HEADS, lp, HEAD_DIM)), per_batch((N_KV_HEADS, VT_ROWS, lp)), per_batch((lp, IDX_DIM))],
        out_specs=pl.BlockSpec((tq, N_HEADS * HEAD_DIM), lambda b, i: (b * nq + i, 0)),
        scratch_shapes=[pltpu.VMEM((lp, tq), F32), pltpu.VMEM((lp, tq), HALF),
                        pltpu.VMEM((N_KV_HEADS, 1, group * tq), F32),
                        pltpu.VMEM((N_KV_HEADS, 1, group * tq), F32),
                        pltpu.VMEM((N_KV_HEADS, VT_ROWS, group * tq), F32),
                        pltpu.VMEM((N_KV_HEADS, tk, group * tq), F32),
                        pltpu.VMEM((N_KV_HEADS, tk, group * tq), F32)],
        compiler_params=_cparams(("arbitrary", "arbitrary")),
        name="dsa_t",
    )(qh, qih, wt, kh, vt, kih)


def _pad_keys(meta_rows, mid_rows, lp):
    nb = mid_rows[0].shape[0]
    tail = mid_rows[0].shape[2:]
    meta_b = jnp.broadcast_to(meta_rows[None], (nb,) + meta_rows.shape)
    used = FRONT + sum(m.shape[1] for m in mid_rows)
    parts = [jnp.zeros((nb, FRONT - N_META) + tail, meta_rows.dtype), meta_b] + list(mid_rows)
    if lp > used:
        parts.append(jnp.zeros((nb, lp - used) + tail, meta_rows.dtype))
    return jnp.concatenate(parts, axis=1)


def _attention_inputs(k_meta, v_meta, ki_meta, k_mid, v_mid, ki_mid, lp, keys_on_lanes):
    nb = k_mid[0].shape[0]
    kp = _pad_keys(k_meta, k_mid, lp).astype(BF16).reshape(nb, lp, N_KV_HEADS, HEAD_DIM).transpose(0, 2, 1, 3)
    vp = _pad_keys(v_meta, v_mid, lp).astype(BF16).reshape(nb, lp, N_KV_HEADS, HEAD_DIM)
    if keys_on_lanes:
        vp = vp.transpose(0, 2, 3, 1)
        ones = jnp.ones((nb, N_KV_HEADS, 1, lp), BF16)
        vp = jnp.concatenate([vp, ones, jnp.zeros((nb, N_KV_HEADS, VT_ROWS - HEAD_DIM - 1, lp), BF16)], axis=2)
    else:
        vp = vp.transpose(0, 2, 1, 3)
        ones = jnp.ones(vp.shape[:3] + (1,), BF16)
        vp = jnp.concatenate([vp, ones, jnp.zeros(vp.shape[:3] + (LANES - HEAD_DIM - 1,), BF16)], axis=3)
    kip = _pad_keys(ki_meta, ki_mid, lp).astype(BF16)
    return kp, vp, kip


def _merge_kernel(xp_ref, xs_ref, cp_ref, cs_ref, ap_ref, as_ref, gp_ref, gs_ref, wao_ref, wout_ref, gffn_ref, wr_ref,
                  br_ref, x1_ref, hn_ref, ids_ref, gts_ref, *, n_first):
    first = pl.program_id(0) < n_first
    x = jnp.where(first, xp_ref[...], xs_ref[...])
    conv = jnp.where(first, cp_ref[...], cs_ref[...]).astype(F32)
    attn = jnp.where(first, ap_ref[...], as_ref[...])
    ao = jnp.dot(attn, wao_ref[...], preferred_element_type=F32)
    gates = jnp.where(first, gp_ref[...], gs_ref[...]).astype(F32)
    mix = gates[:, :D_MODEL] * conv + gates[:, D_MODEL:] * ao
    x1 = x + jnp.dot(mix.astype(BF16), wout_ref[...], preferred_element_type=F32)
    x1_ref[...] = x1
    hn = x1 * lax.rsqrt(jnp.mean(x1 * x1, axis=-1, keepdims=True) + EPS) * gffn_ref[...]
    tr = hn.shape[0]
    for c in range(SUB):
        hn_ref[pl.ds(c, tr, stride=SUB), :] = hn[:, c * LANES:(c + 1) * LANES]
    logits = jnp.dot(hn, wr_ref[...], preferred_element_type=F32, precision=lax.Precision.HIGHEST) + br_ref[...]
    lane = lax.broadcasted_iota(I32, logits.shape, 1)
    lg = jnp.where(lane < N_EXPERTS, logits, NEG_INF)
    vals, ids = [], []
    for _ in range(TOP_K_EXPERTS):
        m = jnp.max(lg, axis=1, keepdims=True)
        idx = jnp.min(jnp.where(lg == m, lane, LANES), axis=1, keepdims=True)
        vals.append(m)
        ids.append(idx)
        lg = jnp.where(lane == idx, NEG_INF, lg)
    es = [jnp.exp(v - vals[0]) for v in vals]
    den = es[0]
    for e in es[1:]:
        den = den + e
    ids_out = jnp.zeros(logits.shape, I32)
    gts_out = jnp.zeros(logits.shape, F32)
    for j in range(TOP_K_EXPERTS):
        ids_out = jnp.where(lane == j, ids[j], ids_out)
        gts_out = jnp.where(lane == j, es[j] / den, gts_out)
    ids_ref[...] = ids_out[:, :SUB]
    gts_ref[...] = gts_out


def _merge(xp, xs, conv_p, conv_s, attn_p, attn_s, gates_p, gates_s, wao, wout, gffn, wr, br, tr):
    n_first, n_second = xp.shape[0] // tr, xs.shape[0] // tr
    rows = xp.shape[0] + xs.shape[0]
    first = lambda w: pl.BlockSpec((tr, w), lambda i: (jnp.minimum(i, n_first - 1), 0))
    second = lambda w: pl.BlockSpec((tr, w), lambda i: (jnp.maximum(i - n_first, 0), 0))
    row = lambda w: pl.BlockSpec((tr, w), lambda i: (i, 0))
    return pl.pallas_call(
        functools.partial(_merge_kernel, n_first=n_first),
        out_shape=[jax.ShapeDtypeStruct((rows, D_MODEL), F32), jax.ShapeDtypeStruct((rows * SUB, LANES), F32),
                   jax.ShapeDtypeStruct((rows, SUB), I32), jax.ShapeDtypeStruct((rows, LANES), F32)],
        grid=(n_first + n_second,),
        in_specs=[first(D_MODEL), second(D_MODEL), first(D_MODEL), second(D_MODEL), first(D_MODEL), second(D_MODEL),
                  first(2 * D_MODEL), second(2 * D_MODEL), _resident(wao), _resident(wout), _resident(gffn),
                  _resident(wr), _resident(br)],
        out_specs=[row(D_MODEL), pl.BlockSpec((tr * SUB, LANES), lambda i: (i, 0)), row(SUB), row(LANES)],
        compiler_params=_cparams(("arbitrary",)),
        name="merge",
    )(xp, xs, conv_p, conv_s, attn_p, attn_s, gates_p, gates_s, wao, wout, gffn, wr, br)


KEY_ALIGN = 1024


def _key_window(tm):
    return -(-(tm + KEY_ALIGN - 1) // KEY_ALIGN) * KEY_ALIGN


def _for_rows(n, fn):
    def group(gi, carry):
        for u in range(SUB):
            fn(gi * SUB + u)
        return carry

    def single(r, carry):
        fn(r)
        return carry

    shift = SUB.bit_length() - 1
    lax.fori_loop(0, n >> shift, group, 0)
    lax.fori_loop((n >> shift) << shift, n, single, 0)


def _moe_kernel(te_ref, base_ref, nval_ref, nreal_ref, src_hbm, dst_hbm, h_hbm, w1_ref, b1_ref, w3_ref, b3_ref, w2_ref,
                b2_ref, y_hbm, src_sm, dst_sm, xbuf, xb, ybuf, wb, ksem, gsem, ssem, *, tm):
    i = pl.program_id(0)
    nreal = nreal_ref[0]
    window = src_sm.shape[0] // 3

    def rows_copy(tile):
        s = lax.rem(tile, 3)
        start = pl.multiple_of(base_ref[tile] & -KEY_ALIGN, KEY_ALIGN)
        slot = pl.ds(pl.multiple_of(s * window, KEY_ALIGN), window)
        return (pltpu.make_async_copy(src_hbm.at[pl.ds(start, window)], src_sm.at[slot], ksem.at[0, s]),
                pltpu.make_async_copy(dst_hbm.at[pl.ds(start, window)], dst_sm.at[slot], ksem.at[1, s]))

    def keys_start(tile):
        for copy in rows_copy(tile):
            copy.start()

    def keys_wait(tile):
        for copy in rows_copy(tile):
            copy.wait()

    def row_reader(table, tile):
        first = lax.rem(tile, 3) * window + (base_ref[tile] & (KEY_ALIGN - 1))
        return lambda r: pl.multiple_of(table[first + r], SUB)

    def gather_row(s, r, src_row):
        return pltpu.make_async_copy(h_hbm.at[pl.ds(src_row, SUB)], xbuf.at[s, pl.ds(r * SUB, SUB)], gsem.at[s])

    def scatter_row(r, dst_row):
        return pltpu.make_async_copy(ybuf.at[pl.ds(r * SUB, SUB)], y_hbm.at[pl.ds(dst_row, SUB)], ssem.at[0])

    def start_gather(tile):
        src_of, s = row_reader(src_sm, tile), tile & 1
        _for_rows(tm, lambda r: gather_row(s, r, src_of(r)).start())

    @pl.when(jnp.logical_and(i == 0, nreal > 0))
    def _():
        keys_start(0)

        @pl.when(nreal > 1)
        def _():
            keys_start(1)

        keys_wait(0)
        start_gather(0)

    @pl.when(i + 1 < nreal)
    def _():
        keys_wait(i + 1)
        start_gather(i + 1)

    @pl.when(i + 2 < nreal)
    def _():
        keys_start(i + 2)

    @pl.when(i < nreal)
    def _():
        s = i & 1
        _for_rows(tm, lambda r: gather_row(s, r, 0).wait())
        for c in range(SUB):
            xb[:, c * LANES:(c + 1) * LANES] = xbuf[s, pl.ds(c, tm, stride=SUB), :].astype(BF16)
        @pl.when(jnp.logical_or(i == 0, te_ref[i] != te_ref[jnp.maximum(i - 1, 0)]))
        def _():
            for j, w_ref in enumerate((w1_ref, w3_ref, w2_ref)):
                wb[j] = w_ref[0].astype(BF16)

        x = xb[...]
        g = jnp.minimum(jnp.dot(x, wb[0], preferred_element_type=F32) + b1_ref[0], SWIGLU_LIMIT)
        u = jnp.clip(jnp.dot(x, wb[1], preferred_element_type=F32) + b3_ref[0], -SWIGLU_LIMIT, SWIGLU_LIMIT)
        act = g * jax.nn.sigmoid(SWIGLU_ALPHA * g) * (u + 1.0)
        y = jnp.dot(act.astype(BF16), wb[2], preferred_element_type=F32) + b2_ref[0]

        @pl.when(i > 0)
        def _():
            _for_rows(nval_ref[i - 1], lambda r: scatter_row(r, 0).wait())

        for c in range(SUB):
            ybuf[pl.ds(c, tm, stride=SUB), :] = y[:, c * LANES:(c + 1) * LANES]

        dst_of = row_reader(dst_sm, i)
        _for_rows(nval_ref[i], lambda r: scatter_row(r, dst_of(r)).start())

        @pl.when(i == nreal - 1)
        def _():
            _for_rows(nval_ref[i], lambda r: scatter_row(r, 0).wait())


def _moe(hn, src_row, dst_row, tile_expert, base, nval, n_real, w1, b1, w3, b3, w2, b2, n_tok, tm):
    nt = tile_expert.shape[0]
    by_expert = lambda i, te, *_: (te[i], 0, 0)
    wspec = pl.BlockSpec((1, D_MODEL, D_FF), by_expert)
    w2spec = pl.BlockSpec((1, D_FF, D_MODEL), by_expert)
    bspec = pl.BlockSpec((1, 1, D_FF), by_expert)
    any_spec = pl.BlockSpec(memory_space=pl.ANY)
    grid_spec = pltpu.PrefetchScalarGridSpec(
        num_scalar_prefetch=4,
        grid=(nt,),
        in_specs=[any_spec, any_spec, any_spec, wspec, bspec, wspec, bspec, w2spec, bspec],
        out_specs=any_spec,
        scratch_shapes=[pltpu.SMEM((3 * _key_window(tm),), I32), pltpu.SMEM((3 * _key_window(tm),), I32),
                        pltpu.VMEM((2, tm * SUB, LANES), F32), pltpu.VMEM((tm, D_MODEL), BF16),
                        pltpu.VMEM((tm * SUB, LANES), F32), pltpu.VMEM((3, D_MODEL, D_FF), BF16),
                        pltpu.SemaphoreType.DMA((2, 3)), pltpu.SemaphoreType.DMA((2,)), pltpu.SemaphoreType.DMA((1,))],
    )
    return pl.pallas_call(
        functools.partial(_moe_kernel, tm=tm),
        out_shape=jax.ShapeDtypeStruct((TOP_K_EXPERTS * n_tok * SUB, LANES), F32),
        grid_spec=grid_spec,
        compiler_params=_cparams(("arbitrary",)),
        name="moe",
    )(tile_expert, base, nval, n_real, src_row, dst_row, hn, w1, b1, w3, b3, w2, b2)


def _route(ids, n, tm):
    n_pairs = n * TOP_K_EXPERTS
    pair_bits = max(1, (n_pairs - 1).bit_length())
    assert pair_bits + (N_EXPERTS - 1).bit_length() <= 31 and tm % LANES == 0
    assert TOP_K_EXPERTS & (TOP_K_EXPERTS - 1) == 0
    pair = jnp.arange(n_pairs, dtype=I32)
    keys = jnp.sort(jnp.left_shift(ids.reshape(-1), pair_bits) | pair)
    starts = jnp.searchsorted(keys, jnp.left_shift(jnp.arange(N_EXPERTS + 1, dtype=I32), pair_bits),
                              method="compare_all").astype(I32)
    counts = starts[1:] - starts[:-1]
    tiles = (counts + tm - 1) // tm
    tile_end = jnp.cumsum(tiles)
    nt = -(-(n_pairs + N_EXPERTS * (tm - 1)) // tm)
    ti = jnp.arange(nt, dtype=I32)
    te = jnp.minimum(jnp.searchsorted(tile_end, ti, side="right", method="compare_all"), N_EXPERTS - 1).astype(I32)
    within = ti - (tile_end - tiles)[te]
    base = jnp.clip(starts[:-1][te] + within * tm, 0, n_pairs - 1).astype(I32)
    nval = jnp.clip(counts[te] - within * tm, 0, tm).astype(I32)
    sorted_pair = keys & ((1 << pair_bits) - 1)
    token, slot = sorted_pair // TOP_K_EXPERTS, sorted_pair % TOP_K_EXPERTS
    padded = -(-n_pairs // KEY_ALIGN) * KEY_ALIGN + _key_window(tm)
    src_row = jnp.pad(token * SUB, (0, padded - n_pairs))
    dst_row = jnp.pad((slot * n + token) * SUB, (0, padded - n_pairs))
    return src_row, dst_row, te, base, nval, tile_end[-1:].astype(I32)


def _final_kernel(x1_ref, y0_ref, y1_ref, y2_ref, y3_ref, gts_ref, g_ref, out_ref):
    gts = gts_ref[...]
    tr = gts.shape[0]
    cols = []
    for c in range(SUB):
        acc = x1_ref[:, c * LANES:(c + 1) * LANES]
        for j, y_ref in enumerate((y0_ref, y1_ref, y2_ref, y3_ref)):
            acc = acc + gts[:, j:j + 1] * y_ref[pl.ds(c, tr, stride=SUB), :]
        cols.append(acc)
    ss = cols[0] * cols[0]
    for acc in cols[1:]:
        ss = ss + acc * acc
    scale = lax.rsqrt(jnp.sum(ss, axis=-1, keepdims=True) / D_MODEL + EPS)
    for c, acc in enumerate(cols):
        out_ref[:, c * LANES:(c + 1) * LANES] = acc * scale * g_ref[:, c * LANES:(c + 1) * LANES]


def _final(x1, yslots, gts, g, tr):
    n = x1.shape[0]
    nblk = n // tr
    row = lambda w: pl.BlockSpec((tr, w), lambda i: (i, 0))
    slot = lambda j: pl.BlockSpec((tr * SUB, LANES), lambda i: (j * nblk + i, 0))
    return pl.pallas_call(
        _final_kernel,
        out_shape=jax.ShapeDtypeStruct((n, D_MODEL), F32),
        grid=(nblk,),
        in_specs=[row(D_MODEL)] + [slot(j) for j in range(TOP_K_EXPERTS)] + [row(LANES), _resident(g)],
        out_specs=row(D_MODEL),
        compiler_params=_cparams(("arbitrary",)),
        name="final",
    )(x1, yslots, yslots, yslots, yslots, gts, g)


def _tile(n, pref):
    t = min(n, pref)
    while n % t:
        t //= 2
    return t


def kernel(x_prompt, x_sample, cache_k, cache_v, cache_idx_k, state_conv, meta_tokens, g_norm_mix, w_in, conv_w,
           conv_b, conv_ln_g, conv_ln_b, w_conv_out, w_attn_out, w_out, g_norm_ffn, w_router, b_router,
           w1, b1, w3, b3, w2, b2, g_norm_final):
    nb, s_p, _ = x_prompt.shape
    db, s_n, _ = x_sample.shape
    past = cache_k.shape[1]
    rows_p, rows_s = nb * s_p, db * s_n
    n = rows_p + rows_s
    tr = _tile(rows_s, ROW_TILE)
    assert rows_p % tr == 0 and s_p % tr == 0 and tr % s_n == 0 and s_p % Q_TILE == 0

    cuts = [0, 2 * CONV_CH, N_HEADS * HEAD_DIM, N_KV_HEADS * HEAD_DIM, N_KV_HEADS * HEAD_DIM, IDX_HEADS * IDX_DIM,
            IDX_DIM + IDX_HEADS, 2 * D_MODEL]
    offs = [sum(cuts[:i + 1]) for i in range(len(cuts))]
    seg = [w_in[:, offs[i]:offs[i + 1]].astype(BF16) for i in range(len(cuts) - 1)]
    seg[5] = jnp.pad(seg[5], ((0, 0), (0, LANES - seg[5].shape[1])))
    g_mix = g_norm_mix.reshape(1, D_MODEL)

    tabs_p = _rope_tables(N_META + jnp.arange(s_p, dtype=I32))
    tabs_s = _rope_tables(jnp.tile(N_META + past + jnp.arange(s_n, dtype=I32), tr // s_n))
    tabs_m = _rope_tables(jnp.arange(N_META, dtype=I32))

    xp, xs = x_prompt.reshape(rows_p, D_MODEL), x_sample.reshape(rows_s, D_MODEL)
    a_p, qh_p, k_p, v_p, qih_p, kw_p, gates_p, kh_mid, vt_mid, ki_mid, wt_p = _project_seq(
        xp, nb, s_p, tabs_p, g_mix, seg, tr)
    a_s, q_s, k_s, v_s, qi_s, kw_s, gates_s = [t[:rows_s] for t in _project(xs, xs, tabs_s, lambda i: 0, g_mix, seg, tr)]
    meta = meta_tokens.astype(F32)
    a_m, _, k_m, v_m, _, kw_m, _ = _project(meta, meta, tabs_m, lambda i: 0, g_mix, seg, N_META)
    a_m, k_m, v_m, ki_m = a_m[:N_META], k_m[:N_META], v_m[:N_META], kw_m[:N_META, :IDX_DIM]

    cw = jnp.pad(conv_w, ((0, HALO - CONV_WIDTH), (0, 0)))
    cb, lg, lb = conv_b.reshape(1, -1), conv_ln_g.reshape(1, -1), conv_ln_b.reshape(1, -1)
    wco = w_conv_out.astype(BF16)
    ctx_p = jnp.concatenate([jnp.zeros((HALO - N_META, CONV_CH), F32), a_m], axis=0)[None]
    ctx_s = jnp.pad(state_conv, ((0, 0), (HALO - (CONV_WIDTH - 1), 0), (0, 0)))
    conv_p = _conv_branch(a_p, 0, nb, s_p, ctx_p, cw, cb, lg, lb, wco, _tile(s_p, 256))
    conv_s = _conv_branch(a_s, 0, db, s_n, ctx_s, cw, cb, lg, lb, wco, s_n)

    k_p3, v_p3 = k_p.reshape(nb, s_p, -1), v_p.reshape(nb, s_p, -1)
    ki_p3 = kw_p[:, :IDX_DIM].reshape(nb, s_p, IDX_DIM)
    k_s3, v_s3 = k_s.reshape(db, s_n, -1), v_s.reshape(db, s_n, -1)
    ki_s3 = kw_s[:, :IDX_DIM].reshape(db, s_n, IDX_DIM)

    tk_p = _tile(s_p, K_TILE_T)
    lp_p = -(-(FRONT + s_p) // tk_p) * tk_p
    front = lambda rows: jnp.pad(rows, ((FRONT - N_META, 0), (0, 0))).astype(BF16)
    per_head = lambda t: t.reshape(FRONT, N_KV_HEADS, HEAD_DIM).transpose(1, 0, 2)
    rep = lambda t: jnp.broadcast_to(t[None], (nb,) + t.shape)
    tail = lp_p - FRONT - s_p
    kp = jnp.concatenate([rep(per_head(front(k_m))), kh_mid, jnp.zeros((nb, N_KV_HEADS, tail, HEAD_DIM), BF16)], axis=2)
    vt_front = jnp.concatenate([per_head(front(v_m)).transpose(0, 2, 1), jnp.ones((N_KV_HEADS, 1, FRONT), BF16),
                                jnp.zeros((N_KV_HEADS, VT_ROWS - HEAD_DIM - 1, FRONT), BF16)], axis=1)
    vtp = jnp.concatenate([rep(vt_front), vt_mid, jnp.zeros((nb, N_KV_HEADS, VT_ROWS, tail), BF16)], axis=3)
    kip = jnp.concatenate([rep(front(ki_m)), ki_mid, jnp.zeros((nb, tail, IDX_DIM), BF16)], axis=1)
    attn_p = _dsa_t(qh_p, qih_p, wt_p, kp, vtp, kip, tk=tk_p, k_sel=min(TOPK_MAX, s_p // 4),
                    n_valid_end=FRONT + s_p)

    def heads(t, b_, s_, nh):
        return t.reshape(b_, s_, nh, -1).transpose(0, 2, 1, 3)

    tk_s = K_TILE
    lp_s = -(-(FRONT + past + s_n) // tk_s) * tk_s
    ck = cache_k.reshape(db, past, -1)
    cv = cache_v.reshape(db, past, -1)
    ks, vs, kis = _attention_inputs(k_m, v_m, ki_m, [ck, k_s3], [cv, v_s3], [cache_idx_k, ki_s3], lp_s, False)
    attn_s = _dsa(heads(q_s, db, s_n, N_HEADS), heads(qi_s, db, s_n, IDX_HEADS), kw_s, 0, ks, vs, kis,
                  tq=s_n, tk=tk_s, k_sel=min(TOPK_MAX, (past + s_n) // 4), q_off=past,
                  n_valid_end=FRONT + past + s_n)

    wr = jnp.pad(w_router, ((0, 0), (0, LANES - N_EXPERTS)))
    br = jnp.pad(b_router, (0, LANES - N_EXPERTS)).reshape(1, LANES)
    x1, hn, ids, gts = _merge(xp, xs, conv_p, conv_s, attn_p, attn_s, gates_p, gates_s, w_attn_out.astype(BF16),
                              w_out.astype(BF16), g_norm_ffn.reshape(1, -1), wr, br, tr)

    tm = MOE_TILE
    src_row, dst_row, tile_expert, base, nval, n_real = _route(ids[:, :TOP_K_EXPERTS], n, tm)
    yslots = _moe(hn, src_row, dst_row, tile_expert, base, nval, n_real, w1, b1.reshape(N_EXPERTS, 1, D_FF),
                  w3, b3.reshape(N_EXPERTS, 1, D_FF), w2, b2.reshape(N_EXPERTS, 1, D_MODEL), n, tm)
    out = _final(x1, yslots, gts, g_norm_final.reshape(1, -1), tr)

    y_prompt = out[:rows_p].reshape(nb, s_p, D_MODEL)
    y_sample = out[rows_p:].reshape(db, s_n, D_MODEL)
    lead = lambda m: jnp.broadcast_to(m[None], (nb,) + m.shape)
    k_prompt = jnp.concatenate([lead(k_m), k_p3], axis=1).reshape(nb, N_META + s_p, N_KV_HEADS, HEAD_DIM)
    v_prompt = jnp.concatenate([lead(v_m), v_p3], axis=1).reshape(nb, N_META + s_p, N_KV_HEADS, HEAD_DIM)
    idxk_prompt = jnp.concatenate([lead(ki_m), ki_p3], axis=1)
    a_p3 = a_p.reshape(nb, s_p, CONV_CH)
    a_s3 = a_s.reshape(db, s_n, CONV_CH)
    keep = CONV_WIDTH - 1
    conv_state_p = jnp.concatenate([lead(a_m), a_p3], axis=1)[:, N_META + s_p - keep:]
    conv_state_s = jnp.concatenate([state_conv, a_s3], axis=1)[:, s_n:]
    return (y_prompt, y_sample, k_prompt, v_prompt, idxk_prompt, conv_state_p,
            k_s3.reshape(db, s_n, N_KV_HEADS, HEAD_DIM), v_s3.reshape(db, s_n, N_KV_HEADS, HEAD_DIM), ki_s3,
            conv_state_s)
```

```python
import functools

import jax
import jax.numpy as jnp
from jax import lax
from jax.experimental import pallas as pl
from jax.experimental.pallas import tpu as pltpu

F32, BF16, I32 = jnp.float32, jnp.bfloat16, jnp.int32
HALF = jnp.bfloat16

D_MODEL = 1024
CHUNK = 64
N_META = 16
CONV_CH = D_MODEL
CONV_WIDTH = 31
N_HEADS = 16
N_KV_HEADS = 4
HEAD_DIM = 64
ROT_DIM = HEAD_DIM // 4
ROPE_THETA = 500000.0
IDX_HEADS = 8
IDX_DIM = 64
TOPK_MAX = 256
N_EXPERTS = 32
TOP_K_EXPERTS = 4
D_FF = D_MODEL
SWIGLU_LIMIT = 7.0
SWIGLU_ALPHA = 1.702
EPS = 1e-5
Q_SCALE = HEAD_DIM ** -0.5 * 1.4426950408889634

LANES = 128
SUB = 8
VT_ROWS = HEAD_DIM + 2 * SUB
FRONT = 128
HALO = 32
VMEM_LIMIT = 56 * 1024 * 1024
NEG_INF = float("-inf")
INT_MIN = -2 ** 31
M_INIT = -1e30

ROW_TILE = 512
Q_TILE = 128
K_TILE = 512
K_TILE_T = 1024
LOW_GROUP = 4
MOE_TILE = 512


def _cparams(sem):
    return pltpu.CompilerParams(dimension_semantics=sem, vmem_limit_bytes=VMEM_LIMIT)


def _resident(arr):
    nd = arr.ndim
    return pl.BlockSpec(arr.shape, lambda *_: (0,) * nd, pipeline_mode=pl.Buffered(1))


def _rope_cols(x, c, s1, s2):
    outs = []
    for j in range(x.shape[1] // LANES):
        xj = x[:, j * LANES:(j + 1) * LANES]
        outs.append(xj * c + pltpu.roll(xj, LANES - ROT_DIM // 2, 1) * s1 + pltpu.roll(xj, ROT_DIM // 2, 1) * s2)
    return outs


def _proj_kernel(xp_ref, xs_ref, g_ref, wa_ref, wq_ref, wk_ref, wv_ref, wqi_ref, wkw_ref, wgl_ref,
                 c_ref, s1_ref, s2_ref,
                 a_ref, q_ref, k_ref, v_ref, qi_ref, kw_ref, gate_ref, *, n_first):
    x = jnp.where(pl.program_id(0) < n_first, xp_ref[...], xs_ref[...])
    h = x * lax.rsqrt(jnp.mean(x * x, axis=-1, keepdims=True) + EPS) * g_ref[...]
    hb = h.astype(BF16)
    c, s1, s2 = c_ref[...], s1_ref[...], s2_ref[...]

    def dot(w_ref):
        return jnp.dot(hb, w_ref[...], preferred_element_type=F32)

    glu = dot(wa_ref)
    a_ref[...] = glu[:, :CONV_CH] * jax.nn.sigmoid(glu[:, CONV_CH:])
    for j, col in enumerate(_rope_cols(dot(wq_ref), c, s1, s2)):
        q_ref[:, j * LANES:(j + 1) * LANES] = (col * Q_SCALE).astype(BF16)
    for j, col in enumerate(_rope_cols(dot(wk_ref), c, s1, s2)):
        k_ref[:, j * LANES:(j + 1) * LANES] = col
    v_ref[...] = dot(wv_ref)
    for j, col in enumerate(_rope_cols(dot(wqi_ref), c, s1, s2)):
        qi_ref[:, j * LANES:(j + 1) * LANES] = (col * IDX_DIM ** -0.5).astype(BF16)
    pkw = dot(wkw_ref)
    lane = lax.broadcasted_iota(I32, pkw.shape, 1)
    kw_ref[...] = jnp.where(lane < IDX_DIM, _rope_cols(pkw, c, s1, s2)[0], pkw * IDX_HEADS ** -0.5)
    gate_ref[...] = jax.nn.sigmoid(dot(wgl_ref)).astype(BF16)


def _rope_tables(pos):
    half = ROT_DIM // 2
    inv = ROPE_THETA ** (-jnp.arange(half, dtype=F32) / half)
    ang = pos.astype(F32)[:, None] * inv[None, :]
    cos, sin = jnp.cos(ang), jnp.sin(ang)
    r = pos.shape[0]
    z = lambda n: jnp.zeros((r, n), F32)
    c = jnp.concatenate([cos, cos, jnp.ones((r, HEAD_DIM - ROT_DIM), F32)], axis=1)
    s1 = jnp.concatenate([-sin, z(HEAD_DIM - half)], axis=1)
    s2 = jnp.concatenate([z(half), sin, z(HEAD_DIM - ROT_DIM)], axis=1)
    two = lambda t: jnp.concatenate([t, t], axis=1)
    return two(c), two(s1), two(s2)


def _project(xp, xs, tabs, tab_map, g, ws, tr):
    n_first, n_second = xp.shape[0] // tr, xs.shape[0] // tr
    rows = xp.shape[0] + xs.shape[0]
    row = lambda w: pl.BlockSpec((tr, w), lambda i: (i, 0))
    tab = pl.BlockSpec((tr, LANES), lambda i: (tab_map(i), 0))
    in_specs = [pl.BlockSpec((tr, D_MODEL), lambda i: (jnp.minimum(i, n_first - 1), 0)),
                pl.BlockSpec((tr, D_MODEL), lambda i: (jnp.maximum(i - n_first, 0), 0)),
                _resident(g)] + [_resident(w) for w in ws] + [tab, tab, tab]
    widths = [(CONV_CH, F32), (N_HEADS * HEAD_DIM, BF16), (N_KV_HEADS * HEAD_DIM, F32), (N_KV_HEADS * HEAD_DIM, F32),
              (IDX_HEADS * IDX_DIM, BF16), (LANES, F32), (2 * D_MODEL, BF16)]
    return pl.pallas_call(
        functools.partial(_proj_kernel, n_first=n_first),
        out_shape=[jax.ShapeDtypeStruct((rows, w), dt) for w, dt in widths],
        grid=(n_first + n_second,),
        in_specs=in_specs,
        out_specs=[row(w) for w, _ in widths],
        compiler_params=_cparams(("arbitrary",)),
        name="proj",
    )(xp, xs, g, *ws, *tabs)


def _proj_seq_kernel(x_ref, g_ref, wa_ref, wq_ref, wk_ref, wv_ref, wqi_ref, wkw_ref, wgl_ref, c_ref, s1_ref, s2_ref,
                     a_ref, q_ref, k_ref, v_ref, qi_ref, kw_ref, gate_ref, kh_ref, vt_ref, kih_ref, wt_ref):
    x = x_ref[...]
    h = x * lax.rsqrt(jnp.mean(x * x, axis=-1, keepdims=True) + EPS) * g_ref[...]
    hb = h.astype(BF16)
    c, s1, s2 = c_ref[...], s1_ref[...], s2_ref[...]

    def dot(w_ref):
        return jnp.dot(hb, w_ref[...], preferred_element_type=F32)

    def put_heads(ref, j, col):
        ref[0, 2 * j] = col[:, :HEAD_DIM].astype(BF16)
        ref[0, 2 * j + 1] = col[:, HEAD_DIM:].astype(BF16)

    glu = dot(wa_ref)
    a_ref[...] = glu[:, :CONV_CH] * jax.nn.sigmoid(glu[:, CONV_CH:])
    for j, col in enumerate(_rope_cols(dot(wq_ref), c, s1, s2)):
        put_heads(q_ref, j, col * Q_SCALE)
    for j, col in enumerate(_rope_cols(dot(wk_ref), c, s1, s2)):
        k_ref[:, j * LANES:(j + 1) * LANES] = col
        put_heads(kh_ref, j, col)
    v = dot(wv_ref)
    v_ref[...] = v
    tr = v.shape[0]
    ones_row = jnp.where(lax.broadcasted_iota(I32, (VT_ROWS - HEAD_DIM, tr), 0) == 0, 1.0, 0.0).astype(BF16)
    for j in range(v.shape[1] // LANES):
        vt = v[:, j * LANES:(j + 1) * LANES].T
        for half in range(2):
            vt_ref[0, 2 * j + half, 0:HEAD_DIM, :] = vt[half * HEAD_DIM:(half + 1) * HEAD_DIM].astype(BF16)
            vt_ref[0, 2 * j + half, HEAD_DIM:VT_ROWS, :] = ones_row
    for j, col in enumerate(_rope_cols(dot(wqi_ref), c, s1, s2)):
        put_heads(qi_ref, j, col * IDX_DIM ** -0.5)
    pkw = dot(wkw_ref)
    lane = lax.broadcasted_iota(I32, pkw.shape, 1)
    kw = jnp.where(lane < IDX_DIM, _rope_cols(pkw, c, s1, s2)[0], pkw * IDX_HEADS ** -0.5)
    kw_ref[...] = kw[:, :IDX_DIM]
    kih_ref[0] = kw[:, :IDX_DIM].astype(BF16)
    wt_ref[0] = kw.T[IDX_DIM:IDX_DIM + IDX_HEADS, :]
    gate_ref[...] = jax.nn.sigmoid(dot(wgl_ref)).astype(BF16)


def _project_seq(x, nb, s, tabs, g, ws, tr):
    nt = s // tr
    rows = nb * s
    flat = lambda w: pl.BlockSpec((tr, w), lambda b, i: (b * nt + i, 0))
    tab = pl.BlockSpec((tr, LANES), lambda b, i: (i, 0))
    heads = lambda nh: pl.BlockSpec((1, nh, tr, HEAD_DIM), lambda b, i: (b, 0, i, 0))
    widths = [(CONV_CH, F32), None, (N_KV_HEADS * HEAD_DIM, F32), (N_KV_HEADS * HEAD_DIM, F32), None, (IDX_DIM, F32),
              (2 * D_MODEL, BF16)]
    shapes = [jax.ShapeDtypeStruct((rows, w[0]), w[1]) if w else None for w in widths]
    specs = [flat(w[0]) if w else None for w in widths]
    shapes[1], specs[1] = jax.ShapeDtypeStruct((nb, N_HEADS, s, HEAD_DIM), BF16), heads(N_HEADS)
    shapes[4], specs[4] = jax.ShapeDtypeStruct((nb, IDX_HEADS, s, IDX_DIM), BF16), heads(IDX_HEADS)
    shapes += [jax.ShapeDtypeStruct((nb, N_KV_HEADS, s, HEAD_DIM), BF16),
               jax.ShapeDtypeStruct((nb, N_KV_HEADS, VT_ROWS, s), BF16),
               jax.ShapeDtypeStruct((nb, s, IDX_DIM), BF16),
               jax.ShapeDtypeStruct((nb, IDX_HEADS, s), F32)]
    specs += [heads(N_KV_HEADS),
              pl.BlockSpec((1, N_KV_HEADS, VT_ROWS, tr), lambda b, i: (b, 0, 0, i)),
              pl.BlockSpec((1, tr, IDX_DIM), lambda b, i: (b, i, 0)),
              pl.BlockSpec((1, IDX_HEADS, tr), lambda b, i: (b, 0, i))]
    return pl.pallas_call(
        _proj_seq_kernel,
        out_shape=shapes,
        grid=(nb, nt),
        in_specs=[flat(D_MODEL), _resident(g)] + [_resident(w) for w in ws] + [tab, tab, tab],
        out_specs=specs,
        compiler_params=_cparams(("arbitrary", "arbitrary")),
        name="proj_seq",
    )(x, g, *ws, *tabs)


def _conv_kernel(cur_ref, halo_ref, lc_ref, cw_ref, cb_ref, lg_ref, lb_ref, wo_ref, out_ref, e_ref, c_ref, *, tr, rc):
    halo = jnp.where(pl.program_id(1) == 0, lc_ref[0], halo_ref[...])
    e_ref[0, 0:HALO, :] = halo
    e_ref[0, HALO:HALO + tr, :] = cur_ref[...]
    n = tr + HALO - 8
    for b in range(1, 8):
        e_ref[b, 0:n, :] = e_ref[0, b:b + n, :]
    first = HALO - (CONV_WIDTH - 1)

    def chunk(ci, carry):
        r0 = pl.multiple_of(ci * rc, rc)
        acc = jnp.zeros((rc, CONV_CH), F32) + cb_ref[...]
        for w in range(CONV_WIDTH):
            s = first + w
            acc = acc + e_ref[s % 8, pl.ds(r0 + 8 * (s // 8), rc), :] * cw_ref[pl.ds(w, 1), :]
        c_ref[pl.ds(r0, rc), :] = acc
        return carry

    lax.fori_loop(0, tr // rc, chunk, 0)
    c = c_ref[...]
    mu = jnp.mean(c, axis=-1, keepdims=True)
    xc = c - mu
    y = xc * lax.rsqrt(jnp.mean(xc * xc, axis=-1, keepdims=True) + EPS) * lg_ref[...] + lb_ref[...]
    act = (y * jax.nn.sigmoid(y)).astype(BF16)
    out_ref[...] = jnp.dot(act, wo_ref[...], preferred_element_type=F32).astype(BF16)


def _conv_branch(a_flat, row_off, nb, t, left_ctx, cw, cb, lg, lb, wo, tr):
    nt = t // tr
    rc = min(tr, 32)
    per_batch_ctx = left_ctx.shape[0] > 1
    cur = pl.BlockSpec((tr, CONV_CH), lambda b, i: (row_off // tr + b * nt + i, 0))
    halo = pl.BlockSpec((HALO, CONV_CH),
                        lambda b, i: (jnp.maximum((row_off + b * t + i * tr) // HALO - 1, 0), 0))
    lctx = pl.BlockSpec((1, HALO, CONV_CH), lambda b, i: (b if per_batch_ctx else 0, 0, 0))
    return pl.pallas_call(
        functools.partial(_conv_kernel, tr=tr, rc=rc),
        out_shape=jax.ShapeDtypeStruct((nb * t, D_MODEL), BF16),
        grid=(nb, nt),
        in_specs=[cur, halo, lctx, _resident(cw), _resident(cb), _resident(lg), _resident(lb), _resident(wo)],
        out_specs=pl.BlockSpec((tr, D_MODEL), lambda b, i: (b * nt + i, 0)),
        scratch_shapes=[pltpu.VMEM((8, tr + HALO, CONV_CH), F32), pltpu.VMEM((tr, CONV_CH), F32)],
        compiler_params=_cparams(("arbitrary", "arbitrary")),
        name="conv",
    )(a_flat, a_flat, left_ctx, cw, cb, lg, lb, wo)


def _float_to_skey(f):
    bits = lax.bitcast_convert_type(f, I32)
    return bits ^ ((bits >> 31) & 0x7FFFFFFF)


def _skey_to_float(key):
    return lax.bitcast_convert_type(key ^ ((key >> 31) & 0x7FFFFFFF), F32)


SKEY_LOWEST_FINITE = INT_MIN + 0x00800000


def _high_half(f):
    return lax.bitcast_convert_type(lax.bitcast_convert_type(f, I32) & -65536, F32)


def _key_to_float(u):
    key = u ^ INT_MIN
    bits = key ^ ((key >> 31) & 0x7FFFFFFF)
    return lax.bitcast_convert_type(bits, F32)


def _dsa_kernel(q_ref, qi_ref, kw_ref, k_ref, v_ref, ki_ref, o_ref, sc_ref, m_ref, acc_ref,
                *, tq, tk, k_sel, q_off, n_valid_end, idx_bits):
    i = pl.program_id(1)
    qf0 = q_off + i * tq
    klim = jnp.minimum(FRONT + ((qf0 + tq - 1) // CHUNK + 1) * CHUNK, n_valid_end)
    n_tiles = (klim + tk - 1) // tk
    ncol = tk // LANES
    kf = float(k_sel)
    group = N_HEADS // N_KV_HEADS

    kwv = kw_ref[...]
    qrow = lax.broadcasted_iota(I32, (tq, 1), 0)
    lim_row = jnp.minimum(FRONT + ((qf0 + qrow) // CHUNK + 1) * CHUNK, n_valid_end)

    def score_tile(t, carry):
        k0 = pl.multiple_of(t * tk, tk)
        kt = ki_ref[0, pl.ds(k0, tk), :]
        sc = jnp.zeros((tq, tk), F32)
        for h in range(IDX_HEADS):
            rel = lax.dot_general(qi_ref[0, h], kt, (((1,), (1,)), ((), ())), preferred_element_type=F32)
            sc = sc + kwv[:, IDX_DIM + h:IDX_DIM + h + 1] * jnp.maximum(rel, 0.0)
        j = k0 + lax.broadcasted_iota(I32, (tq, tk), 1)
        j = jnp.where(j >= FRONT - N_META, j, n_valid_end)
        sc_ref[:, pl.ds(k0, tk)] = jnp.where(j < lim_row, sc, NEG_INF)
        return carry

    lax.fori_loop(0, n_tiles, score_tile, 0)

    lane = lax.broadcasted_iota(I32, (tq, LANES), 1)

    def count(pred):
        def body(t, acc):
            k0 = pl.multiple_of(t * tk, tk)
            for c in range(ncol):
                blk = sc_ref[:, pl.ds(k0 + c * LANES, LANES)]
                acc = acc + pred(blk, k0 + c * LANES)
            return acc
        acc = lax.fori_loop(0, n_tiles, body, jnp.zeros((tq, LANES), F32))
        return jnp.broadcast_to(jnp.sum(acc, axis=1, keepdims=True), (tq, LANES))

    def bisect(it, u):
        trial = u | jnp.left_shift(jnp.int32(1), 31 - it)
        thr = _key_to_float(trial)
        cnt = count(lambda blk, j0: jnp.where(blk >= thr, 1.0, 0.0))
        return jnp.where(cnt >= kf, trial, u)

    u = lax.fori_loop(0, 32, bisect, jnp.zeros((tq, LANES), I32))
    key = jnp.maximum(u ^ INT_MIN, INT_MIN + 0x00800000)
    thr = _key_to_float(key ^ INT_MIN)
    n_ge = count(lambda blk, j0: jnp.where(blk >= thr, 1.0, 0.0))
    tied = jnp.max(n_ge) > kf

    def write_bias(sel):
        def body(t, carry):
            k0 = pl.multiple_of(t * tk, tk)
            for c in range(ncol):
                blk = sc_ref[:, pl.ds(k0 + c * LANES, LANES)]
                sc_ref[:, pl.ds(k0 + c * LANES, LANES)] = sel(blk, k0 + c * LANES)
            return carry
        lax.fori_loop(0, n_tiles, body, 0)

    @pl.when(jnp.logical_not(tied))
    def _():
        write_bias(lambda blk, j0: jnp.where(blk >= thr, 0.0, NEG_INF))

    @pl.when(tied)
    def _():
        need = kf - count(lambda blk, j0: jnp.where(blk > thr, 1.0, 0.0))

        def jbis(it, ans):
            trial = ans + jnp.left_shift(jnp.int32(1), idx_bits - 1 - it)
            cnt = count(lambda blk, j0: jnp.where(blk == thr, jnp.where(j0 + lane < trial, 1.0, 0.0), 0.0))
            return jnp.where(cnt < need, trial, ans)

        ans = lax.fori_loop(0, idx_bits, jbis, jnp.zeros((tq, LANES), I32))
        write_bias(lambda blk, j0: jnp.where(
            blk > thr, 0.0, jnp.where(blk == thr, jnp.where(j0 + lane <= ans, 0.0, NEG_INF), NEG_INF)))

    m_ref[...] = jnp.full(m_ref.shape, M_INIT, F32)
    acc_ref[...] = jnp.zeros(acc_ref.shape, F32)

    def attend_tile(t, carry):
        k0 = pl.multiple_of(t * tk, tk)
        bias = sc_ref[:, pl.ds(k0, tk)]
        for g in range(N_KV_HEADS):
            qg = q_ref[0, g * group:(g + 1) * group].reshape(group * tq, HEAD_DIM)
            s = lax.dot_general(qg, k_ref[0, g, pl.ds(k0, tk), :], (((1,), (1,)), ((), ())),
                                preferred_element_type=F32)
            s = (s.reshape(group, tq, tk) + bias[None]).reshape(group * tq, tk)
            m_prev = m_ref[g]
            m_new = jnp.maximum(m_prev, jnp.max(s, axis=1, keepdims=True))
            p = jnp.exp2(s - jnp.concatenate([m_new] * ncol, axis=1))
            pv = jnp.dot(p.astype(BF16), v_ref[0, g, pl.ds(k0, tk), :], preferred_element_type=F32)
            acc_ref[g] = jnp.exp2(m_prev - m_new) * acc_ref[g] + pv
            m_ref[g] = m_new
        return carry

    lax.fori_loop(0, n_tiles, attend_tile, 0)

    for g in range(N_KV_HEADS):
        acc = acc_ref[g]
        o = acc[:, :HEAD_DIM] / acc[:, HEAD_DIM:HEAD_DIM + 1]
        for r in range(0, group, 2):
            pair = jnp.concatenate([o[r * tq:(r + 1) * tq], o[(r + 1) * tq:(r + 2) * tq]], axis=1)
            h0 = g * group + r
            o_ref[:, h0 * HEAD_DIM:(h0 + 2) * HEAD_DIM] = pair.astype(BF16)


def _dsa(qh, qih, kw_flat, kw_row_off, kh, vh, kih, *, tq, tk, k_sel, q_off, n_valid_end):
    nb, _, s, _ = qh.shape
    lp = kh.shape[2]
    nq = s // tq
    kern = functools.partial(_dsa_kernel, tq=tq, tk=tk, k_sel=k_sel, q_off=q_off, n_valid_end=n_valid_end,
                             idx_bits=max(1, int(lp).bit_length()))
    per_batch = lambda shp: pl.BlockSpec((1,) + shp, lambda b, i: (b,) + (0,) * len(shp),
                                         pipeline_mode=pl.Buffered(2 if nq == 1 else 1))
    return pl.pallas_call(
        kern,
        out_shape=jax.ShapeDtypeStruct((nb * s, N_HEADS * HEAD_DIM), BF16),
        grid=(nb, nq),
        in_specs=[pl.BlockSpec((1, N_HEADS, tq, HEAD_DIM), lambda b, i: (b, 0, i, 0)),
                  pl.BlockSpec((1, IDX_HEADS, tq, IDX_DIM), lambda b, i: (b, 0, i, 0)),
                  pl.BlockSpec((tq, LANES), lambda b, i: (kw_row_off // tq + b * nq + i, 0)),
                  per_batch((N_KV_HEADS, lp, HEAD_DIM)), per_batch((N_KV_HEADS, lp, LANES)), per_batch((lp, IDX_DIM))],
        out_specs=pl.BlockSpec((tq, N_HEADS * HEAD_DIM), lambda b, i: (b * nq + i, 0)),
        scratch_shapes=[pltpu.VMEM((tq, (lp // LANES | 1) * LANES), F32),
                        pltpu.VMEM((N_KV_HEADS, (N_HEADS // N_KV_HEADS) * tq, LANES), F32),
                        pltpu.VMEM((N_KV_HEADS, (N_HEADS // N_KV_HEADS) * tq, LANES), F32)],
        compiler_params=_cparams(("arbitrary", "arbitrary")),
        name="dsa",
    )(qh, qih, kw_flat, kh, vh, kih)


def _dsa_t_kernel(q_ref, qi_ref, wt_ref, k_ref, vt_ref, ki_ref, o_ref, sc_ref, sch_ref, m_ref, mu_ref, acc_ref,
                  s_ref, s2_ref,
                  *, tk, k_sel, n_valid_end, idx_bits):
    tq = LANES
    i = pl.program_id(1)
    qf0 = i * tq
    klim = jnp.minimum(FRONT + qf0 + tq, n_valid_end)
    n_tiles = (klim + tk - 1) // tk
    kf = float(k_sel)
    group = N_HEADS // N_KV_HEADS
    fold = 32
    fold_h = 64
    assert sch_ref.shape[0] // fold_h <= 256

    wt = wt_ref[0]
    qlane = lax.broadcasted_iota(I32, (1, tq), 1)
    lim = jnp.minimum(FRONT + ((qf0 + qlane) // CHUNK + 1) * CHUNK, n_valid_end)
    qi_all = qi_ref[0].reshape(IDX_HEADS * tq, IDX_DIM)

    def score_tile(t, carry):
        th = tk // 2
        for half in range(2):
            k0 = pl.multiple_of(t * tk, tk) + half * th
            rel = lax.dot_general(ki_ref[0, pl.ds(k0, th), :], qi_all, (((1,), (1,)), ((), ())),
                                  preferred_element_type=F32)
            sc = jnp.zeros((th, tq), F32)
            for h in range(IDX_HEADS):
                sc = sc + wt[h:h + 1, :] * jnp.maximum(rel[:, h * tq:(h + 1) * tq], 0.0)
            j = k0 + lax.broadcasted_iota(I32, (th, tq), 0)
            j = jnp.where(j >= FRONT - N_META, j, n_valid_end)
            sc = jnp.where(j < lim, sc, NEG_INF)
            sc_ref[pl.ds(k0, th), :] = sc
            sch_ref[pl.ds(k0, th), :] = _high_half(sc).astype(HALF)
        return carry

    lax.fori_loop(0, n_tiles, score_tile, 0)

    def count(pred):
        def body(t, acc):
            k0 = pl.multiple_of(t * tk, tk)
            ones = pred(sc_ref[pl.ds(k0, tk), :], k0)
            return acc + jnp.sum(ones.reshape(tk // fold, fold, tq), axis=0)
        acc = lax.fori_loop(0, n_tiles, body, jnp.zeros((fold, tq), F32))
        return jnp.sum(acc, axis=0, keepdims=True)

    def count_high(thr):
        thr_h = _high_half(thr).astype(HALF)
        one, zero = jnp.ones((), HALF), jnp.zeros((), HALF)

        def body(t, acc):
            ones = jnp.where(sch_ref[pl.ds(pl.multiple_of(t * tk, tk), tk), :] >= thr_h, one, zero)
            for c in range(tk // fold_h):
                acc = acc + ones[c * fold_h:(c + 1) * fold_h]
            return acc
        acc = lax.fori_loop(0, n_tiles, body, jnp.zeros((fold_h, tq), HALF))
        return jnp.sum(acc.astype(F32), axis=0, keepdims=True)

    def bisect(counter):
        def step(it, carry):
            u, c_u = carry
            trial = u | jnp.left_shift(jnp.int32(1), 31 - it)
            cnt = counter(_key_to_float(trial))
            take = cnt >= kf
            return jnp.where(take, trial, u), jnp.where(take, cnt, c_u)
        return step

    n_adm = (lim - (FRONT - N_META)).astype(F32)
    low_step = bisect(lambda thr: count(lambda blk, k0: jnp.where(blk >= thr, 1.0, 0.0)))

    def unsettled(c_u):
        return jnp.max(jnp.where((c_u == kf) | (n_adm <= kf), 0, 1))

    def low_bits(state):
        it, _, u, c_u = state
        for j in range(LOW_GROUP):
            u, c_u = low_step(it + j, (u, c_u))
        return it + LOW_GROUP, unsettled(c_u), u, c_u

    u, c_u = lax.fori_loop(0, 16, bisect(count_high), (jnp.zeros((1, tq), I32), jnp.full((1, tq), jnp.inf, F32)))
    _, _, u, c_u = lax.while_loop(lambda state: (state[0] < 32) & (state[1] > 0), low_bits,
                                  (jnp.int32(16), unsettled(c_u), u, c_u))
    key = jnp.maximum(u ^ INT_MIN, SKEY_LOWEST_FINITE)
    thr = _skey_to_float(key)
    n_ge = jnp.minimum(c_u, n_adm)
    tied = jnp.max(n_ge) > kf

    def write_bias(sel):
        def body(t, carry):
            k0 = pl.multiple_of(t * tk, tk)
            sc_ref[pl.ds(k0, tk), :] = sel(sc_ref[pl.ds(k0, tk), :], k0)
            return carry
        lax.fori_loop(0, n_tiles, body, 0)

    @pl.when(jnp.logical_not(tied))
    def _():
        write_bias(lambda blk, k0: jnp.where(blk >= thr, 0.0, NEG_INF))

    @pl.when(tied)
    def _():
        need = kf - count(lambda blk, k0: jnp.where(blk > thr, 1.0, 0.0))
        krow = lax.broadcasted_iota(I32, (tk, tq), 0)

        def jbis(it, ans):
            trial = ans + jnp.left_shift(jnp.int32(1), idx_bits - 1 - it)
            cnt = count(lambda blk, k0: jnp.where(blk == thr, jnp.where(k0 + krow < trial, 1.0, 0.0), 0.0))
            return jnp.where(cnt < need, trial, ans)

        ans = lax.fori_loop(0, idx_bits, jbis, jnp.zeros((1, tq), I32))
        write_bias(lambda blk, k0: jnp.where(
            blk > thr, 0.0, jnp.where(blk == thr, jnp.where(k0 + krow <= ans, 0.0, NEG_INF), NEG_INF)))

    m_ref[...] = jnp.full(m_ref.shape, M_INIT, F32)
    acc_ref[...] = jnp.zeros(acc_ref.shape, F32)

    mu_ref[...] = jnp.full(mu_ref.shape, M_INIT, F32)

    def logits(t, buf, g, m_before):
        k0 = pl.multiple_of(t * tk, tk)
        bias = sc_ref[pl.ds(k0, tk), :]
        qg = q_ref[0, g * group:(g + 1) * group].reshape(group * tq, HEAD_DIM)
        s = lax.dot_general(k_ref[0, g, pl.ds(k0, tk), :], qg, (((1,), (1,)), ((), ())),
                            preferred_element_type=F32)
        s = jnp.concatenate([s[:, r * tq:(r + 1) * tq] + bias for r in range(group)], axis=1)
        buf[g] = s
        return jnp.maximum(m_before, jnp.max(s, axis=0, keepdims=True))

    def values(t, buf, g, m_t):
        p = jnp.exp2(buf[g] - m_t)
        pv = jnp.dot(vt_ref[0, g, :, pl.ds(pl.multiple_of(t * tk, tk), tk)], p.astype(BF16),
                     preferred_element_type=F32)
        acc_ref[g] = jnp.exp2(mu_ref[g] - m_t) * acc_ref[g] + pv
        mu_ref[g] = m_t

    def step(t_next, buf_next, t, buf):
        for g in range(N_KV_HEADS):
            m_t = m_ref[g]
            m_ref[g] = logits(t_next, buf_next, g, m_t)
            values(t, buf, g, m_t)

    for g in range(N_KV_HEADS):
        m_ref[g] = logits(0, s_ref, g, m_ref[g])

    def tile_pair(j, carry):
        step(2 * j + 1, s2_ref, 2 * j, s_ref)
        step(jnp.minimum(2 * j + 2, n_tiles - 1), s_ref, 2 * j + 1, s2_ref)
        return carry

    lax.fori_loop(0, n_tiles >> 1, tile_pair, 0)

    @pl.when((n_tiles & 1) == 1)
    def _():
        for g in range(N_KV_HEADS):
            values(n_tiles - 1, s_ref, g, m_ref[g])

    for g in range(N_KV_HEADS):
        acc = acc_ref[g]
        o = acc[:HEAD_DIM, :] / acc[HEAD_DIM:HEAD_DIM + 1, :]
        for r in range(0, group, 2):
            pair = jnp.concatenate([o[:, r * tq:(r + 1) * tq], o[:, (r + 1) * tq:(r + 2) * tq]], axis=0)
            h0 = g * group + r
            o_ref[:, h0 * HEAD_DIM:(h0 + 2) * HEAD_DIM] = pair.T.astype(BF16)


def _dsa_t(qh, qih, wt, kh, vt, kih, *, tk, k_sel, n_valid_end):
    nb, _, s, _ = qh.shape
    lp = kh.shape[2]
    tq = LANES
    nq = s // tq
    group = N_HEADS // N_KV_HEADS
    kern = functools.partial(_dsa_t_kernel, tk=tk, k_sel=k_sel, n_valid_end=n_valid_end,
                             idx_bits=max(1, int(lp).bit_length()))
    per_batch = lambda shp: pl.BlockSpec((1,) + shp, lambda b, i: (b,) + (0,) * len(shp), pipeline_mode=pl.Buffered(1))
    return pl.pallas_call(
        kern,
        out_shape=jax.ShapeDtypeStruct((nb * s, N_HEADS * HEAD_DIM), BF16),
        grid=(nb, nq),
        in_specs=[pl.BlockSpec((1, N_HEADS, tq, HEAD_DIM), lambda b, i: (b, 0, i, 0)),
                  pl.BlockSpec((1, IDX_HEADS, tq, IDX_DIM), lambda b, i: (b, 0, i, 0)),
                  pl.BlockSpec((1, IDX_HEADS, tq), lambda b, i: (b, 0, i)),
                  per_batch((N_KV_HEADS, lp, HEAD_DIM)), per_batch((N_KV_HEADS, VT_ROWS, lp)), per_batch((lp, IDX_DIM))],
        out_specs=pl.BlockSpec((tq, N_HEADS * HEAD_DIM), lambda b, i: (b * nq + i, 0)),
        scratch_shapes=[pltpu.VMEM((lp, tq), F32), pltpu.VMEM((lp, tq), HALF),
                        pltpu.VMEM((N_KV_HEADS, 1, group * tq), F32),
                        pltpu.VMEM((N_KV_HEADS, 1, group * tq), F32),
                        pltpu.VMEM((N_KV_HEADS, VT_ROWS, group * tq), F32),
                        pltpu.VMEM((N_KV_HEADS, tk, group * tq), F32),
                        pltpu.VMEM((N_KV_HEADS, tk, group * tq), F32)],
        compiler_params=_cparams(("arbitrary", "arbitrary")),
        name="dsa_t",
    )(qh, qih, wt, kh, vt, kih)


def _pad_keys(meta_rows, mid_rows, lp):
    nb = mid_rows[0].shape[0]
    tail = mid_rows[0].shape[2:]
    meta_b = jnp.broadcast_to(meta_rows[None], (nb,) + meta_rows.shape)
    used = FRONT + sum(m.shape[1] for m in mid_rows)
    parts = [jnp.zeros((nb, FRONT - N_META) + tail, meta_rows.dtype), meta_b] + list(mid_rows)
    if lp > used:
        parts.append(jnp.zeros((nb, lp - used) + tail, meta_rows.dtype))
    return jnp.concatenate(parts, axis=1)


def _attention_inputs(k_meta, v_meta, ki_meta, k_mid, v_mid, ki_mid, lp, keys_on_lanes):
    nb = k_mid[0].shape[0]
    kp = _pad_keys(k_meta, k_mid, lp).astype(BF16).reshape(nb, lp, N_KV_HEADS, HEAD_DIM).transpose(0, 2, 1, 3)
    vp = _pad_keys(v_meta, v_mid, lp).astype(BF16).reshape(nb, lp, N_KV_HEADS, HEAD_DIM)
    if keys_on_lanes:
        vp = vp.transpose(0, 2, 3, 1)
        ones = jnp.ones((nb, N_KV_HEADS, 1, lp), BF16)
        vp = jnp.concatenate([vp, ones, jnp.zeros((nb, N_KV_HEADS, VT_ROWS - HEAD_DIM - 1, lp), BF16)], axis=2)
    else:
        vp = vp.transpose(0, 2, 1, 3)
        ones = jnp.ones(vp.shape[:3] + (1,), BF16)
        vp = jnp.concatenate([vp, ones, jnp.zeros(vp.shape[:3] + (LANES - HEAD_DIM - 1,), BF16)], axis=3)
    kip = _pad_keys(ki_meta, ki_mid, lp).astype(BF16)
    return kp, vp, kip


def _merge_kernel(xp_ref, xs_ref, cp_ref, cs_ref, ap_ref, as_ref, gp_ref, gs_ref, wao_ref, wout_ref, gffn_ref, wr_ref,
                  br_ref, x1_ref, hn_ref, ids_ref, gts_ref, *, n_first):
    first = pl.program_id(0) < n_first
    x = jnp.where(first, xp_ref[...], xs_ref[...])
    conv = jnp.where(first, cp_ref[...], cs_ref[...]).astype(F32)
    attn = jnp.where(first, ap_ref[...], as_ref[...])
    ao = jnp.dot(attn, wao_ref[...], preferred_element_type=F32)
    gates = jnp.where(first, gp_ref[...], gs_ref[...]).astype(F32)
    mix = gates[:, :D_MODEL] * conv + gates[:, D_MODEL:] * ao
    x1 = x + jnp.dot(mix.astype(BF16), wout_ref[...], preferred_element_type=F32)
    x1_ref[...] = x1
    hn = x1 * lax.rsqrt(jnp.mean(x1 * x1, axis=-1, keepdims=True) + EPS) * gffn_ref[...]
    tr = hn.shape[0]
    for c in range(SUB):
        hn_ref[pl.ds(c, tr, stride=SUB), :] = hn[:, c * LANES:(c + 1) * LANES]
    logits = jnp.dot(hn, wr_ref[...], preferred_element_type=F32, precision=lax.Precision.HIGHEST) + br_ref[...]
    lane = lax.broadcasted_iota(I32, logits.shape, 1)
    lg = jnp.where(lane < N_EXPERTS, logits, NEG_INF)
    vals, ids = [], []
    for _ in range(TOP_K_EXPERTS):
        m = jnp.max(lg, axis=1, keepdims=True)
        idx = jnp.min(jnp.where(lg == m, lane, LANES), axis=1, keepdims=True)
        vals.append(m)
        ids.append(idx)
        lg = jnp.where(lane == idx, NEG_INF, lg)
    es = [jnp.exp(v - vals[0]) for v in vals]
    den = es[0]
    for e in es[1:]:
        den = den + e
    ids_out = jnp.zeros(logits.shape, I32)
    gts_out = jnp.zeros(logits.shape, F32)
    for j in range(TOP_K_EXPERTS):
        ids_out = jnp.where(lane == j, ids[j], ids_out)
        gts_out = jnp.where(lane == j, es[j] / den, gts_out)
    ids_ref[...] = ids_out[:, :SUB]
    gts_ref[...] = gts_out


def _merge(xp, xs, conv_p, conv_s, attn_p, attn_s, gates_p, gates_s, wao, wout, gffn, wr, br, tr):
    n_first, n_second = xp.shape[0] // tr, xs.shape[0] // tr
    rows = xp.shape[0] + xs.shape[0]
    first = lambda w: pl.BlockSpec((tr, w), lambda i: (jnp.minimum(i, n_first - 1), 0))
    second = lambda w: pl.BlockSpec((tr, w), lambda i: (jnp.maximum(i - n_first, 0), 0))
    row = lambda w: pl.BlockSpec((tr, w), lambda i: (i, 0))
    return pl.pallas_call(
        functools.partial(_merge_kernel, n_first=n_first),
        out_shape=[jax.ShapeDtypeStruct((rows, D_MODEL), F32), jax.ShapeDtypeStruct((rows * SUB, LANES), F32),
                   jax.ShapeDtypeStruct((rows, SUB), I32), jax.ShapeDtypeStruct((rows, LANES), F32)],
        grid=(n_first + n_second,),
        in_specs=[first(D_MODEL), second(D_MODEL), first(D_MODEL), second(D_MODEL), first(D_MODEL), second(D_MODEL),
                  first(2 * D_MODEL), second(2 * D_MODEL), _resident(wao), _resident(wout), _resident(gffn),
                  _resident(wr), _resident(br)],
        out_specs=[row(D_MODEL), pl.BlockSpec((tr * SUB, LANES), lambda i: (i, 0)), row(SUB), row(LANES)],
        compiler_params=_cparams(("arbitrary",)),
        name="merge",
    )(xp, xs, conv_p, conv_s, attn_p, attn_s, gates_p, gates_s, wao, wout, gffn, wr, br)


KEY_ALIGN = 1024


def _key_window(tm):
    return -(-(tm + KEY_ALIGN - 1) // KEY_ALIGN) * KEY_ALIGN


def _for_rows(n, fn):
    def group(gi, carry):
        for u in range(SUB):
            fn(gi * SUB + u)
        return carry

    def single(r, carry):
        fn(r)
        return carry

    shift = SUB.bit_length() - 1
    lax.fori_loop(0, n >> shift, group, 0)
    lax.fori_loop((n >> shift) << shift, n, single, 0)


def _moe_kernel(te_ref, base_ref, nval_ref, nreal_ref, src_hbm, dst_hbm, h_hbm, w1_ref, b1_ref, w3_ref, b3_ref, w2_ref,
                b2_ref, y_hbm, src_sm, dst_sm, xbuf, xb, ybuf, wb, ksem, gsem, ssem, *, tm):
    i = pl.program_id(0)
    nreal = nreal_ref[0]
    window = src_sm.shape[0] // 3

    def rows_copy(tile):
        s = lax.rem(tile, 3)
        start = pl.multiple_of(base_ref[tile] & -KEY_ALIGN, KEY_ALIGN)
        slot = pl.ds(pl.multiple_of(s * window, KEY_ALIGN), window)
        return (pltpu.make_async_copy(src_hbm.at[pl.ds(start, window)], src_sm.at[slot], ksem.at[0, s]),
                pltpu.make_async_copy(dst_hbm.at[pl.ds(start, window)], dst_sm.at[slot], ksem.at[1, s]))

    def keys_start(tile):
        for copy in rows_copy(tile):
            copy.start()

    def keys_wait(tile):
        for copy in rows_copy(tile):
            copy.wait()

    def row_reader(table, tile):
        first = lax.rem(tile, 3) * window + (base_ref[tile] & (KEY_ALIGN - 1))
        return lambda r: pl.multiple_of(table[first + r], SUB)

    def gather_row(s, r, src_row):
        return pltpu.make_async_copy(h_hbm.at[pl.ds(src_row, SUB)], xbuf.at[s, pl.ds(r * SUB, SUB)], gsem.at[s])

    def scatter_row(r, dst_row):
        return pltpu.make_async_copy(ybuf.at[pl.ds(r * SUB, SUB)], y_hbm.at[pl.ds(dst_row, SUB)], ssem.at[0])

    def start_gather(tile):
        src_of, s = row_reader(src_sm, tile), tile & 1
        _for_rows(tm, lambda r: gather_row(s, r, src_of(r)).start())

    @pl.when(jnp.logical_and(i == 0, nreal > 0))
    def _():
        keys_start(0)

        @pl.when(nreal > 1)
        def _():
            keys_start(1)

        keys_wait(0)
        start_gather(0)

    @pl.when(i + 1 < nreal)
    def _():
        keys_wait(i + 1)
        start_gather(i + 1)

    @pl.when(i + 2 < nreal)
    def _():
        keys_start(i + 2)

    @pl.when(i < nreal)
    def _():
        s = i & 1
        _for_rows(tm, lambda r: gather_row(s, r, 0).wait())
        for c in range(SUB):
            xb[:, c * LANES:(c + 1) * LANES] = xbuf[s, pl.ds(c, tm, stride=SUB), :].astype(BF16)
        @pl.when(jnp.logical_or(i == 0, te_ref[i] != te_ref[jnp.maximum(i - 1, 0)]))
        def _():
            for j, w_ref in enumerate((w1_ref, w3_ref, w2_ref)):
                wb[j] = w_ref[0].astype(BF16)

        x = xb[...]
        g = jnp.minimum(jnp.dot(x, wb[0], preferred_element_type=F32) + b1_ref[0], SWIGLU_LIMIT)
        u = jnp.clip(jnp.dot(x, wb[1], preferred_element_type=F32) + b3_ref[0], -SWIGLU_LIMIT, SWIGLU_LIMIT)
        act = g * jax.nn.sigmoid(SWIGLU_ALPHA * g) * (u + 1.0)
        y = jnp.dot(act.astype(BF16), wb[2], preferred_element_type=F32) + b2_ref[0]

        @pl.when(i > 0)
        def _():
            _for_rows(nval_ref[i - 1], lambda r: scatter_row(r, 0).wait())

        for c in range(SUB):
            ybuf[pl.ds(c, tm, stride=SUB), :] = y[:, c * LANES:(c + 1) * LANES]

        dst_of = row_reader(dst_sm, i)
        _for_rows(nval_ref[i], lambda r: scatter_row(r, dst_of(r)).start())

        @pl.when(i == nreal - 1)
        def _():
            _for_rows(nval_ref[i], lambda r: scatter_row(r, 0).wait())


def _moe(hn, src_row, dst_row, tile_expert, base, nval, n_real, w1, b1, w3, b3, w2, b2, n_tok, tm):
    nt = tile_expert.shape[0]
    by_expert = lambda i, te, *_: (te[i], 0, 0)
    wspec = pl.BlockSpec((1, D_MODEL, D_FF), by_expert)
    w2spec = pl.BlockSpec((1, D_FF, D_MODEL), by_expert)
    bspec = pl.BlockSpec((1, 1, D_FF), by_expert)
    any_spec = pl.BlockSpec(memory_space=pl.ANY)
    grid_spec = pltpu.PrefetchScalarGridSpec(
        num_scalar_prefetch=4,
        grid=(nt,),
        in_specs=[any_spec, any_spec, any_spec, wspec, bspec, wspec, bspec, w2spec, bspec],
        out_specs=any_spec,
        scratch_shapes=[pltpu.SMEM((3 * _key_window(tm),), I32), pltpu.SMEM((3 * _key_window(tm),), I32),
                        pltpu.VMEM((2, tm * SUB, LANES), F32), pltpu.VMEM((tm, D_MODEL), BF16),
                        pltpu.VMEM((tm * SUB, LANES), F32), pltpu.VMEM((3, D_MODEL, D_FF), BF16),
                        pltpu.SemaphoreType.DMA((2, 3)), pltpu.SemaphoreType.DMA((2,)), pltpu.SemaphoreType.DMA((1,))],
    )
    return pl.pallas_call(
        functools.partial(_moe_kernel, tm=tm),
        out_shape=jax.ShapeDtypeStruct((TOP_K_EXPERTS * n_tok * SUB, LANES), F32),
        grid_spec=grid_spec,
        compiler_params=_cparams(("arbitrary",)),
        name="moe",
    )(tile_expert, base, nval, n_real, src_row, dst_row, hn, w1, b1, w3, b3, w2, b2)


def _route(ids, n, tm):
    n_pairs = n * TOP_K_EXPERTS
    pair_bits = max(1, (n_pairs - 1).bit_length())
    assert pair_bits + (N_EXPERTS - 1).bit_length() <= 31 and tm % LANES == 0
    assert TOP_K_EXPERTS & (TOP_K_EXPERTS - 1) == 0
    pair = jnp.arange(n_pairs, dtype=I32)
    keys = jnp.sort(jnp.left_shift(ids.reshape(-1), pair_bits) | pair)
    starts = jnp.searchsorted(keys, jnp.left_shift(jnp.arange(N_EXPERTS + 1, dtype=I32), pair_bits),
                              method="compare_all").astype(I32)
    counts = starts[1:] - starts[:-1]
    tiles = (counts + tm - 1) // tm
    tile_end = jnp.cumsum(tiles)
    nt = -(-(n_pairs + N_EXPERTS * (tm - 1)) // tm)
    ti = jnp.arange(nt, dtype=I32)
    te = jnp.minimum(jnp.searchsorted(tile_end, ti, side="right", method="compare_all"), N_EXPERTS - 1).astype(I32)
    within = ti - (tile_end - tiles)[te]
    base = jnp.clip(starts[:-1][te] + within * tm, 0, n_pairs - 1).astype(I32)
    nval = jnp.clip(counts[te] - within * tm, 0, tm).astype(I32)
    sorted_pair = keys & ((1 << pair_bits) - 1)
    token, slot = sorted_pair // TOP_K_EXPERTS, sorted_pair % TOP_K_EXPERTS
    padded = -(-n_pairs // KEY_ALIGN) * KEY_ALIGN + _key_window(tm)
    src_row = jnp.pad(token * SUB, (0, padded - n_pairs))
    dst_row = jnp.pad((slot * n + token) * SUB, (0, padded - n_pairs))
    return src_row, dst_row, te, base, nval, tile_end[-1:].astype(I32)


def _final_kernel(x1_ref, y0_ref, y1_ref, y2_ref, y3_ref, gts_ref, g_ref, out_ref):
    gts = gts_ref[...]
    tr = gts.shape[0]
    cols = []
    for c in range(SUB):
        acc = x1_ref[:, c * LANES:(c + 1) * LANES]
        for j, y_ref in enumerate((y0_ref, y1_ref, y2_ref, y3_ref)):
            acc = acc + gts[:, j:j + 1] * y_ref[pl.ds(c, tr, stride=SUB), :]
        cols.append(acc)
    ss = cols[0] * cols[0]
    for acc in cols[1:]:
        ss = ss + acc * acc
    scale = lax.rsqrt(jnp.sum(ss, axis=-1, keepdims=True) / D_MODEL + EPS)
    for c, acc in enumerate(cols):
        out_ref[:, c * LANES:(c + 1) * LANES] = acc * scale * g_ref[:, c * LANES:(c + 1) * LANES]


def _final(x1, yslots, gts, g, tr):
    n = x1.shape[0]
    nblk = n // tr
    row = lambda w: pl.BlockSpec((tr, w), lambda i: (i, 0))
    slot = lambda j: pl.BlockSpec((tr * SUB, LANES), lambda i: (j * nblk + i, 0))
    return pl.pallas_call(
        _final_kernel,
        out_shape=jax.ShapeDtypeStruct((n, D_MODEL), F32),
        grid=(nblk,),
        in_specs=[row(D_MODEL)] + [slot(j) for j in range(TOP_K_EXPERTS)] + [row(LANES), _resident(g)],
        out_specs=row(D_MODEL),
        compiler_params=_cparams(("arbitrary",)),
        name="final",
    )(x1, yslots, yslots, yslots, yslots, gts, g)


def _tile(n, pref):
    t = min(n, pref)
    while n % t:
        t //= 2
    return t


def kernel(x_prompt, x_sample, cache_k, cache_v, cache_idx_k, state_conv, meta_tokens, g_norm_mix, w_in, conv_w,
           conv_b, conv_ln_g, conv_ln_b, w_conv_out, w_attn_out, w_out, g_norm_ffn, w_router, b_router,
           w1, b1, w3, b3, w2, b2, g_norm_final):
    nb, s_p, _ = x_prompt.shape
    db, s_n, _ = x_sample.shape
    past = cache_k.shape[1]
    rows_p, rows_s = nb * s_p, db * s_n
    n = rows_p + rows_s
    tr = _tile(rows_s, ROW_TILE)
    assert rows_p % tr == 0 and s_p % tr == 0 and tr % s_n == 0 and s_p % Q_TILE == 0

    cuts = [0, 2 * CONV_CH, N_HEADS * HEAD_DIM, N_KV_HEADS * HEAD_DIM, N_KV_HEADS * HEAD_DIM, IDX_HEADS * IDX_DIM,
            IDX_DIM + IDX_HEADS, 2 * D_MODEL]
    offs = [sum(cuts[:i + 1]) for i in range(len(cuts))]
    seg = [w_in[:, offs[i]:offs[i + 1]].astype(BF16) for i in range(len(cuts) - 1)]
    seg[5] = jnp.pad(seg[5], ((0, 0), (0, LANES - seg[5].shape[1])))
    g_mix = g_norm_mix.reshape(1, D_MODEL)

    tabs_p = _rope_tables(N_META + jnp.arange(s_p, dtype=I32))
    tabs_s = _rope_tables(jnp.tile(N_META + past + jnp.arange(s_n, dtype=I32), tr // s_n))
    tabs_m = _rope_tables(jnp.arange(N_META, dtype=I32))

    xp, xs = x_prompt.reshape(rows_p, D_MODEL), x_sample.reshape(rows_s, D_MODEL)
    a_p, qh_p, k_p, v_p, qih_p, kw_p, gates_p, kh_mid, vt_mid, ki_mid, wt_p = _project_seq(
        xp, nb, s_p, tabs_p, g_mix, seg, tr)
    a_s, q_s, k_s, v_s, qi_s, kw_s, gates_s = [t[:rows_s] for t in _project(xs, xs, tabs_s, lambda i: 0, g_mix, seg, tr)]
    meta = meta_tokens.astype(F32)
    a_m, _, k_m, v_m, _, kw_m, _ = _project(meta, meta, tabs_m, lambda i: 0, g_mix, seg, N_META)
    a_m, k_m, v_m, ki_m = a_m[:N_META], k_m[:N_META], v_m[:N_META], kw_m[:N_META, :IDX_DIM]

    cw = jnp.pad(conv_w, ((0, HALO - CONV_WIDTH), (0, 0)))
    cb, lg, lb = conv_b.reshape(1, -1), conv_ln_g.reshape(1, -1), conv_ln_b.reshape(1, -1)
    wco = w_conv_out.astype(BF16)
    ctx_p = jnp.concatenate([jnp.zeros((HALO - N_META, CONV_CH), F32), a_m], axis=0)[None]
    ctx_s = jnp.pad(state_conv, ((0, 0), (HALO - (CONV_WIDTH - 1), 0), (0, 0)))
    conv_p = _conv_branch(a_p, 0, nb, s_p, ctx_p, cw, cb, lg, lb, wco, _tile(s_p, 256))
    conv_s = _conv_branch(a_s, 0, db, s_n, ctx_s, cw, cb, lg, lb, wco, s_n)

    k_p3, v_p3 = k_p.reshape(nb, s_p, -1), v_p.reshape(nb, s_p, -1)
    ki_p3 = kw_p.reshape(nb, s_p, IDX_DIM)
    k_s3, v_s3 = k_s.reshape(db, s_n, -1), v_s.reshape(db, s_n, -1)
    ki_s3 = kw_s[:, :IDX_DIM].reshape(db, s_n, IDX_DIM)

    tk_p = _tile(s_p, K_TILE_T)
    lp_p = -(-(FRONT + s_p) // tk_p) * tk_p
    front = lambda rows: jnp.pad(rows, ((FRONT - N_META, 0), (0, 0))).astype(BF16)
    per_head = lambda t: t.reshape(FRONT, N_KV_HEADS, HEAD_DIM).transpose(1, 0, 2)
    rep = lambda t: jnp.broadcast_to(t[None], (nb,) + t.shape)
    tail = lp_p - FRONT - s_p
    kp = jnp.concatenate([rep(per_head(front(k_m))), kh_mid, jnp.zeros((nb, N_KV_HEADS, tail, HEAD_DIM), BF16)], axis=2)
    vt_front = jnp.concatenate([per_head(front(v_m)).transpose(0, 2, 1), jnp.ones((N_KV_HEADS, 1, FRONT), BF16),
                                jnp.zeros((N_KV_HEADS, VT_ROWS - HEAD_DIM - 1, FRONT), BF16)], axis=1)
    vtp = jnp.concatenate([rep(vt_front), vt_mid, jnp.zeros((nb, N_KV_HEADS, VT_ROWS, tail), BF16)], axis=3)
    kip = jnp.concatenate([rep(front(ki_m)), ki_mid, jnp.zeros((nb, tail, IDX_DIM), BF16)], axis=1)
    attn_p = _dsa_t(qh_p, qih_p, wt_p, kp, vtp, kip, tk=tk_p, k_sel=min(TOPK_MAX, s_p // 4),
                    n_valid_end=FRONT + s_p)

    def heads(t, b_, s_, nh):
        return t.reshape(b_, s_, nh, -1).transpose(0, 2, 1, 3)

    tk_s = K_TILE
    lp_s = -(-(FRONT + past + s_n) // tk_s) * tk_s
    ck = cache_k.reshape(db, past, -1)
    cv = cache_v.reshape(db, past, -1)
    ks, vs, kis = _attention_inputs(k_m, v_m, ki_m, [ck, k_s3], [cv, v_s3], [cache_idx_k, ki_s3], lp_s, False)
    attn_s = _dsa(heads(q_s, db, s_n, N_HEADS), heads(qi_s, db, s_n, IDX_HEADS), kw_s, 0, ks, vs, kis,
                  tq=s_n, tk=tk_s, k_sel=min(TOPK_MAX, (past + s_n) // 4), q_off=past,
                  n_valid_end=FRONT + past + s_n)

    wr = jnp.pad(w_router, ((0, 0), (0, LANES - N_EXPERTS)))
    br = jnp.pad(b_router, (0, LANES - N_EXPERTS)).reshape(1, LANES)
    x1, hn, ids, gts = _merge(xp, xs, conv_p, conv_s, attn_p, attn_s, gates_p, gates_s, w_attn_out.astype(BF16),
                              w_out.astype(BF16), g_norm_ffn.reshape(1, -1), wr, br, tr)

    tm = MOE_TILE
    src_row, dst_row, tile_expert, base, nval, n_real = _route(ids[:, :TOP_K_EXPERTS], n, tm)
    yslots = _moe(hn, src_row, dst_row, tile_expert, base, nval, n_real, w1, b1.reshape(N_EXPERTS, 1, D_FF),
                  w3, b3.reshape(N_EXPERTS, 1, D_FF), w2, b2.reshape(N_EXPERTS, 1, D_MODEL), n, tm)
    out = _final(x1, yslots, gts, g_norm_final.reshape(1, -1), tr)

    y_prompt = out[:rows_p].reshape(nb, s_p, D_MODEL)
    y_sample = out[rows_p:].reshape(db, s_n, D_MODEL)
    lead = lambda m: jnp.broadcast_to(m[None], (nb,) + m.shape)
    k_prompt = jnp.concatenate([lead(k_m), k_p3], axis=1).reshape(nb, N_META + s_p, N_KV_HEADS, HEAD_DIM)
    v_prompt = jnp.concatenate([lead(v_m), v_p3], axis=1).reshape(nb, N_META + s_p, N_KV_HEADS, HEAD_DIM)
    idxk_prompt = jnp.concatenate([lead(ki_m), ki_p3], axis=1)
    a_p3 = a_p.reshape(nb, s_p, CONV_CH)
    a_s3 = a_s.reshape(db, s_n, CONV_CH)
    keep = CONV_WIDTH - 1
    conv_state_p = jnp.concatenate([lead(a_m), a_p3], axis=1)[:, N_META + s_p - keep:]
    conv_state_s = jnp.concatenate([state_conv, a_s3], axis=1)[:, s_n:]
    return (y_prompt, y_sample, k_prompt, v_prompt, idxk_prompt, conv_state_p,
            k_s3.reshape(db, s_n, N_KV_HEADS, HEAD_DIM), v_s3.reshape(db, s_n, N_KV_HEADS, HEAD_DIM), ki_s3,
            conv_state_s)
```

```python
import functools

import jax
import jax.numpy as jnp
from jax import lax
from jax.experimental import pallas as pl
from jax.experimental.pallas import tpu as pltpu

F32, BF16, I32 = jnp.float32, jnp.bfloat16, jnp.int32
HALF = jnp.bfloat16

D_MODEL = 1024
CHUNK = 64
N_META = 16
CONV_CH = D_MODEL
CONV_WIDTH = 31
N_HEADS = 16
N_KV_HEADS = 4
HEAD_DIM = 64
ROT_DIM = HEAD_DIM // 4
ROPE_THETA = 500000.0
IDX_HEADS = 8
IDX_DIM = 64
TOPK_MAX = 256
N_EXPERTS = 32
TOP_K_EXPERTS = 4
D_FF = D_MODEL
SWIGLU_LIMIT = 7.0
SWIGLU_ALPHA = 1.702
EPS = 1e-5
Q_SCALE = HEAD_DIM ** -0.5 * 1.4426950408889634

LANES = 128
SUB = 8
VT_ROWS = HEAD_DIM + 2 * SUB
FRONT = 128
HALO = 32
VMEM_LIMIT = 56 * 1024 * 1024
NEG_INF = float("-inf")
INT_MIN = -2 ** 31
M_INIT = -1e30

ROW_TILE = 512
Q_TILE = 128
K_TILE = 512
K_TILE_T = 1024
LOW_GROUP = 4
MOE_TILE = 512


def _cparams(sem):
    return pltpu.CompilerParams(dimension_semantics=sem, vmem_limit_bytes=VMEM_LIMIT)


def _resident(arr):
    nd = arr.ndim
    return pl.BlockSpec(arr.shape, lambda *_: (0,) * nd, pipeline_mode=pl.Buffered(1))


def _rope_cols(x, c, s1, s2):
    outs = []
    for j in range(x.shape[1] // LANES):
        xj = x[:, j * LANES:(j + 1) * LANES]
        outs.append(xj * c + pltpu.roll(xj, LANES - ROT_DIM // 2, 1) * s1 + pltpu.roll(xj, ROT_DIM // 2, 1) * s2)
    return outs


def _proj_kernel(xp_ref, xs_ref, g_ref, wa_ref, wq_ref, wk_ref, wv_ref, wqi_ref, wkw_ref, wgl_ref,
                 c_ref, s1_ref, s2_ref,
                 a_ref, q_ref, k_ref, v_ref, qi_ref, kw_ref, gate_ref, *, n_first):
    x = jnp.where(pl.program_id(0) < n_first, xp_ref[...], xs_ref[...])
    h = x * lax.rsqrt(jnp.mean(x * x, axis=-1, keepdims=True) + EPS) * g_ref[...]
    hb = h.astype(BF16)
    c, s1, s2 = c_ref[...], s1_ref[...], s2_ref[...]

    def dot(w_ref):
        return jnp.dot(hb, w_ref[...], preferred_element_type=F32)

    glu = dot(wa_ref)
    a_ref[...] = glu[:, :CONV_CH] * jax.nn.sigmoid(glu[:, CONV_CH:])
    for j, col in enumerate(_rope_cols(dot(wq_ref), c, s1, s2)):
        q_ref[:, j * LANES:(j + 1) * LANES] = (col * Q_SCALE).astype(BF16)
    for j, col in enumerate(_rope_cols(dot(wk_ref), c, s1, s2)):
        k_ref[:, j * LANES:(j + 1) * LANES] = col
    v_ref[...] = dot(wv_ref)
    for j, col in enumerate(_rope_cols(dot(wqi_ref), c, s1, s2)):
        qi_ref[:, j * LANES:(j + 1) * LANES] = (col * IDX_DIM ** -0.5).astype(BF16)
    pkw = dot(wkw_ref)
    lane = lax.broadcasted_iota(I32, pkw.shape, 1)
    kw_ref[...] = jnp.where(lane < IDX_DIM, _rope_cols(pkw, c, s1, s2)[0], pkw * IDX_HEADS ** -0.5)
    gate_ref[...] = jax.nn.sigmoid(dot(wgl_ref)).astype(BF16)


def _rope_tables(pos):
    half = ROT_DIM // 2
    inv = ROPE_THETA ** (-jnp.arange(half, dtype=F32) / half)
    ang = pos.astype(F32)[:, None] * inv[None, :]
    cos, sin = jnp.cos(ang), jnp.sin(ang)
    r = pos.shape[0]
    z = lambda n: jnp.zeros((r, n), F32)
    c = jnp.concatenate([cos, cos, jnp.ones((r, HEAD_DIM - ROT_DIM), F32)], axis=1)
    s1 = jnp.concatenate([-sin, z(HEAD_DIM - half)], axis=1)
    s2 = jnp.concatenate([z(half), sin, z(HEAD_DIM - ROT_DIM)], axis=1)
    two = lambda t: jnp.concatenate([t, t], axis=1)
    return two(c), two(s1), two(s2)


def _project(xp, xs, tabs, tab_map, g, ws, tr):
    n_first, n_second = xp.shape[0] // tr, xs.shape[0] // tr
    rows = xp.shape[0] + xs.shape[0]
    row = lambda w: pl.BlockSpec((tr, w), lambda i: (i, 0))
    tab = pl.BlockSpec((tr, LANES), lambda i: (tab_map(i), 0))
    in_specs = [pl.BlockSpec((tr, D_MODEL), lambda i: (jnp.minimum(i, n_first - 1), 0)),
                pl.BlockSpec((tr, D_MODEL), lambda i: (jnp.maximum(i - n_first, 0), 0)),
                _resident(g)] + [_resident(w) for w in ws] + [tab, tab, tab]
    widths = [(CONV_CH, F32), (N_HEADS * HEAD_DIM, BF16), (N_KV_HEADS * HEAD_DIM, F32), (N_KV_HEADS * HEAD_DIM, F32),
              (IDX_HEADS * IDX_DIM, BF16), (LANES, F32), (2 * D_MODEL, BF16)]
    return pl.pallas_call(
        functools.partial(_proj_kernel, n_first=n_first),
        out_shape=[jax.ShapeDtypeStruct((rows, w), dt) for w, dt in widths],
        grid=(n_first + n_second,),
        in_specs=in_specs,
        out_specs=[row(w) for w, _ in widths],
        compiler_params=_cparams(("arbitrary",)),
        name="proj",
    )(xp, xs, g, *ws, *tabs)


def _proj_seq_kernel(x_ref, g_ref, wa_ref, wq_ref, wk_ref, wv_ref, wqi_ref, wkw_ref, wgl_ref, c_ref, s1_ref, s2_ref,
                     a_ref, q_ref, k_ref, v_ref, qi_ref, kw_ref, gate_ref, kh_ref, vt_ref, kih_ref, wt_ref):
    x = x_ref[...]
    h = x * lax.rsqrt(jnp.mean(x * x, axis=-1, keepdims=True) + EPS) * g_ref[...]
    hb = h.astype(BF16)
    c, s1, s2 = c_ref[...], s1_ref[...], s2_ref[...]

    def dot(w_ref):
        return jnp.dot(hb, w_ref[...], preferred_element_type=F32)

    def put_heads(ref, j, col):
        ref[0, 2 * j] = col[:, :HEAD_DIM].astype(BF16)
        ref[0, 2 * j + 1] = col[:, HEAD_DIM:].astype(BF16)

    glu = dot(wa_ref)
    a_ref[...] = glu[:, :CONV_CH] * jax.nn.sigmoid(glu[:, CONV_CH:])
    for j, col in enumerate(_rope_cols(dot(wq_ref), c, s1, s2)):
        put_heads(q_ref, j, col * Q_SCALE)
    for j, col in enumerate(_rope_cols(dot(wk_ref), c, s1, s2)):
        k_ref[:, j * LANES:(j + 1) * LANES] = col
        put_heads(kh_ref, j, col)
    v = dot(wv_ref)
    v_ref[...] = v
    tr = v.shape[0]
    ones_row = jnp.where(lax.broadcasted_iota(I32, (VT_ROWS - HEAD_DIM, tr), 0) == 0, 1.0, 0.0).astype(BF16)
    for j in range(v.shape[1] // LANES):
        vt = v[:, j * LANES:(j + 1) * LANES].T
        for half in range(2):
            vt_ref[0, 2 * j + half, 0:HEAD_DIM, :] = vt[half * HEAD_DIM:(half + 1) * HEAD_DIM].astype(BF16)
            vt_ref[0, 2 * j + half, HEAD_DIM:VT_ROWS, :] = ones_row
    for j, col in enumerate(_rope_cols(dot(wqi_ref), c, s1, s2)):
        put_heads(qi_ref, j, col * IDX_DIM ** -0.5)
    pkw = dot(wkw_ref)
    lane = lax.broadcasted_iota(I32, pkw.shape, 1)
    kw = jnp.where(lane < IDX_DIM, _rope_cols(pkw, c, s1, s2)[0], pkw * IDX_HEADS ** -0.5)
    kw_ref[...] = kw[:, :IDX_DIM]
    kih_ref[0] = kw[:, :IDX_DIM].astype(BF16)
    wt_ref[0] = kw.T[IDX_DIM:IDX_DIM + IDX_HEADS, :]
    gate_ref[...] = jax.nn.sigmoid(dot(wgl_ref)).astype(BF16)


def _project_seq(x, nb, s, tabs, g, ws, tr):
    nt = s // tr
    rows = nb * s
    flat = lambda w: pl.BlockSpec((tr, w), lambda b, i: (b * nt + i, 0))
    tab = pl.BlockSpec((tr, LANES), lambda b, i: (i, 0))
    heads = lambda nh: pl.BlockSpec((1, nh, tr, HEAD_DIM), lambda b, i: (b, 0, i, 0))
    widths = [(CONV_CH, F32), None, (N_KV_HEADS * HEAD_DIM, F32), (N_KV_HEADS * HEAD_DIM, F32), None, (IDX_DIM, F32),
              (2 * D_MODEL, BF16)]
    shapes = [jax.ShapeDtypeStruct((rows, w[0]), w[1]) if w else None for w in widths]
    specs = [flat(w[0]) if w else None for w in widths]
    shapes[1], specs[1] = jax.ShapeDtypeStruct((nb, N_HEADS, s, HEAD_DIM), BF16), heads(N_HEADS)
    shapes[4], specs[4] = jax.ShapeDtypeStruct((nb, IDX_HEADS, s, IDX_DIM), BF16), heads(IDX_HEADS)
    shapes += [jax.ShapeDtypeStruct((nb, N_KV_HEADS, s, HEAD_DIM), BF16),
               jax.ShapeDtypeStruct((nb, N_KV_HEADS, VT_ROWS, s), BF16),
               jax.ShapeDtypeStruct((nb, s, IDX_DIM), BF16),
               jax.ShapeDtypeStruct((nb, IDX_HEADS, s), F32)]
    specs += [heads(N_KV_HEADS),
              pl.BlockSpec((1, N_KV_HEADS, VT_ROWS, tr), lambda b, i: (b, 0, 0, i)),
              pl.BlockSpec((1, tr, IDX_DIM), lambda b, i: (b, i, 0)),
              pl.BlockSpec((1, IDX_HEADS, tr), lambda b, i: (b, 0, i))]
    return pl.pallas_call(
        _proj_seq_kernel,
        out_shape=shapes,
        grid=(nb, nt),
        in_specs=[flat(D_MODEL), _resident(g)] + [_resident(w) for w in ws] + [tab, tab, tab],
        out_specs=specs,
        compiler_params=_cparams(("arbitrary", "arbitrary")),
        name="proj_seq",
    )(x, g, *ws, *tabs)


def _conv_kernel(cur_ref, halo_ref, lc_ref, cw_ref, cb_ref, lg_ref, lb_ref, wo_ref, out_ref, e_ref, c_ref, *, tr, rc):
    halo = jnp.where(pl.program_id(1) == 0, lc_ref[0], halo_ref[...])
    e_ref[0, 0:HALO, :] = halo
    e_ref[0, HALO:HALO + tr, :] = cur_ref[...]
    n = tr + HALO - 8
    for b in range(1, 8):
        e_ref[b, 0:n, :] = e_ref[0, b:b + n, :]
    first = HALO - (CONV_WIDTH - 1)

    def chunk(ci, carry):
        r0 = pl.multiple_of(ci * rc, rc)
        acc = jnp.zeros((rc, CONV_CH), F32) + cb_ref[...]
        for w in range(CONV_WIDTH):
            s = first + w
            acc = acc + e_ref[s % 8, pl.ds(r0 + 8 * (s // 8), rc), :] * cw_ref[pl.ds(w, 1), :]
        c_ref[pl.ds(r0, rc), :] = acc
        return carry

    lax.fori_loop(0, tr // rc, chunk, 0)
    c = c_ref[...]
    mu = jnp.mean(c, axis=-1, keepdims=True)
    xc = c - mu
    y = xc * lax.rsqrt(jnp.mean(xc * xc, axis=-1, keepdims=True) + EPS) * lg_ref[...] + lb_ref[...]
    act = (y * jax.nn.sigmoid(y)).astype(BF16)
    out_ref[...] = jnp.dot(act, wo_ref[...], preferred_element_type=F32).astype(BF16)


def _conv_branch(a_flat, row_off, nb, t, left_ctx, cw, cb, lg, lb, wo, tr):
    nt = t // tr
    rc = min(tr, 32)
    per_batch_ctx = left_ctx.shape[0] > 1
    cur = pl.BlockSpec((tr, CONV_CH), lambda b, i: (row_off // tr + b * nt + i, 0))
    halo = pl.BlockSpec((HALO, CONV_CH),
                        lambda b, i: (jnp.maximum((row_off + b * t + i * tr) // HALO - 1, 0), 0))
    lctx = pl.BlockSpec((1, HALO, CONV_CH), lambda b, i: (b if per_batch_ctx else 0, 0, 0))
    return pl.pallas_call(
        functools.partial(_conv_kernel, tr=tr, rc=rc),
        out_shape=jax.ShapeDtypeStruct((nb * t, D_MODEL), BF16),
        grid=(nb, nt),
        in_specs=[cur, halo, lctx, _resident(cw), _resident(cb), _resident(lg), _resident(lb), _resident(wo)],
        out_specs=pl.BlockSpec((tr, D_MODEL), lambda b, i: (b * nt + i, 0)),
        scratch_shapes=[pltpu.VMEM((8, tr + HALO, CONV_CH), F32), pltpu.VMEM((tr, CONV_CH), F32)],
        compiler_params=_cparams(("arbitrary", "arbitrary")),
        name="conv",
    )(a_flat, a_flat, left_ctx, cw, cb, lg, lb, wo)


def _float_to_skey(f):
    bits = lax.bitcast_convert_type(f, I32)
    return bits ^ ((bits >> 31) & 0x7FFFFFFF)


def _skey_to_float(key):
    return lax.bitcast_convert_type(key ^ ((key >> 31) & 0x7FFFFFFF), F32)


SKEY_LOWEST_FINITE = INT_MIN + 0x00800000


def _high_half(f):
    return lax.bitcast_convert_type(lax.bitcast_convert_type(f, I32) & -65536, F32)


def _key_to_float(u):
    key = u ^ INT_MIN
    bits = key ^ ((key >> 31) & 0x7FFFFFFF)
    return lax.bitcast_convert_type(bits, F32)


def _dsa_kernel(q_ref, qi_ref, kw_ref, k_ref, v_ref, ki_ref, o_ref, sc_ref, m_ref, acc_ref,
                *, tq, tk, k_sel, q_off, n_valid_end, idx_bits):
    i = pl.program_id(1)
    qf0 = q_off + i * tq
    klim = jnp.minimum(FRONT + ((qf0 + tq - 1) // CHUNK + 1) * CHUNK, n_valid_end)
    n_tiles = (klim + tk - 1) // tk
    ncol = tk // LANES
    kf = float(k_sel)
    group = N_HEADS // N_KV_HEADS

    kwv = kw_ref[...]
    qrow = lax.broadcasted_iota(I32, (tq, 1), 0)
    lim_row = jnp.minimum(FRONT + ((qf0 + qrow) // CHUNK + 1) * CHUNK, n_valid_end)

    def score_tile(t, carry):
        k0 = pl.multiple_of(t * tk, tk)
        kt = ki_ref[0, pl.ds(k0, tk), :]
        sc = jnp.zeros((tq, tk), F32)
        for h in range(IDX_HEADS):
            rel = lax.dot_general(qi_ref[0, h], kt, (((1,), (1,)), ((), ())), preferred_element_type=F32)
            sc = sc + kwv[:, IDX_DIM + h:IDX_DIM + h + 1] * jnp.maximum(rel, 0.0)
        j = k0 + lax.broadcasted_iota(I32, (tq, tk), 1)
        j = jnp.where(j >= FRONT - N_META, j, n_valid_end)
        sc_ref[:, pl.ds(k0, tk)] = jnp.where(j < lim_row, sc, NEG_INF)
        return carry

    lax.fori_loop(0, n_tiles, score_tile, 0)

    lane = lax.broadcasted_iota(I32, (tq, LANES), 1)

    def count(pred):
        def body(t, acc):
            k0 = pl.multiple_of(t * tk, tk)
            for c in range(ncol):
                blk = sc_ref[:, pl.ds(k0 + c * LANES, LANES)]
                acc = acc + pred(blk, k0 + c * LANES)
            return acc
        acc = lax.fori_loop(0, n_tiles, body, jnp.zeros((tq, LANES), F32))
        return jnp.broadcast_to(jnp.sum(acc, axis=1, keepdims=True), (tq, LANES))

    def bisect(it, u):
        trial = u | jnp.left_shift(jnp.int32(1), 31 - it)
        thr = _key_to_float(trial)
        cnt = count(lambda blk, j0: jnp.where(blk >= thr, 1.0, 0.0))
        return jnp.where(cnt >= kf, trial, u)

    u = lax.fori_loop(0, 32, bisect, jnp.zeros((tq, LANES), I32))
    key = jnp.maximum(u ^ INT_MIN, INT_MIN + 0x00800000)
    thr = _key_to_float(key ^ INT_MIN)
    n_ge = count(lambda blk, j0: jnp.where(blk >= thr, 1.0, 0.0))
    tied = jnp.max(n_ge) > kf

    def write_bias(sel):
        def body(t, carry):
            k0 = pl.multiple_of(t * tk, tk)
            for c in range(ncol):
                blk = sc_ref[:, pl.ds(k0 + c * LANES, LANES)]
                sc_ref[:, pl.ds(k0 + c * LANES, LANES)] = sel(blk, k0 + c * LANES)
            return carry
        lax.fori_loop(0, n_tiles, body, 0)

    @pl.when(jnp.logical_not(tied))
    def _():
        write_bias(lambda blk, j0: jnp.where(blk >= thr, 0.0, NEG_INF))

    @pl.when(tied)
    def _():
        need = kf - count(lambda blk, j0: jnp.where(blk > thr, 1.0, 0.0))

        def jbis(it, ans):
            trial = ans + jnp.left_shift(jnp.int32(1), idx_bits - 1 - it)
            cnt = count(lambda blk, j0: jnp.where(blk == thr, jnp.where(j0 + lane < trial, 1.0, 0.0), 0.0))
            return jnp.where(cnt < need, trial, ans)

        ans = lax.fori_loop(0, idx_bits, jbis, jnp.zeros((tq, LANES), I32))
        write_bias(lambda blk, j0: jnp.where(
            blk > thr, 0.0, jnp.where(blk == thr, jnp.where(j0 + lane <= ans, 0.0, NEG_INF), NEG_INF)))

    m_ref[...] = jnp.full(m_ref.shape, M_INIT, F32)
    acc_ref[...] = jnp.zeros(acc_ref.shape, F32)

    def attend_tile(t, carry):
        k0 = pl.multiple_of(t * tk, tk)
        bias = sc_ref[:, pl.ds(k0, tk)]
        for g in range(N_KV_HEADS):
            qg = q_ref[0, g * group:(g + 1) * group].reshape(group * tq, HEAD_DIM)
            s = lax.dot_general(qg, k_ref[0, g, pl.ds(k0, tk), :], (((1,), (1,)), ((), ())),
                                preferred_element_type=F32)
            s = (s.reshape(group, tq, tk) + bias[None]).reshape(group * tq, tk)
            m_prev = m_ref[g]
            m_new = jnp.maximum(m_prev, jnp.max(s, axis=1, keepdims=True))
            p = jnp.exp2(s - jnp.concatenate([m_new] * ncol, axis=1))
            pv = jnp.dot(p.astype(BF16), v_ref[0, g, pl.ds(k0, tk), :], preferred_element_type=F32)
            acc_ref[g] = jnp.exp2(m_prev - m_new) * acc_ref[g] + pv
            m_ref[g] = m_new
        return carry

    lax.fori_loop(0, n_tiles, attend_tile, 0)

    for g in range(N_KV_HEADS):
        acc = acc_ref[g]
        o = acc[:, :HEAD_DIM] / acc[:, HEAD_DIM:HEAD_DIM + 1]
        for r in range(0, group, 2):
            pair = jnp.concatenate([o[r * tq:(r + 1) * tq], o[(r + 1) * tq:(r + 2) * tq]], axis=1)
            h0 = g * group + r
            o_ref[:, h0 * HEAD_DIM:(h0 + 2) * HEAD_DIM] = pair.astype(BF16)


def _dsa(qh, qih, kw_flat, kw_row_off, kh, vh, kih, *, tq, tk, k_sel, q_off, n_valid_end):
    nb, _, s, _ = qh.shape
    lp = kh.shape[2]
    nq = s // tq
    kern = functools.partial(_dsa_kernel, tq=tq, tk=tk, k_sel=k_sel, q_off=q_off, n_valid_end=n_valid_end,
                             idx_bits=max(1, int(lp).bit_length()))
    per_batch = lambda shp: pl.BlockSpec((1,) + shp, lambda b, i: (b,) + (0,) * len(shp),
                                         pipeline_mode=pl.Buffered(2 if nq == 1 else 1))
    return pl.pallas_call(
        kern,
        out_shape=jax.ShapeDtypeStruct((nb * s, N_HEADS * HEAD_DIM), BF16),
        grid=(nb, nq),
        in_specs=[pl.BlockSpec((1, N_HEADS, tq, HEAD_DIM), lambda b, i: (b, 0, i, 0)),
                  pl.BlockSpec((1, IDX_HEADS, tq, IDX_DIM), lambda b, i: (b, 0, i, 0)),
                  pl.BlockSpec((tq, LANES), lambda b, i: (kw_row_off // tq + b * nq + i, 0)),
                  per_batch((N_KV_HEADS, lp, HEAD_DIM)), per_batch((N_KV_HEADS, lp, LANES)), per_batch((lp, IDX_DIM))],
        out_specs=pl.BlockSpec((tq, N_HEADS * HEAD_DIM), lambda b, i: (b * nq + i, 0)),
        scratch_shapes=[pltpu.VMEM((tq, (lp // LANES | 1) * LANES), F32),
                        pltpu.VMEM((N_KV_HEADS, (N_HEADS // N_KV_HEADS) * tq, LANES), F32),
                        pltpu.VMEM((N_KV_HEADS, (N_HEADS // N_KV_HEADS) * tq, LANES), F32)],
        compiler_params=_cparams(("arbitrary", "arbitrary")),
        name="dsa",
    )(qh, qih, kw_flat, kh, vh, kih)


def _dsa_t_kernel(q_ref, qi_ref, wt_ref, k_ref, vt_ref, ki_ref, o_ref, sc_ref, sch_ref, m_ref, mu_ref, acc_ref,
                  s_ref, s2_ref,
                  *, tk, k_sel, n_valid_end, idx_bits):
    tq = LANES
    i = pl.program_id(1)
    qf0 = i * tq
    klim = jnp.minimum(FRONT + qf0 + tq, n_valid_end)
    n_tiles = (klim + tk - 1) // tk
    kf = float(k_sel)
    group = N_HEADS // N_KV_HEADS
    fold = 32
    fold_h = 64
    assert sch_ref.shape[0] // fold_h <= 256

    wt = wt_ref[0]
    qlane = lax.broadcasted_iota(I32, (1, tq), 1)
    lim = jnp.minimum(FRONT + ((qf0 + qlane) // CHUNK + 1) * CHUNK, n_valid_end)
    qi_all = qi_ref[0].reshape(IDX_HEADS * tq, IDX_DIM)

    def score_tile(t, carry):
        th = tk // 2
        for half in range(2):
            k0 = pl.multiple_of(t * tk, tk) + half * th
            rel = lax.dot_general(ki_ref[0, pl.ds(k0, th), :], qi_all, (((1,), (1,)), ((), ())),
                                  preferred_element_type=F32)
            sc = jnp.zeros((th, tq), F32)
            for h in range(IDX_HEADS):
                sc = sc + wt[h:h + 1, :] * jnp.maximum(rel[:, h * tq:(h + 1) * tq], 0.0)
            j = k0 + lax.broadcasted_iota(I32, (th, tq), 0)
            j = jnp.where(j >= FRONT - N_META, j, n_valid_end)
            sc = jnp.where(j < lim, sc, NEG_INF)
            sc_ref[pl.ds(k0, th), :] = sc
            sch_ref[pl.ds(k0, th), :] = _high_half(sc).astype(HALF)
        return carry

    lax.fori_loop(0, n_tiles, score_tile, 0)

    def count(pred):
        def body(t, acc):
            k0 = pl.multiple_of(t * tk, tk)
            ones = pred(sc_ref[pl.ds(k0, tk), :], k0)
            return acc + jnp.sum(ones.reshape(tk // fold, fold, tq), axis=0)
        acc = lax.fori_loop(0, n_tiles, body, jnp.zeros((fold, tq), F32))
        return jnp.sum(acc, axis=0, keepdims=True)

    def count_high(thr):
        thr_h = _high_half(thr).astype(HALF)
        one, zero = jnp.ones((), HALF), jnp.zeros((), HALF)

        def body(t, acc):
            ones = jnp.where(sch_ref[pl.ds(pl.multiple_of(t * tk, tk), tk), :] >= thr_h, one, zero)
            for c in range(tk // fold_h):
                acc = acc + ones[c * fold_h:(c + 1) * fold_h]
            return acc
        acc = lax.fori_loop(0, n_tiles, body, jnp.zeros((fold_h, tq), HALF))
        return jnp.sum(acc.astype(F32), axis=0, keepdims=True)

    def bisect(counter):
        def step(it, carry):
            u, c_u = carry
            trial = u | jnp.left_shift(jnp.int32(1), 31 - it)
            cnt = counter(_key_to_float(trial))
            take = cnt >= kf
            return jnp.where(take, trial, u), jnp.where(take, cnt, c_u)
        return step

    n_adm = (lim - (FRONT - N_META)).astype(F32)
    low_step = bisect(lambda thr: count(lambda blk, k0: jnp.where(blk >= thr, 1.0, 0.0)))

    def unsettled(c_u):
        return jnp.max(jnp.where((c_u == kf) | (n_adm <= kf), 0, 1))

    def low_bits(state):
        it, _, u, c_u = state
        for j in range(LOW_GROUP):
            u, c_u = low_step(it + j, (u, c_u))
        return it + LOW_GROUP, unsettled(c_u), u, c_u

    u, c_u = lax.fori_loop(0, 16, bisect(count_high), (jnp.zeros((1, tq), I32), jnp.full((1, tq), jnp.inf, F32)))
    _, _, u, c_u = lax.while_loop(lambda state: (state[0] < 32) & (state[1] > 0), low_bits,
                                  (jnp.int32(16), unsettled(c_u), u, c_u))
    key = jnp.maximum(u ^ INT_MIN, SKEY_LOWEST_FINITE)
    thr = _skey_to_float(key)
    n_ge = jnp.minimum(c_u, n_adm)
    tied = jnp.max(n_ge) > kf

    def write_bias(sel):
        def body(t, carry):
            k0 = pl.multiple_of(t * tk, tk)
            sc_ref[pl.ds(k0, tk), :] = sel(sc_ref[pl.ds(k0, tk), :], k0)
            return carry
        lax.fori_loop(0, n_tiles, body, 0)

    @pl.when(jnp.logical_not(tied))
    def _():
        write_bias(lambda blk, k0: jnp.where(blk >= thr, 0.0, NEG_INF))

    @pl.when(tied)
    def _():
        need = kf - count(lambda blk, k0: jnp.where(blk > thr, 1.0, 0.0))
        krow = lax.broadcasted_iota(I32, (tk, tq), 0)

        def jbis(it, ans):
            trial = ans + jnp.left_shift(jnp.int32(1), idx_bits - 1 - it)
            cnt = count(lambda blk, k0: jnp.where(blk == thr, jnp.where(k0 + krow < trial, 1.0, 0.0), 0.0))
            return jnp.where(cnt < need, trial, ans)

        ans = lax.fori_loop(0, idx_bits, jbis, jnp.zeros((1, tq), I32))
        write_bias(lambda blk, k0: jnp.where(
            blk > thr, 0.0, jnp.where(blk == thr, jnp.where(k0 + krow <= ans, 0.0, NEG_INF), NEG_INF)))

    m_ref[...] = jnp.full(m_ref.shape, M_INIT, F32)
    acc_ref[...] = jnp.zeros(acc_ref.shape, F32)

    mu_ref[...] = jnp.full(mu_ref.shape, M_INIT, F32)

    def logits(t, buf, g, m_before):
        k0 = pl.multiple_of(t * tk, tk)
        bias = sc_ref[pl.ds(k0, tk), :]
        qg = q_ref[0, g * group:(g + 1) * group].reshape(group * tq, HEAD_DIM)
        s = lax.dot_general(k_ref[0, g, pl.ds(k0, tk), :], qg, (((1,), (1,)), ((), ())),
                            preferred_element_type=F32)
        s = jnp.concatenate([s[:, r * tq:(r + 1) * tq] + bias for r in range(group)], axis=1)
        buf[g] = s
        return jnp.maximum(m_before, jnp.max(s, axis=0, keepdims=True))

    def values(t, buf, g, m_t):
        p = jnp.exp2(buf[g] - m_t)
        pv = jnp.dot(vt_ref[0, g, :, pl.ds(pl.multiple_of(t * tk, tk), tk)], p.astype(BF16),
                     preferred_element_type=F32)
        acc_ref[g] = jnp.exp2(mu_ref[g] - m_t) * acc_ref[g] + pv
        mu_ref[g] = m_t

    def step(t_next, buf_next, t, buf):
        for g in range(N_KV_HEADS):
            m_t = m_ref[g]
            m_ref[g] = logits(t_next, buf_next, g, m_t)
            values(t, buf, g, m_t)

    for g in range(N_KV_HEADS):
        m_ref[g] = logits(0, s_ref, g, m_ref[g])

    def tile_pair(j, carry):
        step(2 * j + 1, s2_ref, 2 * j, s_ref)
        step(jnp.minimum(2 * j + 2, n_tiles - 1), s_ref, 2 * j + 1, s2_ref)
        return carry

    lax.fori_loop(0, n_tiles >> 1, tile_pair, 0)

    @pl.when((n_tiles & 1) == 1)
    def _():
        for g in range(N_KV_HEADS):
            values(n_tiles - 1, s_ref, g, m_ref[g])

    for g in range(N_KV_HEADS):
        acc = acc_ref[g]
        o = acc[:HEAD_DIM, :] / acc[HEAD_DIM:HEAD_DIM + 1, :]
        for r in range(0, group, 2):
            pair = jnp.concatenate([o[:, r * tq:(r + 1) * tq], o[:, (r + 1) * tq:(r + 2) * tq]], axis=0)
            h0 = g * group + r
            o_ref[:, h0 * HEAD_DIM:(h0 + 2) * HEAD_DIM] = pair.T.astype(BF16)


def _dsa_t(qh, qih, wt, kh, vt, kih, *, tk, k_sel, n_valid_end):
    nb, _, s, _ = qh.shape
    lp = kh.shape[2]
    tq = LANES
    nq = s // tq
    group = N_HEADS // N_KV_HEADS
    kern = functools.partial(_dsa_t_kernel, tk=tk, k_sel=k_sel, n_valid_end=n_valid_end,
                             idx_bits=max(1, int(lp).bit_length()))
    per_batch = lambda shp: pl.BlockSpec((1,) + shp, lambda b, i: (b,) + (0,) * len(shp), pipeline_mode=pl.Buffered(1))
    return pl.pallas_call(
        kern,
        out_shape=jax.ShapeDtypeStruct((nb * s, N_HEADS * HEAD_DIM), BF16),
        grid=(nb, nq),
        in_specs=[pl.BlockSpec((1, N_HEADS, tq, HEAD_DIM), lambda b, i: (b, 0, i, 0)),
                  pl.BlockSpec((1, IDX_HEADS, tq, IDX_DIM), lambda b, i: (b, 0, i, 0)),
                  pl.BlockSpec((1, IDX_HEADS, tq), lambda b, i: (b, 0, i)),
                  per_batch((N_KV_HEADS, lp, HEAD_DIM)), per_batch((N_KV_HEADS, VT_ROWS, lp)), per_batch((lp, IDX_DIM))],
        out_specs=pl.BlockSpec((tq, N_HEADS * HEAD_DIM), lambda b, i: (b * nq + i, 0)),
        scratch_shapes=[pltpu.VMEM((lp, tq), F32), pltpu.VMEM((lp, tq), HALF),
                        pltpu.VMEM((N_KV_HEADS, 1, group * tq), F32),
                        pltpu.VMEM((N_KV_HEADS, 1, group * tq), F32),
                        pltpu.VMEM((N_KV_HEADS, VT_ROWS, group * tq), F32),
                        pltpu.VMEM((N_KV_HEADS, tk, group * tq), F32),
                        pltpu.VMEM((N_KV_HEADS, tk, group * tq), F32)],
        compiler_params=_cparams(("arbitrary", "arbitrary")),
        name="dsa_t",
    )(qh, qih, wt, kh, vt, kih)


def _pad_keys(meta_rows, mid_rows, lp):
    nb = mid_rows[0].shape[0]
    tail = mid_rows[0].shape[2:]
    meta_b = jnp.broadcast_to(meta_rows[None], (nb,) + meta_rows.shape)
    used = FRONT + sum(m.shape[1] for m in mid_rows)
    parts = [jnp.zeros((nb, FRONT - N_META) + tail, meta_rows.dtype), meta_b] + list(mid_rows)
    if lp > used:
        parts.append(jnp.zeros((nb, lp - used) + tail, meta_rows.dtype))
    return jnp.concatenate(parts, axis=1)


def _attention_inputs(k_meta, v_meta, ki_meta, k_mid, v_mid, ki_mid, lp, keys_on_lanes):
    nb = k_mid[0].shape[0]
    kp = _pad_keys(k_meta, k_mid, lp).astype(BF16).reshape(nb, lp, N_KV_HEADS, HEAD_DIM).transpose(0, 2, 1, 3)
    vp = _pad_keys(v_meta, v_mid, lp).astype(BF16).reshape(nb, lp, N_KV_HEADS, HEAD_DIM)
    if keys_on_lanes:
        vp = vp.transpose(0, 2, 3, 1)
        ones = jnp.ones((nb, N_KV_HEADS, 1, lp), BF16)
        vp = jnp.concatenate([vp, ones, jnp.zeros((nb, N_KV_HEADS, VT_ROWS - HEAD_DIM - 1, lp), BF16)], axis=2)
    else:
        vp = vp.transpose(0, 2, 1, 3)
        ones = jnp.ones(vp.shape[:3] + (1,), BF16)
        vp = jnp.concatenate([vp, ones, jnp.zeros(vp.shape[:3] + (LANES - HEAD_DIM - 1,), BF16)], axis=3)
    kip = _pad_keys(ki_meta, ki_mid, lp).astype(BF16)
    return kp, vp, kip


def _merge_kernel(xp_ref, xs_ref, cp_ref, cs_ref, ap_ref, as_ref, gp_ref, gs_ref, wao_ref, wout_ref, gffn_ref, wr_ref,
                  br_ref, x1_ref, hn_ref, ids_ref, gts_ref, *, n_first):
    first = pl.program_id(0) < n_first
    x = jnp.where(first, xp_ref[...], xs_ref[...])
    conv = jnp.where(first, cp_ref[...], cs_ref[...]).astype(F32)
    attn = jnp.where(first, ap_ref[...], as_ref[...])
    ao = jnp.dot(attn, wao_ref[...], preferred_element_type=F32)
    gates = jnp.where(first, gp_ref[...], gs_ref[...]).astype(F32)
    mix = gates[:, :D_MODEL] * conv + gates[:, D_MODEL:] * ao
    x1 = x + jnp.dot(mix.astype(BF16), wout_ref[...], preferred_element_type=F32)
    x1_ref[...] = x1
    hn = x1 * lax.rsqrt(jnp.mean(x1 * x1, axis=-1, keepdims=True) + EPS) * gffn_ref[...]
    tr = hn.shape[0]
    for c in range(SUB):
        hn_ref[pl.ds(c, tr, stride=SUB), :] = hn[:, c * LANES:(c + 1) * LANES]
    logits = jnp.dot(hn, wr_ref[...], preferred_element_type=F32, precision=lax.Precision.HIGHEST) + br_ref[...]
    lane = lax.broadcasted_iota(I32, logits.shape, 1)
    lg = jnp.where(lane < N_EXPERTS, logits, NEG_INF)
    vals, ids = [], []
    for _ in range(TOP_K_EXPERTS):
        m = jnp.max(lg, axis=1, keepdims=True)
        idx = jnp.min(jnp.where(lg == m, lane, LANES), axis=1, keepdims=True)
        vals.append(m)
        ids.append(idx)
        lg = jnp.where(lane == idx, NEG_INF, lg)
    es = [jnp.exp(v - vals[0]) for v in vals]
    den = es[0]
    for e in es[1:]:
        den = den + e
    ids_out = jnp.zeros(logits.shape, I32)
    gts_out = jnp.zeros(logits.shape, F32)
    for j in range(TOP_K_EXPERTS):
        ids_out = jnp.where(lane == j, ids[j], ids_out)
        gts_out = jnp.where(lane == j, es[j] / den, gts_out)
    ids_ref[...] = ids_out[:, :SUB]
    gts_ref[...] = gts_out


def _merge(xp, xs, conv_p, conv_s, attn_p, attn_s, gates_p, gates_s, wao, wout, gffn, wr, br, tr):
    n_first, n_second = xp.shape[0] // tr, xs.shape[0] // tr
    rows = xp.shape[0] + xs.shape[0]
    first = lambda w: pl.BlockSpec((tr, w), lambda i: (jnp.minimum(i, n_first - 1), 0))
    second = lambda w: pl.BlockSpec((tr, w), lambda i: (jnp.maximum(i - n_first, 0), 0))
    row = lambda w: pl.BlockSpec((tr, w), lambda i: (i, 0))
    return pl.pallas_call(
        functools.partial(_merge_kernel, n_first=n_first),
        out_shape=[jax.ShapeDtypeStruct((rows, D_MODEL), F32), jax.ShapeDtypeStruct((rows * SUB, LANES), F32),
                   jax.ShapeDtypeStruct((rows, SUB), I32), jax.ShapeDtypeStruct((rows, LANES), F32)],
        grid=(n_first + n_second,),
        in_specs=[first(D_MODEL), second(D_MODEL), first(D_MODEL), second(D_MODEL), first(D_MODEL), second(D_MODEL),
                  first(2 * D_MODEL), second(2 * D_MODEL), _resident(wao), _resident(wout), _resident(gffn),
                  _resident(wr), _resident(br)],
        out_specs=[row(D_MODEL), pl.BlockSpec((tr * SUB, LANES), lambda i: (i, 0)), row(SUB), row(LANES)],
        compiler_params=_cparams(("arbitrary",)),
        name="merge",
    )(xp, xs, conv_p, conv_s, attn_p, attn_s, gates_p, gates_s, wao, wout, gffn, wr, br)


KEY_ALIGN = 1024


def _key_window(tm):
    return -(-(tm + KEY_ALIGN - 1) // KEY_ALIGN) * KEY_ALIGN


def _for_rows(n, fn):
    def group(gi, carry):
        for u in range(SUB):
            fn(gi * SUB + u)
        return carry

    def single(r, carry):
        fn(r)
        return carry

    shift = SUB.bit_length() - 1
    lax.fori_loop(0, n >> shift, group, 0)
    lax.fori_loop((n >> shift) << shift, n, single, 0)


def _moe_kernel(te_ref, base_ref, nval_ref, nreal_ref, src_hbm, dst_hbm, h_hbm, w1_ref, b1_ref, w3_ref, b3_ref, w2_ref,
                b2_ref, y_hbm, src_sm, dst_sm, xbuf, xb, ybuf, wb, ksem, gsem, ssem, *, tm):
    i = pl.program_id(0)
    nreal = nreal_ref[0]
    window = src_sm.shape[0] // 3

    def rows_copy(tile):
        s = lax.rem(tile, 3)
        start = pl.multiple_of(base_ref[tile] & -KEY_ALIGN, KEY_ALIGN)
        slot = pl.ds(pl.multiple_of(s * window, KEY_ALIGN), window)
        return (pltpu.make_async_copy(src_hbm.at[pl.ds(start, window)], src_sm.at[slot], ksem.at[0, s]),
                pltpu.make_async_copy(dst_hbm.at[pl.ds(start, window)], dst_sm.at[slot], ksem.at[1, s]))

    def keys_start(tile):
        for copy in rows_copy(tile):
            copy.start()

    def keys_wait(tile):
        for copy in rows_copy(tile):
            copy.wait()

    def row_reader(table, tile):
        first = lax.rem(tile, 3) * window + (base_ref[tile] & (KEY_ALIGN - 1))
        return lambda r: pl.multiple_of(table[first + r], SUB)

    def gather_row(s, r, src_row):
        return pltpu.make_async_copy(h_hbm.at[pl.ds(src_row, SUB)], xbuf.at[s, pl.ds(r * SUB, SUB)], gsem.at[s])

    def scatter_row(r, dst_row):
        return pltpu.make_async_copy(ybuf.at[pl.ds(r * SUB, SUB)], y_hbm.at[pl.ds(dst_row, SUB)], ssem.at[0])

    def start_gather(tile):
        src_of, s = row_reader(src_sm, tile), tile & 1
        _for_rows(tm, lambda r: gather_row(s, r, src_of(r)).start())

    @pl.when(jnp.logical_and(i == 0, nreal > 0))
    def _():
        keys_start(0)

        @pl.when(nreal > 1)
        def _():
            keys_start(1)

        keys_wait(0)
        start_gather(0)

    @pl.when(i + 1 < nreal)
    def _():
        keys_wait(i + 1)
        start_gather(i + 1)

    @pl.when(i + 2 < nreal)
    def _():
        keys_start(i + 2)

    @pl.when(i < nreal)
    def _():
        s = i & 1
        _for_rows(tm, lambda r: gather_row(s, r, 0).wait())
        for c in range(SUB):
            xb[:, c * LANES:(c + 1) * LANES] = xbuf[s, pl.ds(c, tm, stride=SUB), :].astype(BF16)
        @pl.when(jnp.logical_or(i == 0, te_ref[i] != te_ref[jnp.maximum(i - 1, 0)]))
        def _():
            for j, w_ref in enumerate((w1_ref, w3_ref, w2_ref)):
                wb[j] = w_ref[0].astype(BF16)

        x = xb[...]
        g = jnp.minimum(jnp.dot(x, wb[0], preferred_element_type=F32) + b1_ref[0], SWIGLU_LIMIT)
        u = jnp.clip(jnp.dot(x, wb[1], preferred_element_type=F32) + b3_ref[0], -SWIGLU_LIMIT, SWIGLU_LIMIT)
        act = g * jax.nn.sigmoid(SWIGLU_ALPHA * g) * (u + 1.0)
        y = jnp.dot(act.astype(BF16), wb[2], preferred_element_type=F32) + b2_ref[0]

        @pl.when(i > 0)
        def _():
            _for_rows(nval_ref[i - 1], lambda r: scatter_row(r, 0).wait())

        for c in range(SUB):
            ybuf[pl.ds(c, tm, stride=SUB), :] = y[:, c * LANES:(c + 1) * LANES]

        dst_of = row_reader(dst_sm, i)
        _for_rows(nval_ref[i], lambda r: scatter_row(r, dst_of(r)).start())

        @pl.when(i == nreal - 1)
        def _():
            _for_rows(nval_ref[i], lambda r: scatter_row(r, 0).wait())


def _moe(hn, src_row, dst_row, tile_expert, base, nval, n_real, w1, b1, w3, b3, w2, b2, n_tok, tm):
    nt = tile_expert.shape[0]
    by_expert = lambda i, te, *_: (te[i], 0, 0)
    wspec = pl.BlockSpec((1, D_MODEL, D_FF), by_expert)
    w2spec = pl.BlockSpec((1, D_FF, D_MODEL), by_expert)
    bspec = pl.BlockSpec((1, 1, D_FF), by_expert)
    any_spec = pl.BlockSpec(memory_space=pl.ANY)
    grid_spec = pltpu.PrefetchScalarGridSpec(
        num_scalar_prefetch=4,
        grid=(nt,),
        in_specs=[any_spec, any_spec, any_spec, wspec, bspec, wspec, bspec, w2spec, bspec],
        out_specs=any_spec,
        scratch_shapes=[pltpu.SMEM((3 * _key_window(tm),), I32), pltpu.SMEM((3 * _key_window(tm),), I32),
                        pltpu.VMEM((2, tm * SUB, LANES), F32), pltpu.VMEM((tm, D_MODEL), BF16),
                        pltpu.VMEM((tm * SUB, LANES), F32), pltpu.VMEM((3, D_MODEL, D_FF), BF16),
                        pltpu.SemaphoreType.DMA((2, 3)), pltpu.SemaphoreType.DMA((2,)), pltpu.SemaphoreType.DMA((1,))],
    )
    return pl.pallas_call(
        functools.partial(_moe_kernel, tm=tm),
        out_shape=jax.ShapeDtypeStruct((TOP_K_EXPERTS * n_tok * SUB, LANES), F32),
        grid_spec=grid_spec,
        compiler_params=_cparams(("arbitrary",)),
        name="moe",
    )(tile_expert, base, nval, n_real, src_row, dst_row, hn, w1, b1, w3, b3, w2, b2)


def _route(ids, n, tm):
    n_pairs = n * TOP_K_EXPERTS
    pair_bits = max(1, (n_pairs - 1).bit_length())
    assert pair_bits + (N_EXPERTS - 1).bit_length() <= 31 and tm % LANES == 0
    assert TOP_K_EXPERTS & (TOP_K_EXPERTS - 1) == 0
    pair = jnp.arange(n_pairs, dtype=I32)
    keys = jnp.sort(jnp.left_shift(ids.reshape(-1), pair_bits) | pair)
    starts = jnp.searchsorted(keys, jnp.left_shift(jnp.arange(N_EXPERTS + 1, dtype=I32), pair_bits),
                              method="compare_all").astype(I32)
    counts = starts[1:] - starts[:-1]
    tiles = (counts + tm - 1) // tm
    tile_end = jnp.cumsum(tiles)
    nt = -(-(n_pairs + N_EXPERTS * (tm - 1)) // tm)
    ti = jnp.arange(nt, dtype=I32)
    te = jnp.minimum(jnp.searchsorted(tile_end, ti, side="right", method="compare_all"), N_EXPERTS - 1).astype(I32)
    within = ti - (tile_end - tiles)[te]
    base = jnp.clip(starts[:-1][te] + within * tm, 0, n_pairs - 1).astype(I32)
    nval = jnp.clip(counts[te] - within * tm, 0, tm).astype(I32)
    sorted_pair = keys & ((1 << pair_bits) - 1)
    token, slot = sorted_pair // TOP_K_EXPERTS, sorted_pair % TOP_K_EXPERTS
    padded = -(-n_pairs // KEY_ALIGN) * KEY_ALIGN + _key_window(tm)
    src_row = jnp.pad(token * SUB, (0, padded - n_pairs))
    dst_row = jnp.pad((slot * n + token) * SUB, (0, padded - n_pairs))
    return src_row, dst_row, te, base, nval, tile_end[-1:].astype(I32)


def _final_kernel(x1_ref, y0_ref, y1_ref, y2_ref, y3_ref, gts_ref, g_ref, out_ref):
    gts = gts_ref[...]
    tr = gts.shape[0]
    cols = []
    for c in range(SUB):
        acc = x1_ref[:, c * LANES:(c + 1) * LANES]
        for j, y_ref in enumerate((y0_ref, y1_ref, y2_ref, y3_ref)):
            acc = acc + gts[:, j:j + 1] * y_ref[pl.ds(c, tr, stride=SUB), :]
        cols.append(acc)
    ss = cols[0] * cols[0]
    for acc in cols[1:]:
        ss = ss + acc * acc
    scale = lax.rsqrt(jnp.sum(ss, axis=-1, keepdims=True) / D_MODEL + EPS)
    for c, acc in enumerate(cols):
        out_ref[:, c * LANES:(c + 1) * LANES] = acc * scale * g_ref[:, c * LANES:(c + 1) * LANES]


def _final(x1, yslots, gts, g, tr, row0, rows):
    nblk = x1.shape[0] // tr
    first = row0 // tr
    row = lambda w: pl.BlockSpec((tr, w), lambda i: (first + i, 0))
    slot = lambda j: pl.BlockSpec((tr * SUB, LANES), lambda i: (j * nblk + first + i, 0))
    return pl.pallas_call(
        _final_kernel,
        out_shape=jax.ShapeDtypeStruct((rows, D_MODEL), F32),
        grid=(rows // tr,),
        in_specs=[row(D_MODEL)] + [slot(j) for j in range(TOP_K_EXPERTS)] + [row(LANES), _resident(g)],
        out_specs=pl.BlockSpec((tr, D_MODEL), lambda i: (i, 0)),
        compiler_params=_cparams(("arbitrary",)),
        name="final",
    )(x1, yslots, yslots, yslots, yslots, gts, g)


def _tile(n, pref):
    t = min(n, pref)
    while n % t:
        t //= 2
    return t


def kernel(x_prompt, x_sample, cache_k, cache_v, cache_idx_k, state_conv, meta_tokens, g_norm_mix, w_in, conv_w,
           conv_b, conv_ln_g, conv_ln_b, w_conv_out, w_attn_out, w_out, g_norm_ffn, w_router, b_router,
           w1, b1, w3, b3, w2, b2, g_norm_final):
    nb, s_p, _ = x_prompt.shape
    db, s_n, _ = x_sample.shape
    past = cache_k.shape[1]
    rows_p, rows_s = nb * s_p, db * s_n
    n = rows_p + rows_s
    tr = _tile(rows_s, ROW_TILE)
    assert rows_p % tr == 0 and s_p % tr == 0 and tr % s_n == 0 and s_p % Q_TILE == 0

    cuts = [0, 2 * CONV_CH, N_HEADS * HEAD_DIM, N_KV_HEADS * HEAD_DIM, N_KV_HEADS * HEAD_DIM, IDX_HEADS * IDX_DIM,
            IDX_DIM + IDX_HEADS, 2 * D_MODEL]
    offs = [sum(cuts[:i + 1]) for i in range(len(cuts))]
    seg = [w_in[:, offs[i]:offs[i + 1]].astype(BF16) for i in range(len(cuts) - 1)]
    seg[5] = jnp.pad(seg[5], ((0, 0), (0, LANES - seg[5].shape[1])))
    g_mix = g_norm_mix.reshape(1, D_MODEL)

    tabs_p = _rope_tables(N_META + jnp.arange(s_p, dtype=I32))
    tabs_s = _rope_tables(jnp.tile(N_META + past + jnp.arange(s_n, dtype=I32), tr // s_n))
    tabs_m = _rope_tables(jnp.arange(N_META, dtype=I32))

    xp, xs = x_prompt.reshape(rows_p, D_MODEL), x_sample.reshape(rows_s, D_MODEL)
    a_p, qh_p, k_p, v_p, qih_p, kw_p, gates_p, kh_mid, vt_mid, ki_mid, wt_p = _project_seq(
        xp, nb, s_p, tabs_p, g_mix, seg, tr)
    a_s, q_s, k_s, v_s, qi_s, kw_s, gates_s = [t[:rows_s] for t in _project(xs, xs, tabs_s, lambda i: 0, g_mix, seg, tr)]
    meta = meta_tokens.astype(F32)
    a_m, _, k_m, v_m, _, kw_m, _ = _project(meta, meta, tabs_m, lambda i: 0, g_mix, seg, N_META)
    a_m, k_m, v_m, ki_m = a_m[:N_META], k_m[:N_META], v_m[:N_META], kw_m[:N_META, :IDX_DIM]

    cw = jnp.pad(conv_w, ((0, HALO - CONV_WIDTH), (0, 0)))
    cb, lg, lb = conv_b.reshape(1, -1), conv_ln_g.reshape(1, -1), conv_ln_b.reshape(1, -1)
    wco = w_conv_out.astype(BF16)
    ctx_p = jnp.concatenate([jnp.zeros((HALO - N_META, CONV_CH), F32), a_m], axis=0)[None]
    ctx_s = jnp.pad(state_conv, ((0, 0), (HALO - (CONV_WIDTH - 1), 0), (0, 0)))
    conv_p = _conv_branch(a_p, 0, nb, s_p, ctx_p, cw, cb, lg, lb, wco, _tile(s_p, 256))
    conv_s = _conv_branch(a_s, 0, db, s_n, ctx_s, cw, cb, lg, lb, wco, s_n)

    k_p3, v_p3 = k_p.reshape(nb, s_p, -1), v_p.reshape(nb, s_p, -1)
    ki_p3 = kw_p.reshape(nb, s_p, IDX_DIM)
    k_s3, v_s3 = k_s.reshape(db, s_n, -1), v_s.reshape(db, s_n, -1)
    ki_s3 = kw_s[:, :IDX_DIM].reshape(db, s_n, IDX_DIM)

    tk_p = _tile(s_p, K_TILE_T)
    lp_p = -(-(FRONT + s_p) // tk_p) * tk_p
    front = lambda rows: jnp.pad(rows, ((FRONT - N_META, 0), (0, 0))).astype(BF16)
    per_head = lambda t: t.reshape(FRONT, N_KV_HEADS, HEAD_DIM).transpose(1, 0, 2)
    rep = lambda t: jnp.broadcast_to(t[None], (nb,) + t.shape)
    tail = lp_p - FRONT - s_p
    kp = jnp.concatenate([rep(per_head(front(k_m))), kh_mid, jnp.zeros((nb, N_KV_HEADS, tail, HEAD_DIM), BF16)], axis=2)
    vt_front = jnp.concatenate([per_head(front(v_m)).transpose(0, 2, 1), jnp.ones((N_KV_HEADS, 1, FRONT), BF16),
                                jnp.zeros((N_KV_HEADS, VT_ROWS - HEAD_DIM - 1, FRONT), BF16)], axis=1)
    vtp = jnp.concatenate([rep(vt_front), vt_mid, jnp.zeros((nb, N_KV_HEADS, VT_ROWS, tail), BF16)], axis=3)
    kip = jnp.concatenate([rep(front(ki_m)), ki_mid, jnp.zeros((nb, tail, IDX_DIM), BF16)], axis=1)
    attn_p = _dsa_t(qh_p, qih_p, wt_p, kp, vtp, kip, tk=tk_p, k_sel=min(TOPK_MAX, s_p // 4),
                    n_valid_end=FRONT + s_p)

    def heads(t, b_, s_, nh):
        return t.reshape(b_, s_, nh, -1).transpose(0, 2, 1, 3)

    tk_s = K_TILE
    lp_s = -(-(FRONT + past + s_n) // tk_s) * tk_s
    ck = cache_k.reshape(db, past, -1)
    cv = cache_v.reshape(db, past, -1)
    ks, vs, kis = _attention_inputs(k_m, v_m, ki_m, [ck, k_s3], [cv, v_s3], [cache_idx_k, ki_s3], lp_s, False)
    attn_s = _dsa(heads(q_s, db, s_n, N_HEADS), heads(qi_s, db, s_n, IDX_HEADS), kw_s, 0, ks, vs, kis,
                  tq=s_n, tk=tk_s, k_sel=min(TOPK_MAX, (past + s_n) // 4), q_off=past,
                  n_valid_end=FRONT + past + s_n)

    wr = jnp.pad(w_router, ((0, 0), (0, LANES - N_EXPERTS)))
    br = jnp.pad(b_router, (0, LANES - N_EXPERTS)).reshape(1, LANES)
    x1, hn, ids, gts = _merge(xp, xs, conv_p, conv_s, attn_p, attn_s, gates_p, gates_s, w_attn_out.astype(BF16),
                              w_out.astype(BF16), g_norm_ffn.reshape(1, -1), wr, br, tr)

    tm = MOE_TILE
    src_row, dst_row, tile_expert, base, nval, n_real = _route(ids[:, :TOP_K_EXPERTS], n, tm)
    yslots = _moe(hn, src_row, dst_row, tile_expert, base, nval, n_real, w1, b1.reshape(N_EXPERTS, 1, D_FF),
                  w3, b3.reshape(N_EXPERTS, 1, D_FF), w2, b2.reshape(N_EXPERTS, 1, D_MODEL), n, tm)
    g_fin = g_norm_final.reshape(1, -1)
    y_prompt = _final(x1, yslots, gts, g_fin, tr, 0, rows_p).reshape(nb, s_p, D_MODEL)
    y_sample = _final(x1, yslots, gts, g_fin, tr, rows_p, rows_s).reshape(db, s_n, D_MODEL)

    lead = lambda m: jnp.broadcast_to(m[None], (nb,) + m.shape)
    k_prompt = jnp.concatenate([lead(k_m), k_p3], axis=1).reshape(nb, N_META + s_p, N_KV_HEADS, HEAD_DIM)
    v_prompt = jnp.concatenate([lead(v_m), v_p3], axis=1).reshape(nb, N_META + s_p, N_KV_HEADS, HEAD_DIM)
    idxk_prompt = jnp.concatenate([lead(ki_m), ki_p3], axis=1)
    a_p3 = a_p.reshape(nb, s_p, CONV_CH)
    a_s3 = a_s.reshape(db, s_n, CONV_CH)
    keep = CONV_WIDTH - 1
    conv_state_p = jnp.concatenate([lead(a_m), a_p3], axis=1)[:, N_META + s_p - keep:]
    conv_state_s = jnp.concatenate([state_conv, a_s3], axis=1)[:, s_n:]
    return (y_prompt, y_sample, k_prompt, v_prompt, idxk_prompt, conv_state_p,
            k_s3.reshape(db, s_n, N_KV_HEADS, HEAD_DIM), v_s3.reshape(db, s_n, N_KV_HEADS, HEAD_DIM), ki_s3,
            conv_state_s)
```
